```python
import math
import jax, jax.numpy as jnp
from jax import lax
import numpy as np

D_MODEL = 2048
BATCH = 4
SEQ = 2048
DEPTH = 2

GRID_W = 64
CTX_LEN = 256
N_MIXERS = 2
EPS = 1e-6

S5_GROUP = 16
S5_GROUPS = D_MODEL // S5_GROUP
S5_STATE = 64
S5_DT_MIN = 1e-3
S5_DT_MAX = 1e-1

SSD_EXPAND = 2
SSD_INNER = SSD_EXPAND * D_MODEL
SSD_HEADDIM = 64
SSD_HEADS = SSD_INNER // SSD_HEADDIM
SSD_STATE = 128
SSD_GROUPS = 8
SSD_CONV = 5
SSD_CHUNK = 128
SSD_CONV_DIM = SSD_INNER + 2 * SSD_GROUPS * SSD_STATE
SSD_PROJ = SSD_INNER + SSD_CONV_DIM + 2 * SSD_HEADS

N_EXPERTS = 64
TOP_K = 8
EXPERT_FF = 512
SHARED_FF = 512
N_EXPERT_GROUPS = 8
TOPK_GROUPS = 4
ROUTED_SCALE = 2.5
MOE_BLOCK = 128

kernel_name = 'hybrid_s5_ssd_moe_prefix_dit'


def _rmsnorm(h, w):
    hf = h.astype(jnp.float32)
    hf = hf * lax.rsqrt(jnp.mean(hf * hf, axis=-1, keepdims=True) + EPS)
    return (hf * w.astype(jnp.float32)).astype(h.dtype)


def _to_column_major(h):
    bsz, l, d = h.shape
    rows = l // GRID_W
    return h.reshape(bsz, rows, GRID_W, d).transpose(0, 2, 1, 3).reshape(bsz, l, d)


def _to_row_major(h):
    bsz, l, d = h.shape
    rows = l // GRID_W
    return h.reshape(bsz, GRID_W, rows, d).transpose(0, 2, 1, 3).reshape(bsz, l, d)


def _s5_discretize(lam_re, lam_im, log_step, b_re, b_im):
    f32 = jnp.float32
    lam_re, lam_im, b_re, b_im = (t.astype(f32) for t in (lam_re, lam_im, b_re, b_im))
    dt = jnp.exp(log_step.astype(f32))[:, None]
    mag = jnp.exp(lam_re * dt)
    abar_re = mag * jnp.cos(lam_im * dt)
    abar_im = mag * jnp.sin(lam_im * dt)
    num_re = abar_re - 1.0
    num_im = abar_im
    den = lam_re * lam_re + lam_im * lam_im
    f_re = (num_re * lam_re + num_im * lam_im) / den
    f_im = (num_im * lam_re - num_re * lam_im) / den
    bbar_re = f_re[..., None] * b_re - f_im[..., None] * b_im
    bbar_im = f_re[..., None] * b_im + f_im[..., None] * b_re
    return abar_re, abar_im, bbar_re, bbar_im


def _cplx_combine(e1, e2):
    a1r, a1i, b1r, b1i = e1
    a2r, a2i, b2r, b2i = e2
    ar = a1r * a2r - a1i * a2i
    ai = a1r * a2i + a1i * a2r
    br = a2r * b1r - a2i * b1i + b2r
    bi = a2r * b1i + a2i * b1r + b2i
    return ar, ai, br, bi


def _s5_run(u, s0_re, s0_im, abar_re, abar_im, bbar_re, bbar_im, c_re, c_im, reverse, with_output):
    if reverse:
        u = u[::-1]
    bu_re = jnp.einsum('lbgk,gpk->lbgp', u, bbar_re)
    bu_im = jnp.einsum('lbgk,gpk->lbgp', u, bbar_im)
    bu_re = bu_re.at[0].add(abar_re * s0_re - abar_im * s0_im)
    bu_im = bu_im.at[0].add(abar_re * s0_im + abar_im * s0_re)
    shape = (u.shape[0], 1) + abar_re.shape
    elems = (jnp.broadcast_to(abar_re, shape), jnp.broadcast_to(abar_im, shape), bu_re, bu_im)
    _, _, s_re, s_im = lax.associative_scan(_cplx_combine, elems, axis=0)
    y = None
    if with_output:
        y = (jnp.einsum('lbgp,gkp->lbgk', s_re, c_re.astype(jnp.float32))
             - jnp.einsum('lbgp,gkp->lbgk', s_im, c_im.astype(jnp.float32)))
        if reverse:
            y = y[::-1]
    return y, s_re[-1], s_im[-1]


def _s5_mixer(h_ctx, h_lat, lam_re, lam_im, log_step, b_re, b_im, c_re, c_im, d_skip, glu_w, glu_b, need_ctx):
    dtype = h_lat.dtype
    bsz = h_lat.shape[0]

    def to_groups(h):
        return jnp.swapaxes(h.astype(jnp.float32), 0, 1).reshape(h.shape[1], bsz, S5_GROUPS, S5_GROUP)

    def from_groups(y):
        return jnp.swapaxes(y, 0, 1).reshape(bsz, y.shape[0], D_MODEL)

    u_c, u_l = to_groups(h_ctx), to_groups(h_lat)
    skip = d_skip.astype(jnp.float32).reshape(S5_GROUPS, S5_GROUP)
    y_l = skip * u_l
    y_c = skip * u_c if need_ctx else None
    zero = jnp.zeros((bsz, S5_GROUPS, S5_STATE), jnp.float32)
    for d in range(2):
        abr, abi, bbr, bbi = _s5_discretize(lam_re[d], lam_im[d], log_step[d], b_re[d], b_im[d])
        yc_d, fr, fi = _s5_run(u_c, zero, zero, abr, abi, bbr, bbi, c_re[d], c_im[d], d == 1, need_ctx)
        yl_d, _, _ = _s5_run(u_l, fr, fi, abr, abi, bbr, bbi, c_re[d], c_im[d], d == 1, True)
        y_l = y_l + yl_d
        if need_ctx:
            y_c = y_c + yc_d

    def glu(y):
        z = jax.nn.gelu(from_groups(y)).astype(dtype) @ glu_w + glu_b
        out, gate = jnp.split(z, 2, axis=-1)
        return out * jax.nn.sigmoid(gate)

    return (glu(y_c) if need_ctx else None), glu(y_l)


def _centred_depthwise_conv(u, w, b):
    ch = u.shape[-1]
    pad = w.shape[0] // 2
    y = lax.conv_general_dilated(u, w[:, None, :], window_strides=(1,), padding=[(pad, pad)],
                                 dimension_numbers=('NWC', 'WIO', 'NWC'), feature_group_count=ch)
    return y + b


def _ssd_project(h, in_w, conv_w, conv_b, dt_bias):
    bsz, l, _ = h.shape
    f32 = jnp.float32
    zxbcdt = h @ in_w
    z, xbc, dt_raw = jnp.split(zxbcdt, [SSD_INNER, SSD_INNER + SSD_CONV_DIM], axis=-1)
    xbc = jax.nn.silu(_centred_depthwise_conv(xbc, conv_w, conv_b))
    xs, b_in, c_in = jnp.split(xbc, [SSD_INNER, SSD_INNER + SSD_GROUPS * SSD_STATE], axis=-1)
    xs = xs.reshape(bsz, l, SSD_HEADS, SSD_HEADDIM).astype(f32)
    b_in = b_in.reshape(bsz, l, SSD_GROUPS, SSD_STATE).astype(f32)
    c_in = c_in.reshape(bsz, l, SSD_GROUPS, SSD_STATE).astype(f32)
    dt = jax.nn.softplus(dt_raw.reshape(bsz, l, 2, SSD_HEADS).astype(f32) + dt_bias.astype(f32))
    return z, xs, b_in, c_in, dt


def _ssd_chunked(x, dt, a, b_in, c_in, s0):
    bsz, l, h, p = x.shape
    g, n = b_in.shape[2], b_in.shape[3]
    r = h // g
    q = SSD_CHUNK
    nc = l // q
    xd = (x * dt[..., None]).reshape(bsz, nc, q, g, r, p)
    log_a = (dt * a).reshape(bsz, nc, q, g, r)
    a_cs = jnp.cumsum(jnp.moveaxis(log_a, 2, -1), axis=-1)
    bc = b_in.reshape(bsz, nc, q, g, n)
    cc = c_in.reshape(bsz, nc, q, g, n)
    lower_tri = jnp.tril(jnp.ones((q, q), dtype=bool))
    seg = a_cs[..., :, None] - a_cs[..., None, :]
    decay_qs = jnp.exp(jnp.where(lower_tri, seg, -jnp.inf))
    cb = jnp.einsum('bcqgn,bcsgn->bcgqs', cc, bc)
    y_diag = jnp.einsum('bcgrqs,bcsgrp->bcqgrp', cb[:, :, :, None] * decay_qs, xd)
    decay_to_end = jnp.exp(a_cs[..., -1:] - a_cs)
    chunk_states = jnp.einsum('bcsgn,bcgrs,bcsgrp->bcgrpn', bc, decay_to_end, xd)
    chunk_decay = jnp.exp(a_cs[..., -1])

    def step(state, inp):
        dec, new = inp
        return state * dec[..., None, None] + new, state

    final, states_in = lax.scan(step, s0, (jnp.moveaxis(chunk_decay, 1, 0), jnp.moveaxis(chunk_states, 1, 0)))
    states_in = jnp.moveaxis(states_in, 0, 1)
    y_off = jnp.einsum('bcqgn,bcgrpn,bcgrq->bcqgrp', cc, states_in, jnp.exp(a_cs))
    y = (y_diag + y_off).reshape(bsz, l, h, p)
    return y, final


def _ssd_direction(x, dt, a, b_in, c_in, s0, reverse):
    if reverse:
        x, dt, b_in, c_in = (jnp.flip(t, axis=1) for t in (x, dt, b_in, c_in))
    y, final = _ssd_chunked(x, dt, a, b_in, c_in, s0)
    if reverse:
        y = jnp.flip(y, axis=1)
    return y, final


def _gated_group_rmsnorm(y, z, w):
    gshape = y.shape[:-1] + (SSD_GROUPS, SSD_INNER // SSD_GROUPS)
    v = (y * jax.nn.silu(z.astype(jnp.float32))).reshape(gshape)
    v = v * lax.rsqrt(jnp.mean(v * v, axis=-1, keepdims=True) + EPS)
    return v.reshape(y.shape) * w.astype(jnp.float32)


def _ssd_mixer(h_ctx, h_lat, in_w, conv_w, conv_b, dt_bias, a_log, d_skip, norm_w, out_w, need_ctx):
    dtype = h_lat.dtype
    bsz = h_lat.shape[0]
    z_c, x_c, b_c, c_c, dt_c = _ssd_project(h_ctx, in_w, conv_w, conv_b, dt_bias)
    z_l, x_l, b_l, c_l, dt_l = _ssd_project(h_lat, in_w, conv_w, conv_b, dt_bias)
    a = -jnp.exp(a_log.astype(jnp.float32))
    skip = d_skip.astype(jnp.float32)[:, None]
    s0 = jnp.zeros((bsz, SSD_GROUPS, SSD_HEADS // SSD_GROUPS, SSD_HEADDIM, SSD_STATE), jnp.float32)
    y_l = x_l * skip
    y_c = x_c * skip if need_ctx else None
    for d in range(2):
        yc_d, s_ctx = _ssd_direction(x_c, dt_c[:, :, d], a[d], b_c, c_c, s0, d == 1)
        yl_d, _ = _ssd_direction(x_l, dt_l[:, :, d], a[d], b_l, c_l, s_ctx, d == 1)
        y_l = y_l + yl_d
        if need_ctx:
            y_c = y_c + yc_d

    def finish(y, z):
        y = y.reshape(y.shape[:2] + (SSD_INNER,))
        return _gated_group_rmsnorm(y, z, norm_w).astype(dtype) @ out_w

    return (finish(y_c, z_c) if need_ctx else None), finish(y_l, z_l)


def _moe(h, router_w, router_bias, w_gate, w_up, w_down, sw_gate, sw_up, sw_down):
    n, d = h.shape
    scores = jax.nn.sigmoid((h @ router_w).astype(jnp.float32))
    biased = scores + router_bias.astype(jnp.float32)
    grp = biased.reshape(n, N_EXPERT_GROUPS, N_EXPERTS // N_EXPERT_GROUPS)
    grp_score = lax.top_k(grp, 2)[0].sum(-1)
    _, grp_idx = lax.top_k(grp_score, TOPK_GROUPS)
    grp_mask = jax.nn.one_hot(grp_idx, N_EXPERT_GROUPS, dtype=jnp.float32).sum(1)
    expert_mask = jnp.repeat(grp_mask, N_EXPERTS // N_EXPERT_GROUPS, axis=1)
    _, idx = lax.top_k(jnp.where(expert_mask > 0, biased, -jnp.inf), TOP_K)
    gw = jnp.take_along_axis(scores, idx, axis=1)
    gw = gw / jnp.sum(gw, axis=-1, keepdims=True) * ROUTED_SCALE

    nk = n * TOP_K
    flat_e = idx.reshape(-1)
    flat_tok = jnp.arange(nk, dtype=jnp.int32) // TOP_K
    flat_w = gw.reshape(-1)
    order = jnp.argsort(flat_e)
    sorted_e = flat_e[order]
    counts = jnp.bincount(flat_e, length=N_EXPERTS)
    padded = (counts + MOE_BLOCK - 1) // MOE_BLOCK * MOE_BLOCK
    pad_end = jnp.cumsum(padded)
    pad_start = pad_end - padded
    start = jnp.cumsum(counts) - counts
    dest = pad_start[sorted_e] + (jnp.arange(nk) - start[sorted_e])
    cap = -(-nk // MOE_BLOCK) * MOE_BLOCK + N_EXPERTS * MOE_BLOCK
    n_blocks = cap // MOE_BLOCK
    row_tok = jnp.zeros((cap,), jnp.int32).at[dest].set(flat_tok[order])
    row_w = jnp.zeros((cap,), jnp.float32).at[dest].set(flat_w[order])
    block_e = jnp.minimum(jnp.searchsorted(pad_end, jnp.arange(n_blocks) * MOE_BLOCK, side='right'),
                          N_EXPERTS - 1)

    def expert_block(args):
        tok, w_rows, e = args
        xb = h[tok]
        hb = jax.nn.silu(xb @ w_gate[e]) * (xb @ w_up[e])
        return (hb @ w_down[e]) * w_rows[:, None].astype(xb.dtype)

    yb = lax.map(expert_block, (row_tok.reshape(n_blocks, MOE_BLOCK), row_w.reshape(n_blocks, MOE_BLOCK), block_e))
    routed = jnp.zeros_like(h).at[row_tok].add(yb.reshape(cap, d))
    shared = (jax.nn.silu(h @ sw_gate) * (h @ sw_up)) @ sw_down
    return routed + shared


def setup_inputs(seed: int = 0) -> dict:
    key = jax.random.key(seed)
    ks = iter(jax.random.split(key, 48))
    f32 = jnp.float32

    def nrm(shape, scale):
        return jax.random.normal(next(ks), shape, f32) * scale

    def uni(shape, lo, hi):
        return jax.random.uniform(next(ks), shape, f32, lo, hi)

    na = (DEPTH + N_MIXERS - 1) // N_MIXERS
    nb = DEPTH // N_MIXERS
    D = D_MODEL
    dt_ssd = jnp.exp(uni((nb, 2, SSD_HEADS), math.log(1e-3), math.log(1e-1)))
    return {
        'x': nrm((BATCH, SEQ, D), 1.0),
        'c': nrm((BATCH, D), 1.0),
        'ctx': nrm((BATCH, CTX_LEN, D), 1.0),
        'c_ctx': nrm((D,), 1.0),
        'ada_w': nrm((DEPTH, D, 6 * D), 0.5 * D ** -0.5),
        'ada_b': nrm((DEPTH, 6 * D), 0.01),
        'norm1_w': 1.0 + nrm((DEPTH, D), 0.02),
        'norm2_w': 1.0 + nrm((DEPTH, D), 0.02),
        's5_lambda_re': -0.5 + nrm((na, 2, S5_GROUPS, S5_STATE), 0.01),
        's5_lambda_im': jnp.pi * jnp.arange(S5_STATE, dtype=f32) + nrm((na, 2, S5_GROUPS, S5_STATE), 0.01),
        's5_log_step': uni((na, 2, S5_GROUPS), math.log(S5_DT_MIN), math.log(S5_DT_MAX)),
        's5_b_re': nrm((na, 2, S5_GROUPS, S5_STATE, S5_GROUP), (2 * S5_GROUP) ** -0.5),
        's5_b_im': nrm((na, 2, S5_GROUPS, S5_STATE, S5_GROUP), (2 * S5_GROUP) ** -0.5),
        's5_c_re': nrm((na, 2, S5_GROUPS, S5_GROUP, S5_STATE), (2 * S5_STATE) ** -0.5),
        's5_c_im': nrm((na, 2, S5_GROUPS, S5_GROUP, S5_STATE), (2 * S5_STATE) ** -0.5),
        's5_d': nrm((na, D), 1.0),
        's5_glu_w': nrm((na, D, 2 * D), D ** -0.5),
        's5_glu_b': nrm((na, 2 * D), 0.01),
        'ssd_in_w': nrm((nb, D, SSD_PROJ), D ** -0.5),
        'ssd_conv_w': nrm((nb, SSD_CONV, SSD_CONV_DIM), SSD_CONV ** -0.5),
        'ssd_conv_b': nrm((nb, SSD_CONV_DIM), 0.01),
        'ssd_dt_bias': dt_ssd + jnp.log(-jnp.expm1(-dt_ssd)),
        'ssd_a_log': jnp.log(uni((nb, 2, SSD_HEADS), 1.0, 16.0)),
        'ssd_d': 1.0 + nrm((nb, SSD_HEADS), 0.1),
        'ssd_norm_w': 1.0 + nrm((nb, SSD_INNER), 0.02),
        'ssd_out_w': nrm((nb, SSD_INNER, D), SSD_INNER ** -0.5),
        'moe_router_w': nrm((DEPTH, D, N_EXPERTS), D ** -0.5),
        'moe_router_bias': nrm((DEPTH, N_EXPERTS), 0.01),
        'moe_w_gate': nrm((DEPTH, N_EXPERTS, D, EXPERT_FF), D ** -0.5),
        'moe_w_up': nrm((DEPTH, N_EXPERTS, D, EXPERT_FF), D ** -0.5),
        'moe_w_down': nrm((DEPTH, N_EXPERTS, EXPERT_FF, D), EXPERT_FF ** -0.5),
        'shared_w_gate': nrm((DEPTH, D, SHARED_FF), D ** -0.5),
        'shared_w_up': nrm((DEPTH, D, SHARED_FF), D ** -0.5),
        'shared_w_down': nrm((DEPTH, SHARED_FF, D), SHARED_FF ** -0.5),
        'final_norm_w': 1.0 + nrm((D,), 0.02),
    }


def reference(x, c, ctx, c_ctx, ada_w, ada_b, norm1_w, norm2_w,
              s5_lambda_re, s5_lambda_im, s5_log_step, s5_b_re, s5_b_im, s5_c_re, s5_c_im,
              s5_d, s5_glu_w, s5_glu_b,
              ssd_in_w, ssd_conv_w, ssd_conv_b, ssd_dt_bias, ssd_a_log, ssd_d, ssd_norm_w, ssd_out_w,
              moe_router_w, moe_router_bias, moe_w_gate, moe_w_up, moe_w_down,
              shared_w_gate, shared_w_up, shared_w_down, final_norm_w):
    bsz, seq, d = x.shape
    n_lat = bsz * seq
    silu_c = jax.nn.silu(c)
    silu_cc = jax.nn.silu(c_ctx)
    h_ctx = ctx
    for i in range(DEPTH):
        last = i == DEPTH - 1
        mod_l = [m[:, None, :] for m in jnp.split(silu_c @ ada_w[i] + ada_b[i], 6, axis=-1)]
        mod_c = jnp.split(silu_cc @ ada_w[i] + ada_b[i], 6, axis=-1)
        u_lat = _rmsnorm(x, norm1_w[i]) * (1.0 + mod_l[1]) + mod_l[0]
        u_ctx = _rmsnorm(h_ctx, norm1_w[i]) * (1.0 + mod_c[1]) + mod_c[0]
        j = i // N_MIXERS
        if i % N_MIXERS == 0:
            y_ctx, y_lat = _s5_mixer(u_ctx, u_lat, s5_lambda_re[j], s5_lambda_im[j], s5_log_step[j],
                                     s5_b_re[j], s5_b_im[j], s5_c_re[j], s5_c_im[j], s5_d[j],
                                     s5_glu_w[j], s5_glu_b[j], not last)
        else:
            y_ctx, y_lat = _ssd_mixer(u_ctx, _to_column_major(u_lat), ssd_in_w[j], ssd_conv_w[j],
                                      ssd_conv_b[j], ssd_dt_bias[j], ssd_a_log[j], ssd_d[j],
                                      ssd_norm_w[j], ssd_out_w[j], not last)
            y_lat = _to_row_major(y_lat)
        x = x + mod_l[2] * y_lat
        v_lat = _rmsnorm(x, norm2_w[i]) * (1.0 + mod_l[4]) + mod_l[3]
        moe_params = (moe_router_w[i], moe_router_bias[i], moe_w_gate[i], moe_w_up[i], moe_w_down[i],
                      shared_w_gate[i], shared_w_up[i], shared_w_down[i])
        if last:
            x = x + mod_l[5] * _moe(v_lat.reshape(-1, d), *moe_params).reshape(bsz, seq, d)
        else:
            h_ctx = h_ctx + mod_c[2] * y_ctx
            v_ctx = _rmsnorm(h_ctx, norm2_w[i]) * (1.0 + mod_c[4]) + mod_c[3]
            tokens = jnp.concatenate([v_lat.reshape(-1, d), v_ctx.reshape(-1, d)], axis=0)
            out = _moe(tokens, *moe_params)
            x = x + mod_l[5] * out[:n_lat].reshape(bsz, seq, d)
            h_ctx = h_ctx + mod_c[5] * out[n_lat:].reshape(h_ctx.shape)
    return _rmsnorm(x, final_norm_w)
```

```python
import functools
import math

import jax
import jax.numpy as jnp
from jax import lax
from jax.experimental import pallas as pl
from jax.experimental.pallas import tpu as pltpu

F32 = jnp.float32
BF16 = jnp.bfloat16

GRID_W = 64
EPS = 1e-6
S5_GROUP = 16
S5_STATE = 64
SSD_HEADDIM = 64
SSD_STATE = 128
SSD_GROUPS = 8
SSD_CONV = 5
SSD_CHUNK = 128
N_EXPERTS = 64
TOP_K = 8
N_EXPERT_GROUPS = 8
TOPK_GROUPS = 4
ROUTED_SCALE = 2.5
MOE_BLOCK = 128

VMEM_LIMIT_BYTES = 56 * 1024 * 1024
LANES = 128
SUBLANES = 8


def _cparams(n_axes):
    return pltpu.CompilerParams(
        dimension_semantics=("arbitrary",) * n_axes,
        vmem_limit_bytes=VMEM_LIMIT_BYTES)


def _silu(v):
    return v * jax.nn.sigmoid(v)


def _dot(a, b):
    return jnp.dot(a, b, preferred_element_type=F32)


def _split3(a):
    hi = a.astype(BF16)
    r1 = a - hi.astype(F32)
    mid = r1.astype(BF16)
    lo = (r1 - mid.astype(F32)).astype(BF16)
    return hi, mid, lo


def _dot_exact_rhs(a, sel):
    hi, mid, lo = _split3(a)
    return _dot(hi, sel) + _dot(mid, sel) + _dot(lo, sel)


def _dot_exact_lhs(sel, a):
    hi, mid, lo = _split3(a)
    return _dot(sel, hi) + _dot(sel, mid) + _dot(sel, lo)


def _ada_kernel(c_ref, w_ref, b_ref, o_ref):
    c = _silu(c_ref[...])
    o_ref[0] = _dot(c.astype(BF16), w_ref[0].astype(BF16)) + b_ref[0]


def _ada(cond, ada_w, ada_b):
    depth, d, n = ada_w.shape
    tn = 1024
    rows = cond.shape[0]
    return pl.pallas_call(
        _ada_kernel,
        out_shape=jax.ShapeDtypeStruct((depth, rows, n), F32),
        grid=(depth, n // tn),
        in_specs=[pl.BlockSpec((rows, d), lambda l, j: (0, 0)),
                  pl.BlockSpec((1, d, tn), lambda l, j: (l, 0, j)),
                  pl.BlockSpec((1, 1, tn), lambda l, j: (l, 0, j))],
        out_specs=pl.BlockSpec((1, rows, tn), lambda l, j: (l, 0, j)),
        compiler_params=_cparams(2),
        name="ada",
    )(cond, ada_w, ada_b.reshape(depth, 1, n))


def _mm_kernel(*refs, n_w, has_bias, epilogue):
    x_ref = refs[0]
    w_refs = refs[1:1 + n_w]
    pos = 1 + n_w
    b_refs = refs[pos:pos + n_w] if has_bias else ()
    pos += n_w if has_bias else 0
    o_ref = refs[pos]
    wbf_refs = refs[pos + 1:pos + 1 + n_w]

    @pl.when(pl.program_id(1) == 0)
    def _():
        for w_ref, wbf in zip(w_refs, wbf_refs):
            wbf[...] = w_ref[...].astype(BF16)

    x = x_ref[...]
    zs = []
    for k in range(n_w):
        z = _dot(x, wbf_refs[k][...])
        if has_bias:
            z = z + b_refs[k][...]
        zs.append(z)
    if epilogue is None:
        out = zs[0]
    elif epilogue == "softplus":
        out = jax.nn.softplus(zs[0])
    elif epilogue == "glu":
        out = zs[0] * jax.nn.sigmoid(zs[1])
    elif epilogue == "swiglu":
        out = _silu(zs[0]) * zs[1]
    o_ref[...] = out.astype(o_ref.dtype)


def _matmul(x, ws, col_offsets, n_out, *, tn, tm, biases=None, epilogue=None,
            out_dtype=F32, name="matmul"):
    m, k = x.shape
    n_w = len(ws)
    has_bias = biases is not None
    in_specs = [pl.BlockSpec((tm, k), lambda j, i: (i, 0))]
    for off in col_offsets:
        in_specs.append(pl.BlockSpec((k, tn), lambda j, i, off=off: (0, j + off)))
    args = [x] + list(ws)
    if has_bias:
        for off in col_offsets:
            in_specs.append(pl.BlockSpec((1, tn), lambda j, i, off=off: (0, j + off)))
        args += [b.reshape(1, -1) for b in biases]
    return pl.pallas_call(
        functools.partial(_mm_kernel, n_w=n_w, has_bias=has_bias, epilogue=epilogue),
        out_shape=jax.ShapeDtypeStruct((m, n_out), out_dtype),
        grid=(n_out // tn, m // tm),
        in_specs=in_specs,
        out_specs=pl.BlockSpec((tm, tn), lambda j, i: (i, j)),
        scratch_shapes=[pltpu.VMEM((k, tn), BF16) for _ in range(n_w)],
        compiler_params=_cparams(2),
        name=name,
    )(*args)


SLABS = 8
ROW_TILE = 256


def _get_piece(ref, mode, j, rows, d):
    if mode == "slab":
        return ref[:, j * d:(j + 1) * d]
    return ref[j * rows:(j + 1) * rows, :]


def _put_piece(ref, mode, j, rows, d, val):
    if mode == "slab":
        ref[:, j * d:(j + 1) * d] = val.astype(ref.dtype)
    else:
        ref[j * rows:(j + 1) * rows, :] = val.astype(ref.dtype)


def _resnorm_kernel(*refs, has_y, x_mode, y_mode, xo_mode, v_mode, modulate, rows, d):
    it = iter(refs)
    x_ref = next(it)
    y_ref = next(it) if has_y else None
    g_ref = next(it) if has_y else None
    nw_ref = next(it)
    sh_ref = next(it) if modulate else None
    sc_ref = next(it) if modulate else None
    xo_ref = next(it) if has_y else None
    v_ref = next(it)
    nw = nw_ref[...]
    for j in range(SLABS):
        x = _get_piece(x_ref, x_mode, j, rows, d)
        if has_y:
            y = _get_piece(y_ref, y_mode, j, rows, d).astype(F32)
            x = x + g_ref[0] * y
            _put_piece(xo_ref, xo_mode, j, rows, d, x)
        v = x * lax.rsqrt(jnp.mean(x * x, axis=-1, keepdims=True) + EPS) * nw
        if modulate:
            v = v * (1.0 + sc_ref[0]) + sh_ref[0]
        _put_piece(v_ref, v_mode, j, rows, d, v)


def _resnorm(x, y, gate, norm_w, shift, scale, *, n_batch, seq, x_mode="row",
             y_mode="row", xo_mode="row", v_mode="row", v_dtype=BF16, name="resnorm"):
    n, d = x.shape
    has_y = y is not None
    modulate = shift is not None
    grows = seq // GRID_W
    tiles_per_seq = seq // ROW_TILE
    slab_used = "slab" in (x_mode, y_mode, xo_mode, v_mode)
    rows = grows if slab_used else ROW_TILE // SLABS
    if slab_used:
        assert grows * SLABS == ROW_TILE

    def spec(mode):
        if mode == "slab":
            return pl.BlockSpec((grows, SLABS * d), lambda b, t: (b, t))
        return pl.BlockSpec((ROW_TILE, d), lambda b, t: (b * tiles_per_seq + t, 0))

    def view(a, mode):
        return a.reshape(n_batch * grows, GRID_W * d) if mode == "slab" else a

    vec = pl.BlockSpec((1, 1, d), lambda b, t: (b, 0, 0))
    in_specs = [spec(x_mode)]
    args = [view(x, x_mode)]
    if has_y:
        in_specs += [spec(y_mode), vec]
        args += [view(y, y_mode), gate]
    in_specs.append(pl.BlockSpec((1, d), lambda b, t: (0, 0)))
    args.append(norm_w.reshape(1, d))
    if modulate:
        in_specs += [vec, vec]
        args += [shift, scale]
    out_shape, out_specs = [], []
    if has_y:
        out_shape.append(jax.ShapeDtypeStruct(view(x, xo_mode).shape, F32))
        out_specs.append(spec(xo_mode))
    vshape = (n_batch * grows, GRID_W * d) if v_mode == "slab" else (n, d)
    out_shape.append(jax.ShapeDtypeStruct(vshape, v_dtype))
    out_specs.append(spec(v_mode))
    outs = pl.pallas_call(
        functools.partial(_resnorm_kernel, has_y=has_y, x_mode=x_mode, y_mode=y_mode,
                          xo_mode=xo_mode, v_mode=v_mode, modulate=modulate, rows=rows, d=d),
        out_shape=out_shape,
        grid=(n_batch, tiles_per_seq),
        in_specs=in_specs,
        out_specs=out_specs,
        compiler_params=_cparams(2),
        name=name,
    )(*args)
    outs = [o.reshape(n, d) for o in outs]
    return (outs[0], outs[1]) if has_y else (None, outs[0])


S5_T = 128
S5_CB = 128
S5_NS = (S5_CB // S5_GROUP) * S5_STATE


def _reverse_rows(v_bf16):
    t = v_bf16.shape[0]
    r = lax.broadcasted_iota(jnp.int32, (t, t), 0)
    c = lax.broadcasted_iota(jnp.int32, (t, t), 1)
    flip = jnp.where(r + c == t - 1, 1.0, 0.0).astype(BF16)
    return _dot(flip, v_bf16)


def _s5_prep_kernel(xf_ref, xb_ref, nw_ref, sh_ref, sc_ref, o_ref, *, nb, d):
    nw = nw_ref[...]
    for b in range(nb):
        for rev, x_ref in ((False, xf_ref), (True, xb_ref)):
            x = x_ref[b]
            v = x * lax.rsqrt(jnp.mean(x * x, axis=-1, keepdims=True) + EPS) * nw
            v = (v * (1.0 + sc_ref[b]) + sh_ref[b]).astype(BF16)
            if rev:
                v = _reverse_rows(v).astype(BF16)
            k = b + (nb if rev else 0)
            o_ref[:, k * d:(k + 1) * d] = v


def _s5_prep(x3, norm_w, shift, scale):
    nb, s, d = x3.shape
    nt = s // S5_T
    return pl.pallas_call(
        functools.partial(_s5_prep_kernel, nb=nb, d=d),
        out_shape=jax.ShapeDtypeStruct((s, 2 * nb * d), BF16),
        grid=(nt,),
        in_specs=[pl.BlockSpec((nb, S5_T, d), lambda t: (0, t, 0)),
                  pl.BlockSpec((nb, S5_T, d), lambda t: (0, nt - 1 - t, 0)),
                  pl.BlockSpec((1, d), lambda t: (0, 0)),
                  pl.BlockSpec((nb, 1, d), lambda t: (0, 0, 0)),
                  pl.BlockSpec((nb, 1, d), lambda t: (0, 0, 0))],
        out_specs=pl.BlockSpec((S5_T, 2 * nb * d), lambda t: (t, 0)),
        compiler_params=_cparams(1),
        name="s5_prep",
    )(x3, x3, norm_w.reshape(1, d), shift, scale)


def _s5_scan_kernel(u_ref, bc_ref, are_ref, aim_ref, cc_ref, s0_ref, y_ref, sf_ref,
                    bu_ref, st_ref, *, n_tiles):
    i = pl.program_id(1)
    ns = S5_NS

    @pl.when(i == 0)
    def _():
        st_ref[...] = s0_ref[0]

    u = u_ref[...].astype(F32)
    row = lax.broadcasted_iota(jnp.int32, u.shape, 0)
    is_fwd = jnp.bitwise_and(row, SUBLANES - 1) < (SUBLANES // 2)
    lhs = jnp.concatenate([jnp.where(is_fwd, u, 0.0), jnp.where(is_fwd, 0.0, u)], axis=1)
    bu_ref[...] = _dot(lhs.astype(BF16), bc_ref[0])
    a_re = are_ref[0]
    a_im = aim_ref[0]

    def step(t, carry):
        s_re, s_im = carry
        r = pl.multiple_of(t * SUBLANES, SUBLANES)
        n_re = a_re * s_re - a_im * s_im + bu_ref[pl.ds(r, SUBLANES), 0:ns]
        n_im = a_re * s_im + a_im * s_re + bu_ref[pl.ds(r, SUBLANES), ns:2 * ns]
        bu_ref[pl.ds(r, SUBLANES), 0:ns] = n_re
        bu_ref[pl.ds(r, SUBLANES), ns:2 * ns] = n_im
        return n_re, n_im

    s_re, s_im = lax.fori_loop(0, S5_T, step, (st_ref[:, 0:ns], st_ref[:, ns:2 * ns]))
    st_ref[:, 0:ns] = s_re
    st_ref[:, ns:2 * ns] = s_im
    y2 = _dot(bu_ref[...].astype(BF16), cc_ref[0])
    y_ref[...] = jnp.where(is_fwd, y2[:, 0:S5_CB], y2[:, S5_CB:2 * S5_CB])

    @pl.when(i == n_tiles - 1)
    def _():
        sf_ref[0] = st_ref[...]


def _s5_scan(u2r, bcat, a_re, a_im, ccat, s0):
    rows, d = u2r.shape
    nblk = d // S5_CB
    tr = S5_T * SUBLANES
    n_tiles = rows // tr
    ns2 = 2 * S5_NS
    return pl.pallas_call(
        functools.partial(_s5_scan_kernel, n_tiles=n_tiles),
        out_shape=(jax.ShapeDtypeStruct((rows, d), F32),
                   jax.ShapeDtypeStruct((nblk, SUBLANES, ns2), F32)),
        grid=(nblk, n_tiles),
        in_specs=[pl.BlockSpec((tr, S5_CB), lambda j, i: (i, j)),
                  pl.BlockSpec((1, 2 * S5_CB, ns2), lambda j, i: (j, 0, 0)),
                  pl.BlockSpec((1, SUBLANES, S5_NS), lambda j, i: (j, 0, 0)),
                  pl.BlockSpec((1, SUBLANES, S5_NS), lambda j, i: (j, 0, 0)),
                  pl.BlockSpec((1, ns2, 2 * S5_CB), lambda j, i: (j, 0, 0)),
                  pl.BlockSpec((1, SUBLANES, ns2), lambda j, i: (j, 0, 0))],
        out_specs=(pl.BlockSpec((tr, S5_CB), lambda j, i: (i, j)),
                   pl.BlockSpec((1, SUBLANES, ns2), lambda j, i: (j, 0, 0))),
        scratch_shapes=[pltpu.VMEM((tr, ns2), F32), pltpu.VMEM((SUBLANES, ns2), F32)],
        compiler_params=_cparams(2),
        name="s5_scan",
    )(u2r, bcat, a_re, a_im, ccat, s0)


def _s5_out_kernel(yf_ref, yb_ref, u_ref, skip_ref, o_ref):
    yb = yb_ref[...]
    hi = yb.astype(BF16)
    lo = (yb - hi.astype(F32)).astype(BF16)
    y = (skip_ref[...] * u_ref[...].astype(F32) + yf_ref[...]
         + _reverse_rows(hi) + _reverse_rows(lo))
    o_ref[...] = jax.nn.gelu(y).astype(o_ref.dtype)


def _s5_out(y2, u2, skip, nb, d):
    s = y2.shape[0]
    nt = s // S5_T
    return pl.pallas_call(
        _s5_out_kernel,
        out_shape=jax.ShapeDtypeStruct((nb * s, d), BF16),
        grid=(nb, nt),
        in_specs=[pl.BlockSpec((S5_T, d), lambda b, t: (t, b)),
                  pl.BlockSpec((S5_T, d), lambda b, t: (nt - 1 - t, nb + b)),
                  pl.BlockSpec((S5_T, d), lambda b, t: (t, b)),
                  pl.BlockSpec((1, d), lambda b, t: (0, 0))],
        out_specs=pl.BlockSpec((S5_T, d), lambda b, t: (b * nt + t, 0)),
        compiler_params=_cparams(2),
        name="s5_out",
    )(y2, y2, u2, skip.reshape(1, d))


def _s5_pack_params(lam_re, lam_im, log_step, b_re, b_im, c_re, c_im, nb):
    f32 = F32
    g = lam_re.shape[1]
    gpb = S5_CB // S5_GROUP
    nblk = g // gpb
    eye = jnp.eye(gpb, dtype=f32)
    a_re_rows, a_im_rows, b_parts, c_parts = [], [], [], []
    for dr in range(2):
        lr, li = lam_re[dr].astype(f32), lam_im[dr].astype(f32)
        br, bi = b_re[dr].astype(f32), b_im[dr].astype(f32)
        dt = jnp.exp(log_step[dr].astype(f32))[:, None]
        mag = jnp.exp(lr * dt)
        abar_re = mag * jnp.cos(li * dt)
        abar_im = mag * jnp.sin(li * dt)
        num_re = abar_re - 1.0
        num_im = abar_im
        den = lr * lr + li * li
        f_re = (num_re * lr + num_im * li) / den
        f_im = (num_im * lr - num_re * li) / den
        bbar_re = f_re[..., None] * br - f_im[..., None] * bi
        bbar_im = f_re[..., None] * bi + f_im[..., None] * br
        a_re_rows.append(jnp.broadcast_to(abar_re.reshape(nblk, 1, S5_NS), (nblk, nb, S5_NS)))
        a_im_rows.append(jnp.broadcast_to(abar_im.reshape(nblk, 1, S5_NS), (nblk, nb, S5_NS)))

        def blockdiag_in(bb):
            b4 = bb.reshape(nblk, gpb, S5_STATE, S5_GROUP)
            return jnp.einsum('jgpk,gh->jgkhp', b4, eye).reshape(nblk, S5_CB, S5_NS)

        def blockdiag_out(cc):
            c4 = cc.astype(f32).reshape(nblk, gpb, S5_GROUP, S5_STATE)
            return jnp.einsum('jgkp,gh->jgphk', c4, eye).reshape(nblk, S5_NS, S5_CB)

        b_parts.append(jnp.concatenate([blockdiag_in(bbar_re), blockdiag_in(bbar_im)], axis=2))
        c_parts.append(jnp.concatenate([blockdiag_out(c_re[dr]), -blockdiag_out(c_im[dr])], axis=1))
    a_re = jnp.concatenate(a_re_rows, axis=1)
    a_im = jnp.concatenate(a_im_rows, axis=1)
    bcat = jnp.concatenate(b_parts, axis=1).astype(BF16)
    ccat = jnp.concatenate(c_parts, axis=2).astype(BF16)
    return bcat, a_re, a_im, ccat


def _conv_kernel(x_ref, w_ref, b_ref, o_ref, pad_ref, *, seq):
    halo = SUBLANES
    zeros = jnp.zeros((halo, pad_ref.shape[1]), F32)
    pad_ref[0:halo, :] = zeros
    pad_ref[halo + seq:2 * halo + seq, :] = zeros
    pad_ref[halo:halo + seq, :] = x_ref[...]
    acc = jnp.zeros((seq, pad_ref.shape[1]), F32) + b_ref[...]
    for k in range(SSD_CONV):
        off = halo + k - SSD_CONV // 2
        acc = acc + w_ref[k:k + 1, :] * pad_ref[off:off + seq, :]
    o_ref[...] = _silu(acc).astype(o_ref.dtype)


def _conv_silu(xbc, conv_w, conv_b, seq, row_block_offset, n_seq, out_dtype=BF16):
    _, c = xbc.shape
    tc = 512
    return pl.pallas_call(
        functools.partial(_conv_kernel, seq=seq),
        out_shape=jax.ShapeDtypeStruct((n_seq * seq, c), out_dtype),
        grid=(n_seq, c // tc),
        in_specs=[pl.BlockSpec((seq, tc), lambda b, j: (b + row_block_offset, j)),
                  pl.BlockSpec((SSD_CONV, tc), lambda b, j: (0, j)),
                  pl.BlockSpec((1, tc), lambda b, j: (0, j))],
        out_specs=pl.BlockSpec((seq, tc), lambda b, j: (b, j)),
        scratch_shapes=[pltpu.VMEM((seq + 2 * SUBLANES, tc), F32)],
        compiler_params=_cparams(2),
        name="ssd_conv",
    )(xbc, conv_w, conv_b.reshape(1, c))


HPG = 8
GCH = HPG * SSD_HEADDIM
DTC = 2 * HPG


def _ssd_kernel(xl_ref, bl_ref, btl_ref, cl_ref, dtl_ref, dttl_ref,
                xc_ref, btc_ref, dtc_ref, dttc_ref,
                z_ref, acol_ref, arow_ref, skip_ref, nw_ref, o_ref,
                ht_ref, yacc_ref, *, n_lat, n_ctx):
    q = SSD_CHUNK
    r_i = lax.broadcasted_iota(jnp.int32, (q, q), 0)
    c_i = lax.broadcasted_iota(jnp.int32, (q, q), 1)
    tril = jnp.where(c_i <= r_i, 1.0, 0.0).astype(BF16)
    triu = jnp.where(r_i <= c_i, 1.0, 0.0).astype(BF16)
    lower = c_i <= r_i
    upper = c_i >= r_i
    lane = lax.broadcasted_iota(jnp.int32, (q, 2 * SSD_HEADDIM), 1)
    left = lane < SSD_HEADDIM
    e_r = lax.broadcasted_iota(jnp.int32, (DTC, GCH), 0)
    e_c = lax.broadcasted_iota(jnp.int32, (DTC, GCH), 1)
    s_r = lax.broadcasted_iota(jnp.int32, (DTC, HPG * q), 0)
    s_c = lax.broadcasted_iota(jnp.int32, (DTC, HPG * q), 1)
    a_row = arow_ref[0]
    a_col = acol_ref[0]

    def chunk(x_ref, b_ref, bt_ref, c_ref, dt_ref, dtt_ref, r0, dr, with_y, first):
        head_of_ch = lax.shift_right_logical(e_c, int(math.log2(SSD_HEADDIM)))
        head_of_col = lax.shift_right_logical(s_c, int(math.log2(q)))
        expand = jnp.where(head_of_ch + dr * HPG == e_r, 1.0, 0.0).astype(BF16)
        bcast = jnp.where(head_of_col + dr * HPG == s_r, 1.0, 0.0).astype(BF16)
        x = x_ref[pl.ds(r0, q), :].astype(F32)
        bt = bt_ref[0, :, pl.ds(r0, q)]
        dt = dt_ref[0, pl.ds(r0, q), :]
        dtt = dtt_ref[0, :, pl.ds(r0, q)]
        la = dt * a_row
        lat = dtt * a_col
        cs = _dot_exact_lhs(tril, la)
        cst = _dot_exact_rhs(lat, triu)
        csx = _dot_exact_rhs(cs, expand)
        totx = csx[q - 1:q, :]
        dtx = _dot_exact_rhs(dt, expand)
        if dr == 1:
            rk, rkt = cs - la, cst - lat
            rkx = csx - _dot_exact_rhs(la, expand)
        else:
            rk, rkt, rkx = cs, cst, csx
        xd = x * dtx
        if dr == 0:
            w_state = jnp.exp(totx - rkx)
            w_off = jnp.exp(rkx)
        else:
            w_state = jnp.exp(rkx)
            w_off = jnp.exp(totx - rkx)
        h_old = ht_ref[...]
        ht_ref[...] = h_old * jnp.exp(totx) + _dot(bt, (xd * w_state).astype(BF16))
        if not with_y:
            return
        cm = c_ref[pl.ds(r0, q), :]
        cb = _dot(cm, bt)
        rkb = _dot_exact_rhs(rk, bcast)
        xdb = xd.astype(BF16)
        pieces = []
        for pair in range(HPG // 2):
            ms = []
            for hh in (2 * pair, 2 * pair + 1):
                col = dr * HPG + hh
                colv = rkb[:, hh * q:(hh + 1) * q]
                rowv = rkt[col:col + 1, :]
                if dr == 0:
                    seg = jnp.where(lower, colv - rowv, -1e30)
                else:
                    seg = jnp.where(upper, rowv - colv, -1e30)
                ms.append((cb * jnp.exp(seg)).astype(BF16))
            xp = xdb[:, pair * 2 * SSD_HEADDIM:(pair + 1) * 2 * SSD_HEADDIM]
            zero = jnp.zeros_like(xp)
            rhs = jnp.concatenate([jnp.where(left, xp, zero), jnp.where(left, zero, xp)], axis=0)
            pieces.append(_dot(jnp.concatenate(ms, axis=1), rhs))
        y = jnp.concatenate(pieces, axis=1) + _dot(cm, h_old.astype(BF16)) * w_off
        if first:
            yacc_ref[pl.ds(r0, q), :] = y + skip_ref[...] * x
        else:
            y = yacc_ref[pl.ds(r0, q), :] + y
            v = y * _silu(z_ref[pl.ds(r0, q), :].astype(F32))
            v = v * lax.rsqrt(jnp.mean(v * v, axis=-1, keepdims=True) + EPS) * nw_ref[...]
            o_ref[pl.ds(r0, q), :] = v.astype(o_ref.dtype)

    for dr in range(2):
        ht_ref[...] = jnp.zeros_like(ht_ref)

        def ctx_body(k, _):
            kk = k if dr == 0 else n_ctx - 1 - k
            r0 = pl.multiple_of(kk * q, q)
            chunk(xc_ref, None, btc_ref, None, dtc_ref, dttc_ref, r0, dr, False, False)
            return 0

        lax.fori_loop(0, n_ctx, ctx_body, 0)

        def lat_body(k, _):
            kk = k if dr == 0 else n_lat - 1 - k
            r0 = pl.multiple_of(kk * q, q)
            chunk(xl_ref, bl_ref, btl_ref, cl_ref, dtl_ref, dttl_ref, r0, dr, True, dr == 0)
            return 0

        lax.fori_loop(0, n_lat, lat_body, 0)


def _ssd_scan(xc_all, bt_all, z, dtg, dtgt, a_col, a_row, skip, norm_w, nb, seq, ctx_len):
    n_lat, n_ctx = seq // SSD_CHUNK, ctx_len // SSD_CHUNK
    inner = SSD_GROUPS * GCH
    xoff = 0
    boff = inner // SSD_STATE
    coff = boff + SSD_GROUPS
    cb0 = nb * seq // ctx_len
    in_specs = [
        pl.BlockSpec((seq, GCH), lambda b, g: (b, g)),
        pl.BlockSpec((seq, SSD_STATE), lambda b, g: (b, boff + g)),
        pl.BlockSpec((1, SSD_STATE, seq), lambda b, g: (g, 0, b)),
        pl.BlockSpec((seq, SSD_STATE), lambda b, g: (b, coff + g)),
        pl.BlockSpec((1, seq, DTC), lambda b, g: (g, b, 0)),
        pl.BlockSpec((1, DTC, seq), lambda b, g: (g, 0, b)),
        pl.BlockSpec((ctx_len, GCH), lambda b, g: (cb0 + b, g)),
        pl.BlockSpec((1, SSD_STATE, ctx_len), lambda b, g: (g, 0, cb0 + b)),
        pl.BlockSpec((1, ctx_len, DTC), lambda b, g: (g, cb0 + b, 0)),
        pl.BlockSpec((1, DTC, ctx_len), lambda b, g: (g, 0, cb0 + b)),
        pl.BlockSpec((seq, GCH), lambda b, g: (b, g)),
        pl.BlockSpec((1, DTC, 1), lambda b, g: (g, 0, 0)),
        pl.BlockSpec((1, 1, DTC), lambda b, g: (g, 0, 0)),
        pl.BlockSpec((1, GCH), lambda b, g: (0, g)),
        pl.BlockSpec((1, GCH), lambda b, g: (0, g)),
    ]
    del xoff
    return pl.pallas_call(
        functools.partial(_ssd_kernel, n_lat=n_lat, n_ctx=n_ctx),
        out_shape=jax.ShapeDtypeStruct((nb * seq, inner), BF16),
        grid=(nb, SSD_GROUPS),
        in_specs=in_specs,
        out_specs=pl.BlockSpec((seq, GCH), lambda b, g: (b, g)),
        scratch_shapes=[pltpu.VMEM((SSD_STATE, GCH), F32), pltpu.VMEM((seq, GCH), F32)],
        compiler_params=_cparams(2),
        name="ssd_scan",
    )(xc_all, xc_all, bt_all, xc_all, dtg, dtgt, xc_all, bt_all, dtg, dtgt,
      z, a_col, a_row, skip, norm_w)


def _expert_kernel(be_ref, nu_ref, x_ref, wg_ref, wu_ref, wd_ref, rw_ref, o_ref,
                   wgb, wub, wdb):
    i = pl.program_id(0)
    prev = be_ref[jnp.maximum(i - 1, 0)]
    changed = jnp.logical_or(i == 0, be_ref[i] != prev)

    @pl.when(changed)
    def _():
        wgb[...] = wg_ref[0].astype(BF16)
        wub[...] = wu_ref[0].astype(BF16)
        wdb[...] = wd_ref[0].astype(BF16)

    @pl.when(i < nu_ref[0])
    def _():
        x = x_ref[...]
        h = _silu(_dot(x, wgb[...])) * _dot(x, wub[...])
        o_ref[...] = (_dot(h.astype(BF16), wdb[...]) * rw_ref[...]).astype(o_ref.dtype)

    @pl.when(i >= nu_ref[0])
    def _():
        o_ref[...] = jnp.zeros_like(o_ref)


def _experts(x_sorted, row_w, block_e, n_used, w_gate, w_up, w_down):
    cap, d = x_sorted.shape
    ff = w_gate.shape[2]
    n_blocks = cap // MOE_BLOCK
    grid_spec = pltpu.PrefetchScalarGridSpec(
        num_scalar_prefetch=2,
        grid=(n_blocks,),
        in_specs=[pl.BlockSpec((MOE_BLOCK, d), lambda i, be, nu: (i, 0)),
                  pl.BlockSpec((1, d, ff), lambda i, be, nu: (be[i], 0, 0)),
                  pl.BlockSpec((1, d, ff), lambda i, be, nu: (be[i], 0, 0)),
                  pl.BlockSpec((1, ff, d), lambda i, be, nu: (be[i], 0, 0)),
                  pl.BlockSpec((MOE_BLOCK, 1), lambda i, be, nu: (i, 0))],
        out_specs=pl.BlockSpec((MOE_BLOCK, d), lambda i, be, nu: (i, 0)),
        scratch_shapes=[pltpu.VMEM((d, ff), BF16), pltpu.VMEM((d, ff), BF16),
                        pltpu.VMEM((ff, d), BF16)])
    return pl.pallas_call(
        _expert_kernel,
        out_shape=jax.ShapeDtypeStruct((cap, d), F32),
        grid_spec=grid_spec,
        compiler_params=_cparams(1),
        name="moe_experts",
    )(block_e, n_used, x_sorted, w_gate, w_up, w_down, row_w.reshape(cap, 1))


def _route(logits, router_bias):
    n = logits.shape[0]
    scores = jax.nn.sigmoid(logits)
    biased = scores + router_bias.astype(F32)
    grp = biased.reshape(n, N_EXPERT_GROUPS, N_EXPERTS // N_EXPERT_GROUPS)
    grp_score = lax.top_k(grp, 2)[0].sum(-1)
    _, grp_idx = lax.top_k(grp_score, TOPK_GROUPS)
    grp_mask = jax.nn.one_hot(grp_idx, N_EXPERT_GROUPS, dtype=F32).sum(1)
    expert_mask = jnp.repeat(grp_mask, N_EXPERTS // N_EXPERT_GROUPS, axis=1)
    _, idx = lax.top_k(jnp.where(expert_mask > 0, biased, -jnp.inf), TOP_K)
    gw = jnp.take_along_axis(scores, idx, axis=1)
    gw = gw / jnp.sum(gw, axis=-1, keepdims=True) * ROUTED_SCALE
    return idx, gw


def _moe(v, router_w, router_bias, w_gate, w_up, w_down, sw_gate, sw_up, sw_down):
    n, d = v.shape
    logits = _matmul(v, [jnp.pad(router_w, ((0, 0), (0, LANES - N_EXPERTS)))], [0], LANES,
                     tn=LANES, tm=512, name="router")[:, :N_EXPERTS]
    idx, gw = _route(logits, router_bias)
    nk = n * TOP_K
    flat_e = idx.reshape(-1)
    flat_tok = jnp.arange(nk, dtype=jnp.int32) // TOP_K
    flat_w = gw.reshape(-1)
    order = jnp.argsort(flat_e)
    sorted_e = flat_e[order]
    counts = jnp.bincount(flat_e, length=N_EXPERTS)
    padded = (counts + MOE_BLOCK - 1) // MOE_BLOCK * MOE_BLOCK
    pad_end = jnp.cumsum(padded)
    pad_start = pad_end - padded
    start = jnp.cumsum(counts) - counts
    dest = pad_start[sorted_e] + (jnp.arange(nk) - start[sorted_e])
    cap = -(-nk // MOE_BLOCK) * MOE_BLOCK + N_EXPERTS * MOE_BLOCK
    n_blocks = cap // MOE_BLOCK
    row_tok = jnp.zeros((cap,), jnp.int32).at[dest].set(flat_tok[order])
    row_w = jnp.zeros((cap,), F32).at[dest].set(flat_w[order])
    block_e = jnp.minimum(jnp.searchsorted(pad_end, jnp.arange(n_blocks) * MOE_BLOCK, side='right'),
                          N_EXPERTS - 1).astype(jnp.int32)
    n_used = (pad_end[-1] // MOE_BLOCK).astype(jnp.int32).reshape(1)
    x_sorted = v[row_tok]
    yb = _experts(x_sorted, row_w, block_e, n_used, w_gate, w_up, w_down)
    routed = jnp.zeros((n, d), F32).at[row_tok].add(yb)
    hs = _matmul(v, [sw_gate, sw_up], [0, 0], sw_gate.shape[1], tn=sw_gate.shape[1], tm=512,
                 epilogue="swiglu", out_dtype=BF16, name="shared_up")
    shared = _matmul(hs, [sw_down], [0], d, tn=1024, tm=512, name="shared_down")
    return routed + shared


def kernel(x, c, ctx, c_ctx, ada_w, ada_b, norm1_w, norm2_w, s5_lambda_re, s5_lambda_im, s5_log_step, s5_b_re, s5_b_im, s5_c_re, s5_c_im, s5_d, s5_glu_w, s5_glu_b, ssd_in_w, ssd_conv_w, ssd_conv_b, ssd_dt_bias, ssd_a_log, ssd_d, ssd_norm_w, ssd_out_w, moe_router_w, moe_router_bias, moe_w_gate, moe_w_up, moe_w_down, shared_w_gate, shared_w_up, shared_w_down, final_norm_w):
    nb, seq, d = x.shape
    ctx_len = ctx.shape[1]
    n_lat = nb * seq
    n_ctx = nb * ctx_len

    cond = jnp.concatenate([c, c_ctx[None, :], jnp.zeros((SUBLANES - nb - 1, d), F32)], axis=0)
    mods = _ada(cond, ada_w, ada_b)

    def mod_vecs(layer, k):
        m = mods[layer, :, k * d:(k + 1) * d]
        lat = m[:nb].reshape(nb, 1, d)
        cx = jnp.broadcast_to(m[nb].reshape(1, 1, d), (nb, 1, d))
        return lat, cx

    x_lat = x.reshape(n_lat, d)
    x_ctx = ctx.reshape(n_ctx, d)

    sh_l, sh_c = mod_vecs(0, 0)
    sc_l, sc_c = mod_vecs(0, 1)
    u2_c = _s5_prep(ctx, norm1_w[0], sh_c, sc_c)
    u2_l = _s5_prep(x, norm1_w[0], sh_l, sc_l)
    bcat, a_re, a_im, ccat = _s5_pack_params(
        s5_lambda_re[0], s5_lambda_im[0], s5_log_step[0], s5_b_re[0], s5_b_im[0],
        s5_c_re[0], s5_c_im[0], nb)
    nblk = d // S5_CB
    s0 = jnp.zeros((nblk, SUBLANES, 2 * S5_NS), F32)
    y2_c, s_ctx = _s5_scan(u2_c.reshape(ctx_len * 2 * nb, d), bcat, a_re, a_im, ccat, s0)
    y2_l, _ = _s5_scan(u2_l.reshape(seq * 2 * nb, d), bcat, a_re, a_im, ccat, s_ctx)
    g_l = _s5_out(y2_l.reshape(seq, 2 * nb * d), u2_l, s5_d[0], nb, d)
    g_c = _s5_out(y2_c.reshape(ctx_len, 2 * nb * d), u2_c, s5_d[0], nb, d)
    g_all = jnp.concatenate([g_l, g_c], axis=0)
    half = s5_glu_w.shape[2] // 2
    tn = 1024
    glu = _matmul(g_all, [s5_glu_w[0], s5_glu_w[0]], [0, half // tn], half, tn=tn, tm=512,
                  biases=[s5_glu_b[0], s5_glu_b[0]], epilogue="glu", out_dtype=BF16, name="s5_glu")

    g2_l, g2_c = mod_vecs(0, 2)
    sh4_l, sh4_c = mod_vecs(0, 3)
    sc4_l, sc4_c = mod_vecs(0, 4)
    x1_l, v_l = _resnorm(x_lat, glu[:n_lat], g2_l, norm2_w[0], sh4_l, sc4_l, n_batch=nb, seq=seq)
    x1_c, v_c = _resnorm(x_ctx, glu[n_lat:], g2_c, norm2_w[0], sh4_c, sc4_c, n_batch=nb, seq=ctx_len)
    v_all = jnp.concatenate([v_l, v_c], axis=0)
    moe0 = _moe(v_all, moe_router_w[0], moe_router_bias[0], moe_w_gate[0], moe_w_up[0],
                moe_w_down[0], shared_w_gate[0], shared_w_up[0], shared_w_down[0])

    g5_l, g5_c = mod_vecs(0, 5)
    sh_l, sh_c = mod_vecs(1, 0)
    sc_l, sc_c = mod_vecs(1, 1)
    x2_l, u_l = _resnorm(x1_l, moe0[:n_lat], g5_l, norm1_w[1], sh_l, sc_l, n_batch=nb, seq=seq,
                         x_mode="slab", y_mode="slab", xo_mode="slab", v_mode="row")
    _, u_c = _resnorm(x1_c, moe0[n_lat:], g5_c, norm1_w[1], sh_c, sc_c, n_batch=nb, seq=ctx_len)
    u_all = jnp.concatenate([u_l, u_c], axis=0)
    in_w = ssd_in_w[0]
    inner = ssd_out_w.shape[1]
    conv_dim = ssd_conv_w.shape[2]
    heads = inner // SSD_HEADDIM
    tn = 1024
    z_all = _matmul(u_all, [in_w], [0], inner, tn=tn, tm=512, out_dtype=BF16, name="ssd_in_z")
    xbc = _matmul(u_all, [in_w], [inner // tn], conv_dim, tn=tn, tm=512, name="ssd_in_xbc")
    dt_bias = ssd_dt_bias[0].reshape(-1)
    dt_all = _matmul(u_all, [in_w], [(inner + conv_dim) // LANES], 2 * heads, tn=LANES, tm=512,
                     biases=[jnp.pad(dt_bias, (inner + conv_dim, 0))], epilogue="softplus",
                     name="ssd_in_dt")
    xc_l = _conv_silu(xbc, ssd_conv_w[0], ssd_conv_b[0], seq, 0, nb)
    xc_c = _conv_silu(xbc, ssd_conv_w[0], ssd_conv_b[0], ctx_len, n_lat // ctx_len, nb)
    xc_all = jnp.concatenate([xc_l, xc_c], axis=0)
    rows = n_lat + n_ctx
    bt_all = xc_all[:, inner:inner + SSD_GROUPS * SSD_STATE].reshape(rows, SSD_GROUPS, SSD_STATE)
    bt_all = bt_all.transpose(1, 2, 0)
    dtg = dt_all.reshape(rows, 2, SSD_GROUPS, HPG).transpose(2, 0, 1, 3).reshape(SSD_GROUPS, rows, DTC)
    dtgt = dtg.transpose(0, 2, 1)
    a = -jnp.exp(ssd_a_log[0].astype(F32))
    a_g = a.reshape(2, SSD_GROUPS, HPG).transpose(1, 0, 2).reshape(SSD_GROUPS, DTC)
    skip = jnp.repeat(ssd_d[0].astype(F32), SSD_HEADDIM).reshape(1, inner)
    yn = _ssd_scan(xc_all, bt_all, z_all, dtg, dtgt, a_g.reshape(SSD_GROUPS, DTC, 1),
                   a_g.reshape(SSD_GROUPS, 1, DTC), skip, ssd_norm_w[0].reshape(1, inner),
                   nb, seq, ctx_len)
    y_lat = _matmul(yn, [ssd_out_w[0]], [0], d, tn=512, tm=512, out_dtype=BF16, name="ssd_out")

    g2_l, _ = mod_vecs(1, 2)
    sh4_l, _ = mod_vecs(1, 3)
    sc4_l, _ = mod_vecs(1, 4)
    x3_l, v_l = _resnorm(x2_l, y_lat, g2_l, norm2_w[1], sh4_l, sc4_l, n_batch=nb, seq=seq,
                         x_mode="slab", y_mode="row", xo_mode="slab", v_mode="slab")
    moe1 = _moe(v_l, moe_router_w[1], moe_router_bias[1], moe_w_gate[1], moe_w_up[1],
                moe_w_down[1], shared_w_gate[1], shared_w_up[1], shared_w_down[1])
    g5_l, _ = mod_vecs(1, 5)
    _, out = _resnorm(x3_l, moe1, g5_l, final_norm_w, None, None, n_batch=nb, seq=seq, v_dtype=F32)
    return out.reshape(nb, seq, d)
```

```python
import functools
import math

import jax
import jax.numpy as jnp
from jax import lax
from jax.experimental import pallas as pl
from jax.experimental.pallas import tpu as pltpu

F32 = jnp.float32
BF16 = jnp.bfloat16

GRID_W = 64
EPS = 1e-6
S5_GROUP = 16
S5_STATE = 64
SSD_HEADDIM = 64
SSD_STATE = 128
SSD_GROUPS = 8
SSD_CONV = 5
SSD_CHUNK = 128
N_EXPERTS = 64
TOP_K = 8
N_EXPERT_GROUPS = 8
TOPK_GROUPS = 4
ROUTED_SCALE = 2.5
MOE_BLOCK = 128

VMEM_LIMIT_BYTES = 56 * 1024 * 1024
LANES = 128
SUBLANES = 8


def _cparams(n_axes):
    return pltpu.CompilerParams(
        dimension_semantics=("arbitrary",) * n_axes,
        vmem_limit_bytes=VMEM_LIMIT_BYTES)


def _silu(v):
    return v * jax.nn.sigmoid(v)


def _dot(a, b):
    return jnp.dot(a, b, preferred_element_type=F32)


def _split3(a):
    hi = a.astype(BF16)
    r1 = a - hi.astype(F32)
    mid = r1.astype(BF16)
    lo = (r1 - mid.astype(F32)).astype(BF16)
    return hi, mid, lo


def _dot_exact_rhs(a, sel):
    hi, mid, lo = _split3(a)
    return _dot(hi, sel) + _dot(mid, sel) + _dot(lo, sel)


def _dot_exact_lhs(sel, a):
    hi, mid, lo = _split3(a)
    return _dot(sel, hi) + _dot(sel, mid) + _dot(sel, lo)


def _ada_kernel(c_ref, w_ref, b_ref, o_ref):
    c = _silu(c_ref[...])
    o_ref[0] = _dot(c.astype(BF16), w_ref[0].astype(BF16)) + b_ref[0]


def _ada(cond, ada_w, ada_b):
    depth, d, n = ada_w.shape
    tn = 1024
    rows = cond.shape[0]
    return pl.pallas_call(
        _ada_kernel,
        out_shape=jax.ShapeDtypeStruct((depth, rows, n), F32),
        grid=(depth, n // tn),
        in_specs=[pl.BlockSpec((rows, d), lambda l, j: (0, 0)),
                  pl.BlockSpec((1, d, tn), lambda l, j: (l, 0, j)),
                  pl.BlockSpec((1, 1, tn), lambda l, j: (l, 0, j))],
        out_specs=pl.BlockSpec((1, rows, tn), lambda l, j: (l, 0, j)),
        compiler_params=_cparams(2),
        name="ada",
    )(cond, ada_w, ada_b.reshape(depth, 1, n))


def _mm_kernel(*refs, n_w, has_bias, epilogue):
    x_ref = refs[0]
    w_refs = refs[1:1 + n_w]
    pos = 1 + n_w
    b_refs = refs[pos:pos + n_w] if has_bias else ()
    pos += n_w if has_bias else 0
    o_ref = refs[pos]
    wbf_refs = refs[pos + 1:pos + 1 + n_w]

    @pl.when(pl.program_id(1) == 0)
    def _():
        for w_ref, wbf in zip(w_refs, wbf_refs):
            wbf[...] = w_ref[...].astype(BF16)

    x = x_ref[...]
    zs = []
    for k in range(n_w):
        z = _dot(x, wbf_refs[k][...])
        if has_bias:
            z = z + b_refs[k][...]
        zs.append(z)
    if epilogue is None:
        out = zs[0]
    elif epilogue == "softplus":
        out = jax.nn.softplus(zs[0])
    elif epilogue == "glu":
        out = zs[0] * jax.nn.sigmoid(zs[1])
    elif epilogue == "swiglu":
        out = _silu(zs[0]) * zs[1]
    o_ref[...] = out.astype(o_ref.dtype)


def _matmul(x, ws, col_offsets, n_out, *, tn, tm, biases=None, epilogue=None,
            out_dtype=F32, name="matmul"):
    m, k = x.shape
    n_w = len(ws)
    has_bias = biases is not None
    in_specs = [pl.BlockSpec((tm, k), lambda j, i: (i, 0))]
    for off in col_offsets:
        in_specs.append(pl.BlockSpec((k, tn), lambda j, i, off=off: (0, j + off)))
    args = [x] + list(ws)
    if has_bias:
        for off in col_offsets:
            in_specs.append(pl.BlockSpec((1, tn), lambda j, i, off=off: (0, j + off)))
        args += [b.reshape(1, -1) for b in biases]
    return pl.pallas_call(
        functools.partial(_mm_kernel, n_w=n_w, has_bias=has_bias, epilogue=epilogue),
        out_shape=jax.ShapeDtypeStruct((m, n_out), out_dtype),
        grid=(n_out // tn, m // tm),
        in_specs=in_specs,
        out_specs=pl.BlockSpec((tm, tn), lambda j, i: (i, j)),
        scratch_shapes=[pltpu.VMEM((k, tn), BF16) for _ in range(n_w)],
        compiler_params=_cparams(2),
        name=name,
    )(*args)


SLABS = 8
ROW_TILE = 256


def _get_piece(ref, mode, j, rows, d):
    if mode == "slab":
        return ref[:, j * d:(j + 1) * d]
    return ref[j * rows:(j + 1) * rows, :]


def _put_piece(ref, mode, j, rows, d, val):
    if mode == "slab":
        ref[:, j * d:(j + 1) * d] = val.astype(ref.dtype)
    else:
        ref[j * rows:(j + 1) * rows, :] = val.astype(ref.dtype)


def _resnorm_kernel(*refs, has_y, write_x, x_mode, y_mode, xo_mode, v_mode, modulate, rows, d):
    it = iter(refs)
    x_ref = next(it)
    y_ref = next(it) if has_y else None
    g_ref = next(it) if has_y else None
    nw_ref = next(it)
    sh_ref = next(it) if modulate else None
    sc_ref = next(it) if modulate else None
    xo_ref = next(it) if write_x else None
    v_ref = next(it)
    nw = nw_ref[...]
    for j in range(SLABS):
        x = _get_piece(x_ref, x_mode, j, rows, d)
        if has_y:
            y = _get_piece(y_ref, y_mode, j, rows, d).astype(F32)
            x = x + g_ref[0] * y
            if write_x:
                _put_piece(xo_ref, xo_mode, j, rows, d, x)
        v = x * lax.rsqrt(jnp.mean(x * x, axis=-1, keepdims=True) + EPS) * nw
        if modulate:
            v = v * (1.0 + sc_ref[0]) + sh_ref[0]
        _put_piece(v_ref, v_mode, j, rows, d, v)


def _resnorm(x, y, gate, norm_w, shift, scale, *, n_batch, seq, x_mode="row",
             y_mode="row", xo_mode="row", v_mode="row", v_dtype=BF16, write_x=True,
             name="resnorm"):
    n, d = x.shape
    has_y = y is not None
    write_x = write_x and has_y
    modulate = shift is not None
    grows = seq // GRID_W
    tiles_per_seq = seq // ROW_TILE
    slab_used = "slab" in (x_mode, y_mode, xo_mode, v_mode)
    rows = grows if slab_used else ROW_TILE // SLABS
    if slab_used:
        assert grows * SLABS == ROW_TILE

    def spec(mode):
        if mode == "slab":
            return pl.BlockSpec((grows, SLABS * d), lambda b, t: (b, t))
        return pl.BlockSpec((ROW_TILE, d), lambda b, t: (b * tiles_per_seq + t, 0))

    def view(a, mode):
        return a.reshape(n_batch * grows, GRID_W * d) if mode == "slab" else a

    vec = pl.BlockSpec((1, 1, d), lambda b, t: (b, 0, 0))
    in_specs = [spec(x_mode)]
    args = [view(x, x_mode)]
    if has_y:
        in_specs += [spec(y_mode), vec]
        args += [view(y, y_mode), gate]
    in_specs.append(pl.BlockSpec((1, d), lambda b, t: (0, 0)))
    args.append(norm_w.reshape(1, d))
    if modulate:
        in_specs += [vec, vec]
        args += [shift, scale]
    out_shape, out_specs = [], []
    if write_x:
        out_shape.append(jax.ShapeDtypeStruct(view(x, xo_mode).shape, F32))
        out_specs.append(spec(xo_mode))
    vshape = (n_batch * grows, GRID_W * d) if v_mode == "slab" else (n, d)
    out_shape.append(jax.ShapeDtypeStruct(vshape, v_dtype))
    out_specs.append(spec(v_mode))
    outs = pl.pallas_call(
        functools.partial(_resnorm_kernel, has_y=has_y, write_x=write_x, x_mode=x_mode, y_mode=y_mode,
                          xo_mode=xo_mode, v_mode=v_mode, modulate=modulate, rows=rows, d=d),
        out_shape=out_shape,
        grid=(n_batch, tiles_per_seq),
        in_specs=in_specs,
        out_specs=out_specs,
        compiler_params=_cparams(2),
        name=name,
    )(*args)
    outs = [o.reshape(n, d) for o in outs]
    return (outs[0], outs[1]) if write_x else (None, outs[0])


S5_T = 128
S5_CB = 128
S5_NS = (S5_CB // S5_GROUP) * S5_STATE


def _reverse_rows(v_bf16):
    t = v_bf16.shape[0]
    r = lax.broadcasted_iota(jnp.int32, (t, t), 0)
    c = lax.broadcasted_iota(jnp.int32, (t, t), 1)
    flip = jnp.where(r + c == t - 1, 1.0, 0.0).astype(BF16)
    return _dot(flip, v_bf16)


def _s5_prep_kernel(xf_ref, xb_ref, nw_ref, sh_ref, sc_ref, o_ref, *, nb, d):
    nw = nw_ref[...]
    for b in range(nb):
        for rev, x_ref in ((False, xf_ref), (True, xb_ref)):
            x = x_ref[b]
            v = x * lax.rsqrt(jnp.mean(x * x, axis=-1, keepdims=True) + EPS) * nw
            v = (v * (1.0 + sc_ref[b]) + sh_ref[b]).astype(BF16)
            if rev:
                v = _reverse_rows(v).astype(BF16)
            k = b + (nb if rev else 0)
            o_ref[:, k * d:(k + 1) * d] = v


def _s5_prep(x3, norm_w, shift, scale):
    nb, s, d = x3.shape
    nt = s // S5_T
    return pl.pallas_call(
        functools.partial(_s5_prep_kernel, nb=nb, d=d),
        out_shape=jax.ShapeDtypeStruct((s, 2 * nb * d), BF16),
        grid=(nt,),
        in_specs=[pl.BlockSpec((nb, S5_T, d), lambda t: (0, t, 0)),
                  pl.BlockSpec((nb, S5_T, d), lambda t: (0, nt - 1 - t, 0)),
                  pl.BlockSpec((1, d), lambda t: (0, 0)),
                  pl.BlockSpec((nb, 1, d), lambda t: (0, 0, 0)),
                  pl.BlockSpec((nb, 1, d), lambda t: (0, 0, 0))],
        out_specs=pl.BlockSpec((S5_T, 2 * nb * d), lambda t: (t, 0)),
        compiler_params=_cparams(1),
        name="s5_prep",
    )(x3, x3, norm_w.reshape(1, d), shift, scale)


def _s5_scan_kernel(u_ref, bc_ref, are_ref, aim_ref, cc_ref, s0_ref, y_ref, sf_ref,
                    bu_ref, st_ref, *, n_tiles):
    i = pl.program_id(1)
    ns = S5_NS

    @pl.when(i == 0)
    def _():
        st_ref[...] = s0_ref[0]

    u = u_ref[...].astype(F32)
    row = lax.broadcasted_iota(jnp.int32, u.shape, 0)
    is_fwd = jnp.bitwise_and(row, SUBLANES - 1) < (SUBLANES // 2)
    lhs = jnp.concatenate([jnp.where(is_fwd, u, 0.0), jnp.where(is_fwd, 0.0, u)], axis=1)
    bu_ref[...] = _dot(lhs.astype(BF16), bc_ref[0])
    a_re = are_ref[0]
    a_im = aim_ref[0]

    def step(t, carry):
        s_re, s_im = carry
        r = pl.multiple_of(t * SUBLANES, SUBLANES)
        n_re = a_re * s_re - a_im * s_im + bu_ref[pl.ds(r, SUBLANES), 0:ns]
        n_im = a_re * s_im + a_im * s_re + bu_ref[pl.ds(r, SUBLANES), ns:2 * ns]
        bu_ref[pl.ds(r, SUBLANES), 0:ns] = n_re
        bu_ref[pl.ds(r, SUBLANES), ns:2 * ns] = n_im
        return n_re, n_im

    s_re, s_im = lax.fori_loop(0, S5_T, step, (st_ref[:, 0:ns], st_ref[:, ns:2 * ns]))
    st_ref[:, 0:ns] = s_re
    st_ref[:, ns:2 * ns] = s_im
    y2 = _dot(bu_ref[...].astype(BF16), cc_ref[0])
    y_ref[...] = jnp.where(is_fwd, y2[:, 0:S5_CB], y2[:, S5_CB:2 * S5_CB])

    @pl.when(i == n_tiles - 1)
    def _():
        sf_ref[0] = st_ref[...]


def _s5_scan(u2r, bcat, a_re, a_im, ccat, s0):
    rows, d = u2r.shape
    nblk = d // S5_CB
    tr = S5_T * SUBLANES
    n_tiles = rows // tr
    ns2 = 2 * S5_NS
    return pl.pallas_call(
        functools.partial(_s5_scan_kernel, n_tiles=n_tiles),
        out_shape=(jax.ShapeDtypeStruct((rows, d), F32),
                   jax.ShapeDtypeStruct((nblk, SUBLANES, ns2), F32)),
        grid=(nblk, n_tiles),
        in_specs=[pl.BlockSpec((tr, S5_CB), lambda j, i: (i, j)),
                  pl.BlockSpec((1, 2 * S5_CB, ns2), lambda j, i: (j, 0, 0)),
                  pl.BlockSpec((1, SUBLANES, S5_NS), lambda j, i: (j, 0, 0)),
                  pl.BlockSpec((1, SUBLANES, S5_NS), lambda j, i: (j, 0, 0)),
                  pl.BlockSpec((1, ns2, 2 * S5_CB), lambda j, i: (j, 0, 0)),
                  pl.BlockSpec((1, SUBLANES, ns2), lambda j, i: (j, 0, 0))],
        out_specs=(pl.BlockSpec((tr, S5_CB), lambda j, i: (i, j)),
                   pl.BlockSpec((1, SUBLANES, ns2), lambda j, i: (j, 0, 0))),
        scratch_shapes=[pltpu.VMEM((tr, ns2), F32), pltpu.VMEM((SUBLANES, ns2), F32)],
        compiler_params=_cparams(2),
        name="s5_scan",
    )(u2r, bcat, a_re, a_im, ccat, s0)


def _s5_out_kernel(yf_ref, yb_ref, u_ref, skip_ref, o_ref):
    yb = yb_ref[...]
    hi = yb.astype(BF16)
    lo = (yb - hi.astype(F32)).astype(BF16)
    y = (skip_ref[...] * u_ref[...].astype(F32) + yf_ref[...]
         + _reverse_rows(hi) + _reverse_rows(lo))
    o_ref[...] = jax.nn.gelu(y).astype(o_ref.dtype)


def _s5_out(y2, u2, skip, nb, d):
    s = y2.shape[0]
    nt = s // S5_T
    return pl.pallas_call(
        _s5_out_kernel,
        out_shape=jax.ShapeDtypeStruct((nb * s, d), BF16),
        grid=(nb, nt),
        in_specs=[pl.BlockSpec((S5_T, d), lambda b, t: (t, b)),
                  pl.BlockSpec((S5_T, d), lambda b, t: (nt - 1 - t, nb + b)),
                  pl.BlockSpec((S5_T, d), lambda b, t: (t, b)),
                  pl.BlockSpec((1, d), lambda b, t: (0, 0))],
        out_specs=pl.BlockSpec((S5_T, d), lambda b, t: (b * nt + t, 0)),
        compiler_params=_cparams(2),
        name="s5_out",
    )(y2, y2, u2, skip.reshape(1, d))


def _s5_pack_params(lam_re, lam_im, log_step, b_re, b_im, c_re, c_im, nb):
    f32 = F32
    g = lam_re.shape[1]
    gpb = S5_CB // S5_GROUP
    nblk = g // gpb
    eye = jnp.eye(gpb, dtype=f32)
    a_re_rows, a_im_rows, b_parts, c_parts = [], [], [], []
    for dr in range(2):
        lr, li = lam_re[dr].astype(f32), lam_im[dr].astype(f32)
        br, bi = b_re[dr].astype(f32), b_im[dr].astype(f32)
        dt = jnp.exp(log_step[dr].astype(f32))[:, None]
        mag = jnp.exp(lr * dt)
        abar_re = mag * jnp.cos(li * dt)
        abar_im = mag * jnp.sin(li * dt)
        num_re = abar_re - 1.0
        num_im = abar_im
        den = lr * lr + li * li
        f_re = (num_re * lr + num_im * li) / den
        f_im = (num_im * lr - num_re * li) / den
        bbar_re = f_re[..., None] * br - f_im[..., None] * bi
        bbar_im = f_re[..., None] * bi + f_im[..., None] * br
        a_re_rows.append(jnp.broadcast_to(abar_re.reshape(nblk, 1, S5_NS), (nblk, nb, S5_NS)))
        a_im_rows.append(jnp.broadcast_to(abar_im.reshape(nblk, 1, S5_NS), (nblk, nb, S5_NS)))

        def blockdiag_in(bb):
            b4 = bb.reshape(nblk, gpb, S5_STATE, S5_GROUP)
            return jnp.einsum('jgpk,gh->jgkhp', b4, eye).reshape(nblk, S5_CB, S5_NS)

        def blockdiag_out(cc):
            c4 = cc.astype(f32).reshape(nblk, gpb, S5_GROUP, S5_STATE)
            return jnp.einsum('jgkp,gh->jgphk', c4, eye).reshape(nblk, S5_NS, S5_CB)

        b_parts.append(jnp.concatenate([blockdiag_in(bbar_re), blockdiag_in(bbar_im)], axis=2))
        c_parts.append(jnp.concatenate([blockdiag_out(c_re[dr]), -blockdiag_out(c_im[dr])], axis=1))
    a_re = jnp.concatenate(a_re_rows, axis=1)
    a_im = jnp.concatenate(a_im_rows, axis=1)
    bcat = jnp.concatenate(b_parts, axis=1).astype(BF16)
    ccat = jnp.concatenate(c_parts, axis=2).astype(BF16)
    return bcat, a_re, a_im, ccat


def _conv_kernel(x_ref, w_ref, b_ref, o_ref, pad_ref, *, seq):
    halo = SUBLANES
    zeros = jnp.zeros((halo, pad_ref.shape[1]), F32)
    pad_ref[0:halo, :] = zeros
    pad_ref[halo + seq:2 * halo + seq, :] = zeros
    pad_ref[halo:halo + seq, :] = x_ref[...]
    acc = jnp.zeros((seq, pad_ref.shape[1]), F32) + b_ref[...]
    for k in range(SSD_CONV):
        off = halo + k - SSD_CONV // 2
        acc = acc + w_ref[k:k + 1, :] * pad_ref[off:off + seq, :]
    o_ref[...] = _silu(acc).astype(o_ref.dtype)


def _conv_silu(xbc, conv_w, conv_b, seq, row_block_offset, n_seq, out_dtype=BF16):
    _, c = xbc.shape
    tc = 512
    return pl.pallas_call(
        functools.partial(_conv_kernel, seq=seq),
        out_shape=jax.ShapeDtypeStruct((n_seq * seq, c), out_dtype),
        grid=(n_seq, c // tc),
        in_specs=[pl.BlockSpec((seq, tc), lambda b, j: (b + row_block_offset, j)),
                  pl.BlockSpec((SSD_CONV, tc), lambda b, j: (0, j)),
                  pl.BlockSpec((1, tc), lambda b, j: (0, j))],
        out_specs=pl.BlockSpec((seq, tc), lambda b, j: (b, j)),
        scratch_shapes=[pltpu.VMEM((seq + 2 * SUBLANES, tc), F32)],
        compiler_params=_cparams(2),
        name="ssd_conv",
    )(xbc, conv_w, conv_b.reshape(1, c))


HPG = 8
GCH = HPG * SSD_HEADDIM
DTC = 2 * HPG


def _ssd_kernel(xl_ref, bl_ref, btl_ref, cl_ref, dtl_ref, dttl_ref,
                xc_ref, btc_ref, dtc_ref, dttc_ref,
                z_ref, acol_ref, arow_ref, skip_ref, nw_ref, o_ref,
                ht_ref, yacc_ref, *, n_lat, n_ctx):
    q = SSD_CHUNK
    r_i = lax.broadcasted_iota(jnp.int32, (q, q), 0)
    c_i = lax.broadcasted_iota(jnp.int32, (q, q), 1)
    tril = jnp.where(c_i <= r_i, 1.0, 0.0).astype(BF16)
    triu = jnp.where(r_i <= c_i, 1.0, 0.0).astype(BF16)
    lower = c_i <= r_i
    upper = c_i >= r_i
    lane = lax.broadcasted_iota(jnp.int32, (q, 2 * SSD_HEADDIM), 1)
    left = lane < SSD_HEADDIM
    e_r = lax.broadcasted_iota(jnp.int32, (DTC, GCH), 0)
    e_c = lax.broadcasted_iota(jnp.int32, (DTC, GCH), 1)
    s_r = lax.broadcasted_iota(jnp.int32, (DTC, HPG * q), 0)
    s_c = lax.broadcasted_iota(jnp.int32, (DTC, HPG * q), 1)
    a_row = arow_ref[0]
    a_col = acol_ref[0]

    def chunk(x_ref, b_ref, bt_ref, c_ref, dt_ref, dtt_ref, r0, dr, with_y, first):
        head_of_ch = lax.shift_right_logical(e_c, int(math.log2(SSD_HEADDIM)))
        head_of_col = lax.shift_right_logical(s_c, int(math.log2(q)))
        expand = jnp.where(head_of_ch + dr * HPG == e_r, 1.0, 0.0).astype(BF16)
        bcast = jnp.where(head_of_col + dr * HPG == s_r, 1.0, 0.0).astype(BF16)
        x = x_ref[pl.ds(r0, q), :].astype(F32)
        bt = bt_ref[0, :, pl.ds(r0, q)]
        dt = dt_ref[0, pl.ds(r0, q), :]
        dtt = dtt_ref[0, :, pl.ds(r0, q)]
        la = dt * a_row
        lat = dtt * a_col
        cs = _dot_exact_lhs(tril, la)
        cst = _dot_exact_rhs(lat, triu)
        csx = _dot_exact_rhs(cs, expand)
        totx = csx[q - 1:q, :]
        dtx = _dot_exact_rhs(dt, expand)
        if dr == 1:
            rk, rkt = cs - la, cst - lat
            rkx = csx - _dot_exact_rhs(la, expand)
        else:
            rk, rkt, rkx = cs, cst, csx
        xd = x * dtx
        if dr == 0:
            w_state = jnp.exp(totx - rkx)
            w_off = jnp.exp(rkx)
        else:
            w_state = jnp.exp(rkx)
            w_off = jnp.exp(totx - rkx)
        h_old = ht_ref[...]
        ht_ref[...] = h_old * jnp.exp(totx) + _dot(bt, (xd * w_state).astype(BF16))
        if not with_y:
            return
        cm = c_ref[pl.ds(r0, q), :]
        cb = _dot(cm, bt)
        rkb = _dot_exact_rhs(rk, bcast)
        xdb = xd.astype(BF16)
        pieces = []
        for pair in range(HPG // 2):
            ms = []
            for hh in (2 * pair, 2 * pair + 1):
                col = dr * HPG + hh
                colv = rkb[:, hh * q:(hh + 1) * q]
                rowv = rkt[col:col + 1, :]
                if dr == 0:
                    seg = jnp.where(lower, colv - rowv, -1e30)
                else:
                    seg = jnp.where(upper, rowv - colv, -1e30)
                ms.append((cb * jnp.exp(seg)).astype(BF16))
            xp = xdb[:, pair * 2 * SSD_HEADDIM:(pair + 1) * 2 * SSD_HEADDIM]
            zero = jnp.zeros_like(xp)
            rhs = jnp.concatenate([jnp.where(left, xp, zero), jnp.where(left, zero, xp)], axis=0)
            pieces.append(_dot(jnp.concatenate(ms, axis=1), rhs))
        y = jnp.concatenate(pieces, axis=1) + _dot(cm, h_old.astype(BF16)) * w_off
        if first:
            yacc_ref[pl.ds(r0, q), :] = y + skip_ref[...] * x
        else:
            y = yacc_ref[pl.ds(r0, q), :] + y
            v = y * _silu(z_ref[pl.ds(r0, q), :].astype(F32))
            v = v * lax.rsqrt(jnp.mean(v * v, axis=-1, keepdims=True) + EPS) * nw_ref[...]
            o_ref[pl.ds(r0, q), :] = v.astype(o_ref.dtype)

    for dr in range(2):
        ht_ref[...] = jnp.zeros_like(ht_ref)

        def ctx_body(k, _):
            kk = k if dr == 0 else n_ctx - 1 - k
            r0 = pl.multiple_of(kk * q, q)
            chunk(xc_ref, None, btc_ref, None, dtc_ref, dttc_ref, r0, dr, False, False)
            return 0

        lax.fori_loop(0, n_ctx, ctx_body, 0)

        def lat_body(k, _):
            kk = k if dr == 0 else n_lat - 1 - k
            r0 = pl.multiple_of(kk * q, q)
            chunk(xl_ref, bl_ref, btl_ref, cl_ref, dtl_ref, dttl_ref, r0, dr, True, dr == 0)
            return 0

        lax.fori_loop(0, n_lat, lat_body, 0)


def _ssd_scan(xc_all, bt_all, z, dtg, dtgt, a_col, a_row, skip, norm_w, nb, seq, ctx_len):
    n_lat, n_ctx = seq // SSD_CHUNK, ctx_len // SSD_CHUNK
    inner = SSD_GROUPS * GCH
    xoff = 0
    boff = inner // SSD_STATE
    coff = boff + SSD_GROUPS
    cb0 = nb * seq // ctx_len
    in_specs = [
        pl.BlockSpec((seq, GCH), lambda b, g: (b, g)),
        pl.BlockSpec((seq, SSD_STATE), lambda b, g: (b, boff + g)),
        pl.BlockSpec((1, SSD_STATE, seq), lambda b, g: (g, 0, b)),
        pl.BlockSpec((seq, SSD_STATE), lambda b, g: (b, coff + g)),
        pl.BlockSpec((1, seq, DTC), lambda b, g: (g, b, 0)),
        pl.BlockSpec((1, DTC, seq), lambda b, g: (g, 0, b)),
        pl.BlockSpec((ctx_len, GCH), lambda b, g: (cb0 + b, g)),
        pl.BlockSpec((1, SSD_STATE, ctx_len), lambda b, g: (g, 0, cb0 + b)),
        pl.BlockSpec((1, ctx_len, DTC), lambda b, g: (g, cb0 + b, 0)),
        pl.BlockSpec((1, DTC, ctx_len), lambda b, g: (g, 0, cb0 + b)),
        pl.BlockSpec((seq, GCH), lambda b, g: (b, g)),
        pl.BlockSpec((1, DTC, 1), lambda b, g: (g, 0, 0)),
        pl.BlockSpec((1, 1, DTC), lambda b, g: (g, 0, 0)),
        pl.BlockSpec((1, GCH), lambda b, g: (0, g)),
        pl.BlockSpec((1, GCH), lambda b, g: (0, g)),
    ]
    del xoff
    return pl.pallas_call(
        functools.partial(_ssd_kernel, n_lat=n_lat, n_ctx=n_ctx),
        out_shape=jax.ShapeDtypeStruct((nb * seq, inner), BF16),
        grid=(nb, SSD_GROUPS),
        in_specs=in_specs,
        out_specs=pl.BlockSpec((seq, GCH), lambda b, g: (b, g)),
        scratch_shapes=[pltpu.VMEM((SSD_STATE, GCH), F32), pltpu.VMEM((seq, GCH), F32)],
        compiler_params=_cparams(2),
        name="ssd_scan",
    )(xc_all, xc_all, bt_all, xc_all, dtg, dtgt, xc_all, bt_all, dtg, dtgt,
      z, a_col, a_row, skip, norm_w)


def _expert_kernel(be_ref, nu_ref, x_ref, wg_ref, wu_ref, wd_ref, rw_ref, o_ref,
                   wgb, wub, wdb):
    i = pl.program_id(0)
    prev = be_ref[jnp.maximum(i - 1, 0)]
    changed = jnp.logical_or(i == 0, be_ref[i] != prev)

    @pl.when(changed)
    def _():
        wgb[...] = wg_ref[0].astype(BF16)
        wub[...] = wu_ref[0].astype(BF16)
        wdb[...] = wd_ref[0].astype(BF16)

    @pl.when(i < nu_ref[0])
    def _():
        x = x_ref[...]
        h = _silu(_dot(x, wgb[...])) * _dot(x, wub[...])
        o_ref[...] = (_dot(h.astype(BF16), wdb[...]) * rw_ref[...]).astype(o_ref.dtype)

    @pl.when(i >= nu_ref[0])
    def _():
        o_ref[...] = jnp.zeros_like(o_ref)


def _experts(x_sorted, row_w, block_e, n_used, w_gate, w_up, w_down):
    cap, d = x_sorted.shape
    ff = w_gate.shape[2]
    n_blocks = cap // MOE_BLOCK
    grid_spec = pltpu.PrefetchScalarGridSpec(
        num_scalar_prefetch=2,
        grid=(n_blocks,),
        in_specs=[pl.BlockSpec((MOE_BLOCK, d), lambda i, be, nu: (i, 0)),
                  pl.BlockSpec((1, d, ff), lambda i, be, nu: (be[i], 0, 0)),
                  pl.BlockSpec((1, d, ff), lambda i, be, nu: (be[i], 0, 0)),
                  pl.BlockSpec((1, ff, d), lambda i, be, nu: (be[i], 0, 0)),
                  pl.BlockSpec((MOE_BLOCK, 1), lambda i, be, nu: (i, 0))],
        out_specs=pl.BlockSpec((MOE_BLOCK, d), lambda i, be, nu: (i, 0)),
        scratch_shapes=[pltpu.VMEM((d, ff), BF16), pltpu.VMEM((d, ff), BF16),
                        pltpu.VMEM((ff, d), BF16)])
    return pl.pallas_call(
        _expert_kernel,
        out_shape=jax.ShapeDtypeStruct((cap, d), F32),
        grid_spec=grid_spec,
        compiler_params=_cparams(1),
        name="moe_experts",
    )(block_e, n_used, x_sorted, w_gate, w_up, w_down, row_w.reshape(cap, 1))


ROUTE_T = 512
GROUP_SIZE = N_EXPERTS // N_EXPERT_GROUPS


def _route_kernel(v_ref, rwh_ref, rwl_ref, bias_ref, eid_ref, rank_ref, gate_ref, cnt_ref,
                  carry_ref, *, n_tiles):
    i = pl.program_id(0)
    t = v_ref.shape[0]
    ng, gs = N_EXPERT_GROUPS, GROUP_SIZE
    neg = -jnp.inf

    @pl.when(i == 0)
    def _():
        carry_ref[...] = jnp.zeros_like(carry_ref)

    v = v_ref[...]
    nt_dims = (((1,), (1,)), ((), ()))
    logits = (lax.dot_general(rwh_ref[...], v, nt_dims, preferred_element_type=F32)
              + lax.dot_general(rwl_ref[...], v, nt_dims, preferred_element_type=F32))
    scores = jax.nn.sigmoid(logits)
    biased = scores + bias_ref[:, 0:1]
    x3 = biased.reshape(ng, gs, t)
    s3 = scores.reshape(ng, gs, t)
    mi = lax.broadcasted_iota(jnp.int32, (ng, gs, t), 1).astype(F32)
    fi = lax.broadcasted_iota(jnp.int32, (ng, gs, t), 0).astype(F32) * gs + mi
    gi = lax.broadcasted_iota(jnp.int32, (ng, 1, t), 0).astype(F32)

    m1 = jnp.max(x3, axis=1, keepdims=True)
    i1 = jnp.min(jnp.where(x3 == m1, mi, float(gs)), axis=1, keepdims=True)
    m2 = jnp.max(jnp.where(mi == i1, neg, x3), axis=1, keepdims=True)
    cur = m1 + m2
    gmask = jnp.zeros((ng, 1, t), F32)
    for _ in range(TOPK_GROUPS):
        gm = jnp.max(cur, axis=0, keepdims=True)
        idx = jnp.min(jnp.where(cur == gm, gi, float(ng)), axis=0, keepdims=True)
        hit = gi == idx
        gmask = jnp.where(hit, 1.0, gmask)
        cur = jnp.where(hit, neg, cur)

    cand = jnp.where(gmask > 0.0, x3, neg)
    sel = jnp.zeros((ng, gs, t), F32)
    eids = []
    for _ in range(TOP_K):
        m = jnp.max(jnp.max(cand, axis=1, keepdims=True), axis=0, keepdims=True)
        idx = jnp.where(cand == m, fi, float(N_EXPERTS))
        idx = jnp.min(jnp.min(idx, axis=1, keepdims=True), axis=0, keepdims=True)
        hit = fi == idx
        sel = jnp.where(hit, 1.0, sel)
        cand = jnp.where(hit, neg, cand)
        eids.append(idx)

    selr = sel.reshape(N_EXPERTS, t)
    r_i = lax.broadcasted_iota(jnp.int32, (t, t), 0)
    c_i = lax.broadcasted_iota(jnp.int32, (t, t), 1)
    before = jnp.where(r_i < c_i, 1.0, 0.0).astype(BF16)
    rank = _dot(selr.astype(BF16), before) + carry_ref[:, 0:1]
    carry_ref[...] = carry_ref[...] + jnp.sum(selr, axis=1, keepdims=True)
    rank3 = rank.reshape(ng, gs, t)

    gsel = sel * s3
    denom = jnp.sum(jnp.sum(gsel, axis=1, keepdims=True), axis=0, keepdims=True)
    gate3 = gsel / denom * ROUTED_SCALE

    def pick(a3, hit):
        return jnp.sum(jnp.sum(jnp.where(hit, a3, 0.0), axis=1, keepdims=True), axis=0,
                       keepdims=True).reshape(1, t)

    for k in range(TOP_K):
        hit = fi == eids[k]
        eid_ref[k:k + 1, :] = eids[k].reshape(1, t).astype(jnp.int32)
        rank_ref[k:k + 1, :] = pick(rank3, hit).astype(jnp.int32)
        gate_ref[k:k + 1, :] = pick(gate3, hit)

    @pl.when(i == n_tiles - 1)
    def _():
        cnt_ref[...] = carry_ref[...]


def _route(v, router_w, router_bias):
    n, d = v.shape
    n_tiles = n // ROUTE_T
    rwt = router_w.astype(F32).T
    rwh = rwt.astype(BF16)
    rwl = (rwt - rwh.astype(F32)).astype(BF16)
    bias = jnp.broadcast_to(router_bias.astype(F32)[:, None], (N_EXPERTS, LANES))
    slot = pl.BlockSpec((TOP_K, ROUTE_T), lambda i: (0, i))
    full = pl.BlockSpec((N_EXPERTS, d), lambda i: (0, 0))
    return pl.pallas_call(
        functools.partial(_route_kernel, n_tiles=n_tiles),
        out_shape=(jax.ShapeDtypeStruct((TOP_K, n), jnp.int32),
                   jax.ShapeDtypeStruct((TOP_K, n), jnp.int32),
                   jax.ShapeDtypeStruct((TOP_K, n), F32),
                   jax.ShapeDtypeStruct((N_EXPERTS, LANES), F32)),
        grid=(n_tiles,),
        in_specs=[pl.BlockSpec((ROUTE_T, d), lambda i: (i, 0)), full, full,
                  pl.BlockSpec((N_EXPERTS, LANES), lambda i: (0, 0))],
        out_specs=(slot, slot, slot, pl.BlockSpec((N_EXPERTS, LANES), lambda i: (0, 0))),
        scratch_shapes=[pltpu.VMEM((N_EXPERTS, LANES), F32)],
        compiler_params=_cparams(1),
        name="moe_route",
    )(v, rwh, rwl, bias)


def _moe(v, router_w, router_bias, w_gate, w_up, w_down, sw_gate, sw_up, sw_down):
    n, d = v.shape
    eid, rank, gate, cnt = _route(v, router_w, router_bias)
    nk = n * TOP_K
    cap = -(-nk // MOE_BLOCK) * MOE_BLOCK + N_EXPERTS * MOE_BLOCK
    n_blocks = cap // MOE_BLOCK
    counts = cnt[:, 0].astype(jnp.int32)
    padded = (counts + MOE_BLOCK - 1) // MOE_BLOCK * MOE_BLOCK
    pad_end = jnp.cumsum(padded)
    pad_start = pad_end - padded
    experts = jnp.arange(N_EXPERTS, dtype=jnp.int32)
    dest = rank + jnp.sum(jnp.where(eid[None] == experts[:, None, None],
                                    pad_start[:, None, None], 0), axis=0)
    block_start = jnp.arange(n_blocks, dtype=jnp.int32) * MOE_BLOCK
    block_e = jnp.minimum(jnp.sum((pad_end[None, :] <= block_start[:, None]).astype(jnp.int32), axis=1),
                          N_EXPERTS - 1).astype(jnp.int32)
    n_used = (pad_end[-1] // MOE_BLOCK).astype(jnp.int32).reshape(1)
    tok = jnp.broadcast_to(jnp.arange(n, dtype=jnp.int32)[None, :], (TOP_K, n))
    packed = jnp.stack([tok.astype(F32).reshape(-1), gate.reshape(-1)], axis=1)
    rows = jnp.zeros((cap, 2), F32).at[dest.reshape(-1)].set(packed)
    row_tok = rows[:, 0].astype(jnp.int32)
    row_w = rows[:, 1]
    x_sorted = v[row_tok]
    yb = _experts(x_sorted, row_w, block_e, n_used, w_gate, w_up, w_down)
    routed = jnp.zeros((n, d), F32).at[row_tok].add(yb)
    hs = _matmul(v, [sw_gate, sw_up], [0, 0], sw_gate.shape[1], tn=sw_gate.shape[1], tm=512,
                 epilogue="swiglu", out_dtype=BF16, name="shared_up")
    shared = _matmul(hs, [sw_down], [0], d, tn=1024, tm=512, name="shared_down")
    return routed + shared


def kernel(x, c, ctx, c_ctx, ada_w, ada_b, norm1_w, norm2_w, s5_lambda_re, s5_lambda_im, s5_log_step, s5_b_re, s5_b_im, s5_c_re, s5_c_im, s5_d, s5_glu_w, s5_glu_b, ssd_in_w, ssd_conv_w, ssd_conv_b, ssd_dt_bias, ssd_a_log, ssd_d, ssd_norm_w, ssd_out_w, moe_router_w, moe_router_bias, moe_w_gate, moe_w_up, moe_w_down, shared_w_gate, shared_w_up, shared_w_down, final_norm_w):
    nb, seq, d = x.shape
    ctx_len = ctx.shape[1]
    n_lat = nb * seq
    n_ctx = nb * ctx_len

    cond = jnp.concatenate([c, c_ctx[None, :], jnp.zeros((SUBLANES - nb - 1, d), F32)], axis=0)
    mods = _ada(cond, ada_w, ada_b)

    def mod_vecs(layer, k):
        m = mods[layer, :, k * d:(k + 1) * d]
        lat = m[:nb].reshape(nb, 1, d)
        cx = jnp.broadcast_to(m[nb].reshape(1, 1, d), (nb, 1, d))
        return lat, cx

    x_lat = x.reshape(n_lat, d)
    x_ctx = ctx.reshape(n_ctx, d)

    sh_l, sh_c = mod_vecs(0, 0)
    sc_l, sc_c = mod_vecs(0, 1)
    u2_c = _s5_prep(ctx, norm1_w[0], sh_c, sc_c)
    u2_l = _s5_prep(x, norm1_w[0], sh_l, sc_l)
    bcat, a_re, a_im, ccat = _s5_pack_params(
        s5_lambda_re[0], s5_lambda_im[0], s5_log_step[0], s5_b_re[0], s5_b_im[0],
        s5_c_re[0], s5_c_im[0], nb)
    nblk = d // S5_CB
    s0 = jnp.zeros((nblk, SUBLANES, 2 * S5_NS), F32)
    y2_c, s_ctx = _s5_scan(u2_c.reshape(ctx_len * 2 * nb, d), bcat, a_re, a_im, ccat, s0)
    y2_l, _ = _s5_scan(u2_l.reshape(seq * 2 * nb, d), bcat, a_re, a_im, ccat, s_ctx)
    g_l = _s5_out(y2_l.reshape(seq, 2 * nb * d), u2_l, s5_d[0], nb, d)
    g_c = _s5_out(y2_c.reshape(ctx_len, 2 * nb * d), u2_c, s5_d[0], nb, d)
    g_all = jnp.concatenate([g_l, g_c], axis=0)
    half = s5_glu_w.shape[2] // 2
    tn = 1024
    glu = _matmul(g_all, [s5_glu_w[0], s5_glu_w[0]], [0, half // tn], half, tn=tn, tm=512,
                  biases=[s5_glu_b[0], s5_glu_b[0]], epilogue="glu", out_dtype=BF16, name="s5_glu")

    g2_l, g2_c = mod_vecs(0, 2)
    sh4_l, sh4_c = mod_vecs(0, 3)
    sc4_l, sc4_c = mod_vecs(0, 4)
    x1_l, v_l = _resnorm(x_lat, glu[:n_lat], g2_l, norm2_w[0], sh4_l, sc4_l, n_batch=nb, seq=seq)
    x1_c, v_c = _resnorm(x_ctx, glu[n_lat:], g2_c, norm2_w[0], sh4_c, sc4_c, n_batch=nb, seq=ctx_len)
    v_all = jnp.concatenate([v_l, v_c], axis=0)
    moe0 = _moe(v_all, moe_router_w[0], moe_router_bias[0], moe_w_gate[0], moe_w_up[0],
                moe_w_down[0], shared_w_gate[0], shared_w_up[0], shared_w_down[0])

    g5_l, g5_c = mod_vecs(0, 5)
    sh_l, sh_c = mod_vecs(1, 0)
    sc_l, sc_c = mod_vecs(1, 1)
    x2_l, u_l = _resnorm(x1_l, moe0[:n_lat], g5_l, norm1_w[1], sh_l, sc_l, n_batch=nb, seq=seq,
                         x_mode="slab", y_mode="slab", xo_mode="slab", v_mode="row")
    _, u_c = _resnorm(x1_c, moe0[n_lat:], g5_c, norm1_w[1], sh_c, sc_c, n_batch=nb, seq=ctx_len,
                      write_x=False)
    u_all = jnp.concatenate([u_l, u_c], axis=0)
    in_w = ssd_in_w[0]
    inner = ssd_out_w.shape[1]
    conv_dim = ssd_conv_w.shape[2]
    heads = inner // SSD_HEADDIM
    tn = 1024
    z_all = _matmul(u_all, [in_w], [0], inner, tn=tn, tm=512, out_dtype=BF16, name="ssd_in_z")
    xbc = _matmul(u_all, [in_w], [inner // tn], conv_dim, tn=tn, tm=512, name="ssd_in_xbc")
    dt_bias = ssd_dt_bias[0].reshape(-1)
    dt_all = _matmul(u_all, [in_w], [(inner + conv_dim) // LANES], 2 * heads, tn=LANES, tm=512,
                     biases=[jnp.pad(dt_bias, (inner + conv_dim, 0))], epilogue="softplus",
                     name="ssd_in_dt")
    xc_l = _conv_silu(xbc, ssd_conv_w[0], ssd_conv_b[0], seq, 0, nb)
    xc_c = _conv_silu(xbc, ssd_conv_w[0], ssd_conv_b[0], ctx_len, n_lat // ctx_len, nb)
    xc_all = jnp.concatenate([xc_l, xc_c], axis=0)
    rows = n_lat + n_ctx
    bt_all = xc_all[:, inner:inner + SSD_GROUPS * SSD_STATE].reshape(rows, SSD_GROUPS, SSD_STATE)
    bt_all = bt_all.transpose(1, 2, 0)
    dtg = dt_all.reshape(rows, 2, SSD_GROUPS, HPG).transpose(2, 0, 1, 3).reshape(SSD_GROUPS, rows, DTC)
    dtgt = dtg.transpose(0, 2, 1)
    a = -jnp.exp(ssd_a_log[0].astype(F32))
    a_g = a.reshape(2, SSD_GROUPS, HPG).transpose(1, 0, 2).reshape(SSD_GROUPS, DTC)
    skip = jnp.repeat(ssd_d[0].astype(F32), SSD_HEADDIM).reshape(1, inner)
    yn = _ssd_scan(xc_all, bt_all, z_all, dtg, dtgt, a_g.reshape(SSD_GROUPS, DTC, 1),
                   a_g.reshape(SSD_GROUPS, 1, DTC), skip, ssd_norm_w[0].reshape(1, inner),
                   nb, seq, ctx_len)
    y_lat = _matmul(yn, [ssd_out_w[0]], [0], d, tn=512, tm=512, out_dtype=BF16, name="ssd_out")

    g2_l, _ = mod_vecs(1, 2)
    sh4_l, _ = mod_vecs(1, 3)
    sc4_l, _ = mod_vecs(1, 4)
    x3_l, v_l = _resnorm(x2_l, y_lat, g2_l, norm2_w[1], sh4_l, sc4_l, n_batch=nb, seq=seq,
                         x_mode="slab", y_mode="row", xo_mode="slab", v_mode="slab")
    moe1 = _moe(v_l, moe_router_w[1], moe_router_bias[1], moe_w_gate[1], moe_w_up[1],
                moe_w_down[1], shared_w_gate[1], shared_w_up[1], shared_w_down[1])
    g5_l, _ = mod_vecs(1, 5)
    _, out = _resnorm(x3_l, moe1, g5_l, final_norm_w, None, None, n_batch=nb, seq=seq, v_dtype=F32,
                      write_x=False)
    return out.reshape(nb, seq, d)
```

```python
import functools
import math

import jax
import jax.numpy as jnp
from jax import lax
from jax.experimental import pallas as pl
from jax.experimental.pallas import tpu as pltpu

F32 = jnp.float32
BF16 = jnp.bfloat16

GRID_W = 64
EPS = 1e-6
S5_GROUP = 16
S5_STATE = 64
SSD_HEADDIM = 64
SSD_STATE = 128
SSD_GROUPS = 8
SSD_CONV = 5
SSD_CHUNK = 128
N_EXPERTS = 64
TOP_K = 8
N_EXPERT_GROUPS = 8
TOPK_GROUPS = 4
ROUTED_SCALE = 2.5
MOE_BLOCK = 128

VMEM_LIMIT_BYTES = 56 * 1024 * 1024
LANES = 128
SUBLANES = 8


def _cparams(n_axes):
    return pltpu.CompilerParams(
        dimension_semantics=("arbitrary",) * n_axes,
        vmem_limit_bytes=VMEM_LIMIT_BYTES)


def _silu(v):
    return v * jax.nn.sigmoid(v)


def _dot(a, b):
    return jnp.dot(a, b, preferred_element_type=F32)


def _split3(a):
    hi = a.astype(BF16)
    r1 = a - hi.astype(F32)
    mid = r1.astype(BF16)
    lo = (r1 - mid.astype(F32)).astype(BF16)
    return hi, mid, lo


def _dot_exact_rhs(a, sel):
    hi, mid, lo = _split3(a)
    return _dot(hi, sel) + _dot(mid, sel) + _dot(lo, sel)


def _dot_exact_lhs(sel, a):
    hi, mid, lo = _split3(a)
    return _dot(sel, hi) + _dot(sel, mid) + _dot(sel, lo)


def _ada_kernel(c_ref, w_ref, b_ref, o_ref):
    c = _silu(c_ref[...])
    o_ref[0] = _dot(c.astype(BF16), w_ref[0].astype(BF16)) + b_ref[0]


def _ada(cond, ada_w, ada_b):
    depth, d, n = ada_w.shape
    tn = 1024
    rows = cond.shape[0]
    return pl.pallas_call(
        _ada_kernel,
        out_shape=jax.ShapeDtypeStruct((depth, rows, n), F32),
        grid=(depth, n // tn),
        in_specs=[pl.BlockSpec((rows, d), lambda l, j: (0, 0)),
                  pl.BlockSpec((1, d, tn), lambda l, j: (l, 0, j)),
                  pl.BlockSpec((1, 1, tn), lambda l, j: (l, 0, j))],
        out_specs=pl.BlockSpec((1, rows, tn), lambda l, j: (l, 0, j)),
        compiler_params=_cparams(2),
        name="ada",
    )(cond, ada_w, ada_b.reshape(depth, 1, n))


def _mm_kernel(*refs, n_w, has_bias, epilogue):
    x_ref = refs[0]
    w_refs = refs[1:1 + n_w]
    pos = 1 + n_w
    b_refs = refs[pos:pos + n_w] if has_bias else ()
    pos += n_w if has_bias else 0
    o_ref = refs[pos]
    wbf_refs = refs[pos + 1:pos + 1 + n_w]

    @pl.when(pl.program_id(1) == 0)
    def _():
        for w_ref, wbf in zip(w_refs, wbf_refs):
            wbf[...] = w_ref[...].astype(BF16)

    x = x_ref[...].astype(BF16)
    zs = []
    for k in range(n_w):
        z = _dot(x, wbf_refs[k][...])
        if has_bias:
            z = z + b_refs[k][...]
        zs.append(z)
    if epilogue is None:
        out = zs[0]
    elif epilogue == "softplus":
        out = jax.nn.softplus(zs[0])
    elif epilogue == "glu":
        out = zs[0] * jax.nn.sigmoid(zs[1])
    elif epilogue == "swiglu":
        out = _silu(zs[0]) * zs[1]
    o_ref[...] = out.astype(o_ref.dtype)


def _matmul(x, ws, col_offsets, n_out, *, tn, tm, biases=None, epilogue=None,
            out_dtype=F32, name="matmul"):
    m, k = x.shape
    n_w = len(ws)
    has_bias = biases is not None
    in_specs = [pl.BlockSpec((tm, k), lambda j, i: (i, 0))]
    for off in col_offsets:
        in_specs.append(pl.BlockSpec((k, tn), lambda j, i, off=off: (0, j + off)))
    args = [x] + list(ws)
    if has_bias:
        for off in col_offsets:
            in_specs.append(pl.BlockSpec((1, tn), lambda j, i, off=off: (0, j + off)))
        args += [b.reshape(1, -1) for b in biases]
    return pl.pallas_call(
        functools.partial(_mm_kernel, n_w=n_w, has_bias=has_bias, epilogue=epilogue),
        out_shape=jax.ShapeDtypeStruct((m, n_out), out_dtype),
        grid=(n_out // tn, m // tm),
        in_specs=in_specs,
        out_specs=pl.BlockSpec((tm, tn), lambda j, i: (i, j)),
        scratch_shapes=[pltpu.VMEM((k, tn), BF16) for _ in range(n_w)],
        compiler_params=_cparams(2),
        name=name,
    )(*args)


SLABS = 8
ROW_TILE = 256


def _get_piece(ref, mode, j, rows, d):
    if mode == "slab":
        return ref[:, j * d:(j + 1) * d]
    return ref[j * rows:(j + 1) * rows, :]


def _put_piece(ref, mode, j, rows, d, val):
    if mode == "slab":
        ref[:, j * d:(j + 1) * d] = val.astype(ref.dtype)
    else:
        ref[j * rows:(j + 1) * rows, :] = val.astype(ref.dtype)


def _resnorm_kernel(*refs, has_y, write_x, x_mode, y_mode, xo_mode, v_mode, modulate, rows, d):
    it = iter(refs)
    x_ref = next(it)
    y_ref = next(it) if has_y else None
    g_ref = next(it) if has_y else None
    nw_ref = next(it)
    sh_ref = next(it) if modulate else None
    sc_ref = next(it) if modulate else None
    xo_ref = next(it) if write_x else None
    v_ref = next(it)
    nw = nw_ref[...]
    for j in range(SLABS):
        x = _get_piece(x_ref, x_mode, j, rows, d)
        if has_y:
            y = _get_piece(y_ref, y_mode, j, rows, d).astype(F32)
            x = x + g_ref[0] * y
            if write_x:
                _put_piece(xo_ref, xo_mode, j, rows, d, x)
        v = x * lax.rsqrt(jnp.mean(x * x, axis=-1, keepdims=True) + EPS) * nw
        if modulate:
            v = v * (1.0 + sc_ref[0]) + sh_ref[0]
        _put_piece(v_ref, v_mode, j, rows, d, v)


def _resnorm(x, y, gate, norm_w, shift, scale, *, n_batch, seq, x_mode="row",
             y_mode="row", xo_mode="row", v_mode="row", v_dtype=BF16, write_x=True,
             x_tile0=0, y_tile0=0, vec_index=None, name="resnorm"):
    d = x.shape[1]
    n = n_batch * seq
    if vec_index is None:
        vec_index = lambda b, t: b
    has_y = y is not None
    write_x = write_x and has_y
    modulate = shift is not None
    grows = seq // GRID_W
    tiles_per_seq = seq // ROW_TILE
    slab_used = "slab" in (x_mode, y_mode, xo_mode, v_mode)
    rows = grows if slab_used else ROW_TILE // SLABS
    if slab_used:
        assert grows * SLABS == ROW_TILE

    def spec(mode, tile0=0):
        if mode == "slab":
            return pl.BlockSpec((grows, SLABS * d), lambda b, t: (b, t))
        return pl.BlockSpec((ROW_TILE, d), lambda b, t: (b * tiles_per_seq + t + tile0, 0))

    def view(a, mode):
        return a.reshape(a.shape[0] // GRID_W, GRID_W * d) if mode == "slab" else a

    vec = pl.BlockSpec((1, 1, d), lambda b, t: (vec_index(b, t), 0, 0))
    in_specs = [spec(x_mode, x_tile0)]
    args = [view(x, x_mode)]
    if has_y:
        in_specs += [spec(y_mode, y_tile0), vec]
        args += [view(y, y_mode), gate]
    in_specs.append(pl.BlockSpec((1, d), lambda b, t: (0, 0)))
    args.append(norm_w.reshape(1, d))
    if modulate:
        in_specs += [vec, vec]
        args += [shift, scale]
    out_shape, out_specs = [], []
    def out_struct(mode, dtype):
        shape = (n // GRID_W, GRID_W * d) if mode == "slab" else (n, d)
        return jax.ShapeDtypeStruct(shape, dtype)

    if write_x:
        out_shape.append(out_struct(xo_mode, F32))
        out_specs.append(spec(xo_mode))
    out_shape.append(out_struct(v_mode, v_dtype))
    out_specs.append(spec(v_mode))
    outs = pl.pallas_call(
        functools.partial(_resnorm_kernel, has_y=has_y, write_x=write_x, x_mode=x_mode, y_mode=y_mode,
                          xo_mode=xo_mode, v_mode=v_mode, modulate=modulate, rows=rows, d=d),
        out_shape=out_shape,
        grid=(n_batch, tiles_per_seq),
        in_specs=in_specs,
        out_specs=out_specs,
        compiler_params=_cparams(2),
        name=name,
    )(*args)
    outs = [o.reshape(n, d) for o in outs]
    return (outs[0], outs[1]) if write_x else (None, outs[0])


S5_T = 128
S5_CB = 128
S5_NS = (S5_CB // S5_GROUP) * S5_STATE


def _reverse_rows(v_bf16):
    t = v_bf16.shape[0]
    r = lax.broadcasted_iota(jnp.int32, (t, t), 0)
    c = lax.broadcasted_iota(jnp.int32, (t, t), 1)
    flip = jnp.where(r + c == t - 1, 1.0, 0.0).astype(BF16)
    return _dot(flip, v_bf16)


def _s5_prep_kernel(xf_ref, xb_ref, nw_ref, sh_ref, sc_ref, o_ref, *, nb, d):
    nw = nw_ref[...]
    for b in range(nb):
        for rev, x_ref in ((False, xf_ref), (True, xb_ref)):
            x = x_ref[b]
            v = x * lax.rsqrt(jnp.mean(x * x, axis=-1, keepdims=True) + EPS) * nw
            v = (v * (1.0 + sc_ref[b]) + sh_ref[b]).astype(BF16)
            if rev:
                v = _reverse_rows(v).astype(BF16)
            k = b + (nb if rev else 0)
            o_ref[:, k * d:(k + 1) * d] = v


def _s5_prep(x3, norm_w, shift, scale):
    nb, s, d = x3.shape
    nt = s // S5_T
    return pl.pallas_call(
        functools.partial(_s5_prep_kernel, nb=nb, d=d),
        out_shape=jax.ShapeDtypeStruct((s, 2 * nb * d), BF16),
        grid=(nt,),
        in_specs=[pl.BlockSpec((nb, S5_T, d), lambda t: (0, t, 0)),
                  pl.BlockSpec((nb, S5_T, d), lambda t: (0, nt - 1 - t, 0)),
                  pl.BlockSpec((1, d), lambda t: (0, 0)),
                  pl.BlockSpec((nb, 1, d), lambda t: (0, 0, 0)),
                  pl.BlockSpec((nb, 1, d), lambda t: (0, 0, 0))],
        out_specs=pl.BlockSpec((S5_T, 2 * nb * d), lambda t: (t, 0)),
        compiler_params=_cparams(1),
        name="s5_prep",
    )(x3, x3, norm_w.reshape(1, d), shift, scale)


def _s5_scan_kernel(u_ref, bc_ref, are_ref, aim_ref, cc_ref, s0_ref, y_ref, sf_ref,
                    bu_ref, st_ref, *, n_tiles):
    i = pl.program_id(1)
    ns = S5_NS

    @pl.when(i == 0)
    def _():
        st_ref[...] = s0_ref[0]

    u = u_ref[...].astype(F32)
    row = lax.broadcasted_iota(jnp.int32, u.shape, 0)
    is_fwd = jnp.bitwise_and(row, SUBLANES - 1) < (SUBLANES // 2)
    lhs = jnp.concatenate([jnp.where(is_fwd, u, 0.0), jnp.where(is_fwd, 0.0, u)], axis=1)
    bu_ref[...] = _dot(lhs.astype(BF16), bc_ref[0])
    a_re = are_ref[0]
    a_im = aim_ref[0]

    def step(t, carry):
        s_re, s_im = carry
        r = pl.multiple_of(t * SUBLANES, SUBLANES)
        n_re = a_re * s_re - a_im * s_im + bu_ref[pl.ds(r, SUBLANES), 0:ns]
        n_im = a_re * s_im + a_im * s_re + bu_ref[pl.ds(r, SUBLANES), ns:2 * ns]
        bu_ref[pl.ds(r, SUBLANES), 0:ns] = n_re
        bu_ref[pl.ds(r, SUBLANES), ns:2 * ns] = n_im
        return n_re, n_im

    s_re, s_im = lax.fori_loop(0, S5_T, step, (st_ref[:, 0:ns], st_ref[:, ns:2 * ns]))
    st_ref[:, 0:ns] = s_re
    st_ref[:, ns:2 * ns] = s_im
    y2 = _dot(bu_ref[...].astype(BF16), cc_ref[0])
    y_ref[...] = jnp.where(is_fwd, y2[:, 0:S5_CB], y2[:, S5_CB:2 * S5_CB])

    @pl.when(i == n_tiles - 1)
    def _():
        sf_ref[0] = st_ref[...]


def _s5_scan(u2r, bcat, a_re, a_im, ccat, s0):
    rows, d = u2r.shape
    nblk = d // S5_CB
    tr = S5_T * SUBLANES
    n_tiles = rows // tr
    ns2 = 2 * S5_NS
    return pl.pallas_call(
        functools.partial(_s5_scan_kernel, n_tiles=n_tiles),
        out_shape=(jax.ShapeDtypeStruct((rows, d), F32),
                   jax.ShapeDtypeStruct((nblk, SUBLANES, ns2), F32)),
        grid=(nblk, n_tiles),
        in_specs=[pl.BlockSpec((tr, S5_CB), lambda j, i: (i, j)),
                  pl.BlockSpec((1, 2 * S5_CB, ns2), lambda j, i: (j, 0, 0)),
                  pl.BlockSpec((1, SUBLANES, S5_NS), lambda j, i: (j, 0, 0)),
                  pl.BlockSpec((1, SUBLANES, S5_NS), lambda j, i: (j, 0, 0)),
                  pl.BlockSpec((1, ns2, 2 * S5_CB), lambda j, i: (j, 0, 0)),
                  pl.BlockSpec((1, SUBLANES, ns2), lambda j, i: (j, 0, 0))],
        out_specs=(pl.BlockSpec((tr, S5_CB), lambda j, i: (i, j)),
                   pl.BlockSpec((1, SUBLANES, ns2), lambda j, i: (j, 0, 0))),
        scratch_shapes=[pltpu.VMEM((tr, ns2), F32), pltpu.VMEM((SUBLANES, ns2), F32)],
        compiler_params=_cparams(2),
        name="s5_scan",
    )(u2r, bcat, a_re, a_im, ccat, s0)


def _s5_out_kernel(yf_ref, yb_ref, u_ref, skip_ref, o_ref):
    yb = yb_ref[...]
    hi = yb.astype(BF16)
    lo = (yb - hi.astype(F32)).astype(BF16)
    y = (skip_ref[...] * u_ref[...].astype(F32) + yf_ref[...]
         + _reverse_rows(hi) + _reverse_rows(lo))
    o_ref[...] = jax.nn.gelu(y).astype(o_ref.dtype)


def _s5_out(y2, u2, skip, nb, d):
    s = y2.shape[0]
    nt = s // S5_T
    return pl.pallas_call(
        _s5_out_kernel,
        out_shape=jax.ShapeDtypeStruct((nb * s, d), BF16),
        grid=(nb, nt),
        in_specs=[pl.BlockSpec((S5_T, d), lambda b, t: (t, b)),
                  pl.BlockSpec((S5_T, d), lambda b, t: (nt - 1 - t, nb + b)),
                  pl.BlockSpec((S5_T, d), lambda b, t: (t, b)),
                  pl.BlockSpec((1, d), lambda b, t: (0, 0))],
        out_specs=pl.BlockSpec((S5_T, d), lambda b, t: (b * nt + t, 0)),
        compiler_params=_cparams(2),
        name="s5_out",
    )(y2, y2, u2, skip.reshape(1, d))


def _s5_pack_params(lam_re, lam_im, log_step, b_re, b_im, c_re, c_im, nb):
    f32 = F32
    g = lam_re.shape[1]
    gpb = S5_CB // S5_GROUP
    nblk = g // gpb
    eye = jnp.eye(gpb, dtype=f32)
    a_re_rows, a_im_rows, b_parts, c_parts = [], [], [], []
    for dr in range(2):
        lr, li = lam_re[dr].astype(f32), lam_im[dr].astype(f32)
        br, bi = b_re[dr].astype(f32), b_im[dr].astype(f32)
        dt = jnp.exp(log_step[dr].astype(f32))[:, None]
        mag = jnp.exp(lr * dt)
        abar_re = mag * jnp.cos(li * dt)
        abar_im = mag * jnp.sin(li * dt)
        num_re = abar_re - 1.0
        num_im = abar_im
        den = lr * lr + li * li
        f_re = (num_re * lr + num_im * li) / den
        f_im = (num_im * lr - num_re * li) / den
        bbar_re = f_re[..., None] * br - f_im[..., None] * bi
        bbar_im = f_re[..., None] * bi + f_im[..., None] * br
        a_re_rows.append(jnp.broadcast_to(abar_re.reshape(nblk, 1, S5_NS), (nblk, nb, S5_NS)))
        a_im_rows.append(jnp.broadcast_to(abar_im.reshape(nblk, 1, S5_NS), (nblk, nb, S5_NS)))

        def blockdiag_in(bb):
            b4 = bb.reshape(nblk, gpb, S5_STATE, S5_GROUP)
            return jnp.einsum('jgpk,gh->jgkhp', b4, eye).reshape(nblk, S5_CB, S5_NS)

        def blockdiag_out(cc):
            c4 = cc.astype(f32).reshape(nblk, gpb, S5_GROUP, S5_STATE)
            return jnp.einsum('jgkp,gh->jgphk', c4, eye).reshape(nblk, S5_NS, S5_CB)

        b_parts.append(jnp.concatenate([blockdiag_in(bbar_re), blockdiag_in(bbar_im)], axis=2))
        c_parts.append(jnp.concatenate([blockdiag_out(c_re[dr]), -blockdiag_out(c_im[dr])], axis=1))
    a_re = jnp.concatenate(a_re_rows, axis=1)
    a_im = jnp.concatenate(a_im_rows, axis=1)
    bcat = jnp.concatenate(b_parts, axis=1).astype(BF16)
    ccat = jnp.concatenate(c_parts, axis=2).astype(BF16)
    return bcat, a_re, a_im, ccat


def _conv_kernel(x_ref, w_ref, b_ref, o_ref, pad_ref, *, seq):
    halo = SUBLANES
    zeros = jnp.zeros((halo, pad_ref.shape[1]), F32)
    pad_ref[0:halo, :] = zeros
    pad_ref[halo + seq:2 * halo + seq, :] = zeros
    pad_ref[halo:halo + seq, :] = x_ref[...]
    acc = jnp.zeros((seq, pad_ref.shape[1]), F32) + b_ref[...]
    for k in range(SSD_CONV):
        off = halo + k - SSD_CONV // 2
        acc = acc + w_ref[k:k + 1, :] * pad_ref[off:off + seq, :]
    o_ref[...] = _silu(acc).astype(o_ref.dtype)


def _conv_silu(xbc, conv_w, conv_b, seq, row_block_offset, n_seq, out_dtype=BF16):
    _, c = xbc.shape
    tc = 512
    return pl.pallas_call(
        functools.partial(_conv_kernel, seq=seq),
        out_shape=jax.ShapeDtypeStruct((n_seq * seq, c), out_dtype),
        grid=(n_seq, c // tc),
        in_specs=[pl.BlockSpec((seq, tc), lambda b, j: (b + row_block_offset, j)),
                  pl.BlockSpec((SSD_CONV, tc), lambda b, j: (0, j)),
                  pl.BlockSpec((1, tc), lambda b, j: (0, j))],
        out_specs=pl.BlockSpec((seq, tc), lambda b, j: (b, j)),
        scratch_shapes=[pltpu.VMEM((seq + 2 * SUBLANES, tc), F32)],
        compiler_params=_cparams(2),
        name="ssd_conv",
    )(xbc, conv_w, conv_b.reshape(1, c))


HPG = 8
GCH = HPG * SSD_HEADDIM
DTC = 2 * HPG


def _ssd_kernel(xl_ref, bl_ref, btl_ref, cl_ref, dtl_ref, dttl_ref,
                xc_ref, btc_ref, dtc_ref, dttc_ref,
                z_ref, acol_ref, arow_ref, skip_ref, nw_ref, o_ref,
                ht_ref, yacc_ref, *, n_lat, n_ctx):
    q = SSD_CHUNK
    r_i = lax.broadcasted_iota(jnp.int32, (q, q), 0)
    c_i = lax.broadcasted_iota(jnp.int32, (q, q), 1)
    tril = jnp.where(c_i <= r_i, 1.0, 0.0).astype(BF16)
    triu = jnp.where(r_i <= c_i, 1.0, 0.0).astype(BF16)
    lower = c_i <= r_i
    upper = c_i >= r_i
    lane = lax.broadcasted_iota(jnp.int32, (q, 2 * SSD_HEADDIM), 1)
    left = lane < SSD_HEADDIM
    e_r = lax.broadcasted_iota(jnp.int32, (DTC, GCH), 0)
    e_c = lax.broadcasted_iota(jnp.int32, (DTC, GCH), 1)
    s_r = lax.broadcasted_iota(jnp.int32, (DTC, HPG * q), 0)
    s_c = lax.broadcasted_iota(jnp.int32, (DTC, HPG * q), 1)
    a_row = arow_ref[0]
    a_col = acol_ref[0]

    def chunk(x_ref, b_ref, bt_ref, c_ref, dt_ref, dtt_ref, r0, dr, with_y, first):
        head_of_ch = lax.shift_right_logical(e_c, int(math.log2(SSD_HEADDIM)))
        head_of_col = lax.shift_right_logical(s_c, int(math.log2(q)))
        expand = jnp.where(head_of_ch + dr * HPG == e_r, 1.0, 0.0).astype(BF16)
        bcast = jnp.where(head_of_col + dr * HPG == s_r, 1.0, 0.0).astype(BF16)
        x = x_ref[pl.ds(r0, q), :].astype(F32)
        bt = bt_ref[0, :, pl.ds(r0, q)]
        dt = dt_ref[0, pl.ds(r0, q), :]
        dtt = dtt_ref[0, :, pl.ds(r0, q)]
        la = dt * a_row
        lat = dtt * a_col
        cs = _dot_exact_lhs(tril, la)
        cst = _dot_exact_rhs(lat, triu)
        csx = _dot_exact_rhs(cs, expand)
        totx = csx[q - 1:q, :]
        dtx = _dot_exact_rhs(dt, expand)
        if dr == 1:
            rk, rkt = cs - la, cst - lat
            rkx = csx - _dot_exact_rhs(la, expand)
        else:
            rk, rkt, rkx = cs, cst, csx
        xd = x * dtx
        if dr == 0:
            w_state = jnp.exp(totx - rkx)
            w_off = jnp.exp(rkx)
        else:
            w_state = jnp.exp(rkx)
            w_off = jnp.exp(totx - rkx)
        h_old = ht_ref[...]
        ht_ref[...] = h_old * jnp.exp(totx) + _dot(bt, (xd * w_state).astype(BF16))
        if not with_y:
            return
        cm = c_ref[pl.ds(r0, q), :]
        cb = _dot(cm, bt)
        rkb = _dot_exact_rhs(rk, bcast)
        xdb = xd.astype(BF16)
        pieces = []
        for pair in range(HPG // 2):
            ms = []
            for hh in (2 * pair, 2 * pair + 1):
                col = dr * HPG + hh
                colv = rkb[:, hh * q:(hh + 1) * q]
                rowv = rkt[col:col + 1, :]
                if dr == 0:
                    seg = jnp.where(lower, colv - rowv, -1e30)
                else:
                    seg = jnp.where(upper, rowv - colv, -1e30)
                ms.append((cb * jnp.exp(seg)).astype(BF16))
            xp = xdb[:, pair * 2 * SSD_HEADDIM:(pair + 1) * 2 * SSD_HEADDIM]
            zero = jnp.zeros_like(xp)
            rhs = jnp.concatenate([jnp.where(left, xp, zero), jnp.where(left, zero, xp)], axis=0)
            pieces.append(_dot(jnp.concatenate(ms, axis=1), rhs))
        y = jnp.concatenate(pieces, axis=1) + _dot(cm, h_old.astype(BF16)) * w_off
        if first:
            yacc_ref[pl.ds(r0, q), :] = y + skip_ref[...] * x
        else:
            y = yacc_ref[pl.ds(r0, q), :] + y
            v = y * _silu(z_ref[pl.ds(r0, q), :].astype(F32))
            v = v * lax.rsqrt(jnp.mean(v * v, axis=-1, keepdims=True) + EPS) * nw_ref[...]
            o_ref[pl.ds(r0, q), :] = v.astype(o_ref.dtype)

    for dr in range(2):
        ht_ref[...] = jnp.zeros_like(ht_ref)

        def ctx_body(k, _):
            kk = k if dr == 0 else n_ctx - 1 - k
            r0 = pl.multiple_of(kk * q, q)
            chunk(xc_ref, None, btc_ref, None, dtc_ref, dttc_ref, r0, dr, False, False)
            return 0

        lax.fori_loop(0, n_ctx, ctx_body, 0)

        def lat_body(k, _):
            kk = k if dr == 0 else n_lat - 1 - k
            r0 = pl.multiple_of(kk * q, q)
            chunk(xl_ref, bl_ref, btl_ref, cl_ref, dtl_ref, dttl_ref, r0, dr, True, dr == 0)
            return 0

        lax.fori_loop(0, n_lat, lat_body, 0)


def _ssd_scan(xc_all, bt_all, z, dtg, dtgt, a_col, a_row, skip, norm_w, nb, seq, ctx_len):
    n_lat, n_ctx = seq // SSD_CHUNK, ctx_len // SSD_CHUNK
    inner = SSD_GROUPS * GCH
    xoff = 0
    boff = inner // SSD_STATE
    coff = boff + SSD_GROUPS
    cb0 = nb * seq // ctx_len
    in_specs = [
        pl.BlockSpec((seq, GCH), lambda b, g: (b, g)),
        pl.BlockSpec((seq, SSD_STATE), lambda b, g: (b, boff + g)),
        pl.BlockSpec((1, SSD_STATE, seq), lambda b, g: (g, 0, b)),
        pl.BlockSpec((seq, SSD_STATE), lambda b, g: (b, coff + g)),
        pl.BlockSpec((1, seq, DTC), lambda b, g: (g, b, 0)),
        pl.BlockSpec((1, DTC, seq), lambda b, g: (g, 0, b)),
        pl.BlockSpec((ctx_len, GCH), lambda b, g: (cb0 + b, g)),
        pl.BlockSpec((1, SSD_STATE, ctx_len), lambda b, g: (g, 0, cb0 + b)),
        pl.BlockSpec((1, ctx_len, DTC), lambda b, g: (g, cb0 + b, 0)),
        pl.BlockSpec((1, DTC, ctx_len), lambda b, g: (g, 0, cb0 + b)),
        pl.BlockSpec((seq, GCH), lambda b, g: (b, g)),
        pl.BlockSpec((1, DTC, 1), lambda b, g: (g, 0, 0)),
        pl.BlockSpec((1, 1, DTC), lambda b, g: (g, 0, 0)),
        pl.BlockSpec((1, GCH), lambda b, g: (0, g)),
        pl.BlockSpec((1, GCH), lambda b, g: (0, g)),
    ]
    del xoff
    return pl.pallas_call(
        functools.partial(_ssd_kernel, n_lat=n_lat, n_ctx=n_ctx),
        out_shape=jax.ShapeDtypeStruct((nb * seq, inner), BF16),
        grid=(nb, SSD_GROUPS),
        in_specs=in_specs,
        out_specs=pl.BlockSpec((seq, GCH), lambda b, g: (b, g)),
        scratch_shapes=[pltpu.VMEM((SSD_STATE, GCH), F32), pltpu.VMEM((seq, GCH), F32)],
        compiler_params=_cparams(2),
        name="ssd_scan",
    )(xc_all, xc_all, bt_all, xc_all, dtg, dtgt, xc_all, bt_all, dtg, dtgt,
      z, a_col, a_row, skip, norm_w)


def _expert_kernel(be_ref, nu_ref, x_ref, wg_ref, wu_ref, wd_ref, o_ref, wgb, wub, wdb):
    i = pl.program_id(0)
    prev = be_ref[jnp.maximum(i - 1, 0)]
    changed = jnp.logical_or(i == 0, be_ref[i] != prev)

    @pl.when(changed)
    def _():
        wgb[...] = wg_ref[0].astype(BF16)
        wub[...] = wu_ref[0].astype(BF16)
        wdb[...] = wd_ref[0].astype(BF16)

    @pl.when(i < nu_ref[0])
    def _():
        x = x_ref[...].astype(BF16)
        h = _silu(_dot(x, wgb[...])) * _dot(x, wub[...])
        o_ref[...] = _dot(h.astype(BF16), wdb[...]).astype(o_ref.dtype)

    @pl.when(i >= nu_ref[0])
    def _():
        o_ref[...] = jnp.zeros_like(o_ref)


def _experts(x_sorted, block_e, n_used, w_gate, w_up, w_down):
    cap, d = x_sorted.shape
    ff = w_gate.shape[2]
    n_blocks = cap // MOE_BLOCK

    def row_block(i, be, nu):
        return (jnp.minimum(i, nu[0] - 1), 0)

    grid_spec = pltpu.PrefetchScalarGridSpec(
        num_scalar_prefetch=2,
        grid=(n_blocks,),
        in_specs=[pl.BlockSpec((MOE_BLOCK, d), row_block),
                  pl.BlockSpec((1, d, ff), lambda i, be, nu: (be[i], 0, 0)),
                  pl.BlockSpec((1, d, ff), lambda i, be, nu: (be[i], 0, 0)),
                  pl.BlockSpec((1, ff, d), lambda i, be, nu: (be[i], 0, 0))],
        out_specs=pl.BlockSpec((MOE_BLOCK, d), lambda i, be, nu: (i, 0)),
        scratch_shapes=[pltpu.VMEM((d, ff), BF16), pltpu.VMEM((d, ff), BF16),
                        pltpu.VMEM((ff, d), BF16)])
    return pl.pallas_call(
        _expert_kernel,
        out_shape=jax.ShapeDtypeStruct((cap, d), F32),
        grid_spec=grid_spec,
        compiler_params=_cparams(1),
        name="moe_experts",
    )(block_e, n_used, x_sorted, w_gate, w_up, w_down)


DISPATCH_T = 256
COMBINE_T = 128


def _dispatch_kernel(dest_ref, v_ref, init_ref, xs_ref, sem):
    del init_ref
    t_rows = v_ref.shape[0]

    def issue(t, carry):
        for k in range(TOP_K):
            pltpu.make_async_copy(v_ref.at[pl.ds(t, 1)],
                                  xs_ref.at[pl.ds(dest_ref[k, t], 1)], sem).start()
        return carry

    lax.fori_loop(0, t_rows, issue, 0)
    for k in range(TOP_K):
        pltpu.make_async_copy(v_ref, xs_ref.at[pl.ds(0, t_rows)], sem).wait()


def _dispatch(v, dest, cap):
    n, d = v.shape
    return pl.pallas_call(
        _dispatch_kernel,
        out_shape=jax.ShapeDtypeStruct((cap, d), F32),
        grid=(n // DISPATCH_T,),
        in_specs=[pl.BlockSpec((TOP_K, DISPATCH_T), lambda i: (0, i), memory_space=pltpu.SMEM),
                  pl.BlockSpec((DISPATCH_T, d), lambda i: (i, 0)),
                  pl.BlockSpec(memory_space=pl.ANY)],
        out_specs=pl.BlockSpec(memory_space=pl.ANY),
        scratch_shapes=[pltpu.SemaphoreType.DMA(())],
        input_output_aliases={2: 0},
        compiler_params=_cparams(1),
        name="moe_dispatch",
    )(dest, v, jnp.zeros((cap, d), F32))


def _combine_kernel(dest_ref, y_ref, gate_ref, sh_ref, o_ref, buf, sem):
    t_rows = o_ref.shape[0]

    def issue(t, carry):
        for k in range(TOP_K):
            pltpu.make_async_copy(y_ref.at[pl.ds(dest_ref[k, t], 1)],
                                  buf.at[k, pl.ds(t, 1)], sem).start()
        return carry

    lax.fori_loop(0, t_rows, issue, 0)
    for k in range(TOP_K):
        pltpu.make_async_copy(y_ref.at[pl.ds(0, t_rows)], buf.at[k], sem).wait()
    acc = sh_ref[...]
    for k in range(TOP_K):
        acc = acc + gate_ref[:, k:k + 1] * buf[k]
    o_ref[...] = acc


def _combine(y_sorted, dest, gate_t, shared):
    n, d = shared.shape
    return pl.pallas_call(
        _combine_kernel,
        out_shape=jax.ShapeDtypeStruct((n, d), F32),
        grid=(n // COMBINE_T,),
        in_specs=[pl.BlockSpec((TOP_K, COMBINE_T), lambda i: (0, i), memory_space=pltpu.SMEM),
                  pl.BlockSpec(memory_space=pl.ANY),
                  pl.BlockSpec((COMBINE_T, TOP_K), lambda i: (i, 0)),
                  pl.BlockSpec((COMBINE_T, d), lambda i: (i, 0))],
        out_specs=pl.BlockSpec((COMBINE_T, d), lambda i: (i, 0)),
        scratch_shapes=[pltpu.VMEM((TOP_K, COMBINE_T, d), F32), pltpu.SemaphoreType.DMA(())],
        compiler_params=_cparams(1),
        name="moe_combine",
    )(dest, y_sorted, gate_t, shared)


ROUTE_T = 512
GROUP_SIZE = N_EXPERTS // N_EXPERT_GROUPS


def _route_kernel(v_ref, rwh_ref, rwl_ref, bias_ref, eid_ref, rank_ref, gate_ref, cnt_ref,
                  carry_ref, *, n_tiles):
    i = pl.program_id(0)
    t = v_ref.shape[0]
    ng, gs = N_EXPERT_GROUPS, GROUP_SIZE
    neg = -jnp.inf

    @pl.when(i == 0)
    def _():
        carry_ref[...] = jnp.zeros_like(carry_ref)

    v = v_ref[...].astype(BF16)
    nt_dims = (((1,), (1,)), ((), ()))
    logits = (lax.dot_general(rwh_ref[...], v, nt_dims, preferred_element_type=F32)
              + lax.dot_general(rwl_ref[...], v, nt_dims, preferred_element_type=F32))
    scores = jax.nn.sigmoid(logits)
    biased = scores + bias_ref[:, 0:1]
    x3 = biased.reshape(ng, gs, t)
    s3 = scores.reshape(ng, gs, t)
    mi = lax.broadcasted_iota(jnp.int32, (ng, gs, t), 1).astype(F32)
    fi = lax.broadcasted_iota(jnp.int32, (ng, gs, t), 0).astype(F32) * gs + mi
    gi = lax.broadcasted_iota(jnp.int32, (ng, 1, t), 0).astype(F32)

    m1 = jnp.max(x3, axis=1, keepdims=True)
    i1 = jnp.min(jnp.where(x3 == m1, mi, float(gs)), axis=1, keepdims=True)
    m2 = jnp.max(jnp.where(mi == i1, neg, x3), axis=1, keepdims=True)
    cur = m1 + m2
    gmask = jnp.zeros((ng, 1, t), F32)
    for _ in range(TOPK_GROUPS):
        gm = jnp.max(cur, axis=0, keepdims=True)
        idx = jnp.min(jnp.where(cur == gm, gi, float(ng)), axis=0, keepdims=True)
        hit = gi == idx
        gmask = jnp.where(hit, 1.0, gmask)
        cur = jnp.where(hit, neg, cur)

    cand = jnp.where(gmask > 0.0, x3, neg)
    sel = jnp.zeros((ng, gs, t), F32)
    eids = []
    for _ in range(TOP_K):
        m = jnp.max(jnp.max(cand, axis=1, keepdims=True), axis=0, keepdims=True)
        idx = jnp.where(cand == m, fi, float(N_EXPERTS))
        idx = jnp.min(jnp.min(idx, axis=1, keepdims=True), axis=0, keepdims=True)
        hit = fi == idx
        sel = jnp.where(hit, 1.0, sel)
        cand = jnp.where(hit, neg, cand)
        eids.append(idx)

    selr = sel.reshape(N_EXPERTS, t)
    r_i = lax.broadcasted_iota(jnp.int32, (t, t), 0)
    c_i = lax.broadcasted_iota(jnp.int32, (t, t), 1)
    before = jnp.where(r_i < c_i, 1.0, 0.0).astype(BF16)
    rank = _dot(selr.astype(BF16), before) + carry_ref[:, 0:1]
    carry_ref[...] = carry_ref[...] + jnp.sum(selr, axis=1, keepdims=True)
    rank3 = rank.reshape(ng, gs, t)

    gsel = sel * s3
    denom = jnp.sum(jnp.sum(gsel, axis=1, keepdims=True), axis=0, keepdims=True)
    gate3 = gsel / denom * ROUTED_SCALE

    def pick(a3, hit):
        return jnp.sum(jnp.sum(jnp.where(hit, a3, 0.0), axis=1, keepdims=True), axis=0,
                       keepdims=True).reshape(1, t)

    for k in range(TOP_K):
        hit = fi == eids[k]
        eid_ref[k:k + 1, :] = eids[k].reshape(1, t).astype(jnp.int32)
        rank_ref[k:k + 1, :] = pick(rank3, hit).astype(jnp.int32)
        gate_ref[k:k + 1, :] = pick(gate3, hit)

    @pl.when(i == n_tiles - 1)
    def _():
        cnt_ref[...] = carry_ref[...]


def _route(v, router_w, router_bias):
    n, d = v.shape
    n_tiles = n // ROUTE_T
    rwt = router_w.astype(F32).T
    rwh = rwt.astype(BF16)
    rwl = (rwt - rwh.astype(F32)).astype(BF16)
    bias = jnp.broadcast_to(router_bias.astype(F32)[:, None], (N_EXPERTS, LANES))
    slot = pl.BlockSpec((TOP_K, ROUTE_T), lambda i: (0, i))
    full = pl.BlockSpec((N_EXPERTS, d), lambda i: (0, 0))
    return pl.pallas_call(
        functools.partial(_route_kernel, n_tiles=n_tiles),
        out_shape=(jax.ShapeDtypeStruct((TOP_K, n), jnp.int32),
                   jax.ShapeDtypeStruct((TOP_K, n), jnp.int32),
                   jax.ShapeDtypeStruct((TOP_K, n), F32),
                   jax.ShapeDtypeStruct((N_EXPERTS, LANES), F32)),
        grid=(n_tiles,),
        in_specs=[pl.BlockSpec((ROUTE_T, d), lambda i: (i, 0)), full, full,
                  pl.BlockSpec((N_EXPERTS, LANES), lambda i: (0, 0))],
        out_specs=(slot, slot, slot, pl.BlockSpec((N_EXPERTS, LANES), lambda i: (0, 0))),
        scratch_shapes=[pltpu.VMEM((N_EXPERTS, LANES), F32)],
        compiler_params=_cparams(1),
        name="moe_route",
    )(v, rwh, rwl, bias)


def _moe(v, router_w, router_bias, w_gate, w_up, w_down, sw_gate, sw_up, sw_down):
    n, d = v.shape
    eid, rank, gate, cnt = _route(v, router_w, router_bias)
    nk = n * TOP_K
    cap = -(-nk // MOE_BLOCK) * MOE_BLOCK + N_EXPERTS * MOE_BLOCK
    n_blocks = cap // MOE_BLOCK
    counts = cnt[:, 0].astype(jnp.int32)
    padded = (counts + MOE_BLOCK - 1) // MOE_BLOCK * MOE_BLOCK
    pad_end = jnp.cumsum(padded)
    pad_start = pad_end - padded
    experts = jnp.arange(N_EXPERTS, dtype=jnp.int32)
    dest = rank + jnp.sum(jnp.where(eid[None] == experts[:, None, None],
                                    pad_start[:, None, None], 0), axis=0)
    block_start = jnp.arange(n_blocks, dtype=jnp.int32) * MOE_BLOCK
    block_e = jnp.minimum(jnp.sum((pad_end[None, :] <= block_start[:, None]).astype(jnp.int32), axis=1),
                          N_EXPERTS - 1).astype(jnp.int32)
    n_used = (pad_end[-1] // MOE_BLOCK).astype(jnp.int32).reshape(1)
    x_sorted = _dispatch(v, dest, cap)
    y_sorted = _experts(x_sorted, block_e, n_used, w_gate, w_up, w_down)
    hs = _matmul(v, [sw_gate, sw_up], [0, 0], sw_gate.shape[1], tn=sw_gate.shape[1], tm=512,
                 epilogue="swiglu", out_dtype=BF16, name="shared_up")
    shared = _matmul(hs, [sw_down], [0], d, tn=1024, tm=512, name="shared_down")
    return _combine(y_sorted, dest, gate.T, shared)


def kernel(x, c, ctx, c_ctx, ada_w, ada_b, norm1_w, norm2_w, s5_lambda_re, s5_lambda_im, s5_log_step, s5_b_re, s5_b_im, s5_c_re, s5_c_im, s5_d, s5_glu_w, s5_glu_b, ssd_in_w, ssd_conv_w, ssd_conv_b, ssd_dt_bias, ssd_a_log, ssd_d, ssd_norm_w, ssd_out_w, moe_router_w, moe_router_bias, moe_w_gate, moe_w_up, moe_w_down, shared_w_gate, shared_w_up, shared_w_down, final_norm_w):
    nb, seq, d = x.shape
    ctx_len = ctx.shape[1]
    n_lat = nb * seq
    n_ctx = nb * ctx_len

    cond = jnp.concatenate([c, c_ctx[None, :], jnp.zeros((SUBLANES - nb - 1, d), F32)], axis=0)
    mods = _ada(cond, ada_w, ada_b)

    def mod_vecs(layer, k):
        m = mods[layer, :, k * d:(k + 1) * d]
        lat = m[:nb].reshape(nb, 1, d)
        cx = jnp.broadcast_to(m[nb].reshape(1, 1, d), (nb, 1, d))
        return lat, cx

    x_lat = x.reshape(n_lat, d)
    x_ctx = ctx.reshape(n_ctx, d)

    sh_l, sh_c = mod_vecs(0, 0)
    sc_l, sc_c = mod_vecs(0, 1)
    u2_c = _s5_prep(ctx, norm1_w[0], sh_c, sc_c)
    u2_l = _s5_prep(x, norm1_w[0], sh_l, sc_l)
    bcat, a_re, a_im, ccat = _s5_pack_params(
        s5_lambda_re[0], s5_lambda_im[0], s5_log_step[0], s5_b_re[0], s5_b_im[0],
        s5_c_re[0], s5_c_im[0], nb)
    nblk = d // S5_CB
    s0 = jnp.zeros((nblk, SUBLANES, 2 * S5_NS), F32)
    y2_c, s_ctx = _s5_scan(u2_c.reshape(ctx_len * 2 * nb, d), bcat, a_re, a_im, ccat, s0)
    y2_l, _ = _s5_scan(u2_l.reshape(seq * 2 * nb, d), bcat, a_re, a_im, ccat, s_ctx)
    g_l = _s5_out(y2_l.reshape(seq, 2 * nb * d), u2_l, s5_d[0], nb, d)
    g_c = _s5_out(y2_c.reshape(ctx_len, 2 * nb * d), u2_c, s5_d[0], nb, d)
    g_all = jnp.concatenate([g_l, g_c], axis=0)
    half = s5_glu_w.shape[2] // 2
    tn = 1024
    glu = _matmul(g_all, [s5_glu_w[0], s5_glu_w[0]], [0, half // tn], half, tn=tn, tm=512,
                  biases=[s5_glu_b[0], s5_glu_b[0]], epilogue="glu", out_dtype=BF16, name="s5_glu")

    n_all = n_lat + n_ctx
    lat_tiles = seq // ROW_TILE

    def all_vecs(layer, k):
        return mods[layer, :nb + 1, k * d:(k + 1) * d].reshape(nb + 1, 1, d)

    def all_index(b, t):
        return jnp.minimum(t // lat_tiles, nb)

    xs0 = jnp.concatenate([x_lat, x_ctx], axis=0)
    xs1, v_all = _resnorm(xs0, glu, all_vecs(0, 2), norm2_w[0], all_vecs(0, 3), all_vecs(0, 4),
                          n_batch=1, seq=n_all, vec_index=all_index, v_dtype=F32)
    moe0 = _moe(v_all, moe_router_w[0], moe_router_bias[0], moe_w_gate[0], moe_w_up[0],
                moe_w_down[0], shared_w_gate[0], shared_w_up[0], shared_w_down[0])

    g5_l, g5_c = mod_vecs(0, 5)
    sh_l, sh_c = mod_vecs(1, 0)
    sc_l, sc_c = mod_vecs(1, 1)
    x2_l, u_l = _resnorm(xs1, moe0, g5_l, norm1_w[1], sh_l, sc_l, n_batch=nb, seq=seq,
                         x_mode="slab", y_mode="slab", xo_mode="slab", v_mode="row")
    ctx_tile0 = n_lat // ROW_TILE
    _, u_c = _resnorm(xs1, moe0, g5_c, norm1_w[1], sh_c, sc_c, n_batch=nb, seq=ctx_len,
                      write_x=False, x_tile0=ctx_tile0, y_tile0=ctx_tile0)
    u_all = jnp.concatenate([u_l, u_c], axis=0)
    in_w = ssd_in_w[0]
    inner = ssd_out_w.shape[1]
    conv_dim = ssd_conv_w.shape[2]
    heads = inner // SSD_HEADDIM
    tn = 1024
    z_all = _matmul(u_all, [in_w], [0], inner, tn=tn, tm=512, out_dtype=BF16, name="ssd_in_z")
    xbc = _matmul(u_all, [in_w], [inner // tn], conv_dim, tn=tn, tm=512, name="ssd_in_xbc")
    dt_bias = ssd_dt_bias[0].reshape(-1)
    dt_all = _matmul(u_all, [in_w], [(inner + conv_dim) // LANES], 2 * heads, tn=LANES, tm=512,
                     biases=[jnp.pad(dt_bias, (inner + conv_dim, 0))], epilogue="softplus",
                     name="ssd_in_dt")
    xc_l = _conv_silu(xbc, ssd_conv_w[0], ssd_conv_b[0], seq, 0, nb)
    xc_c = _conv_silu(xbc, ssd_conv_w[0], ssd_conv_b[0], ctx_len, n_lat // ctx_len, nb)
    xc_all = jnp.concatenate([xc_l, xc_c], axis=0)
    rows = n_lat + n_ctx
    bt_all = xc_all[:, inner:inner + SSD_GROUPS * SSD_STATE].reshape(rows, SSD_GROUPS, SSD_STATE)
    bt_all = bt_all.transpose(1, 2, 0)
    dtg = dt_all.reshape(rows, 2, SSD_GROUPS, HPG).transpose(2, 0, 1, 3).reshape(SSD_GROUPS, rows, DTC)
    dtgt = dtg.transpose(0, 2, 1)
    a = -jnp.exp(ssd_a_log[0].astype(F32))
    a_g = a.reshape(2, SSD_GROUPS, HPG).transpose(1, 0, 2).reshape(SSD_GROUPS, DTC)
    skip = jnp.repeat(ssd_d[0].astype(F32), SSD_HEADDIM).reshape(1, inner)
    yn = _ssd_scan(xc_all, bt_all, z_all, dtg, dtgt, a_g.reshape(SSD_GROUPS, DTC, 1),
                   a_g.reshape(SSD_GROUPS, 1, DTC), skip, ssd_norm_w[0].reshape(1, inner),
                   nb, seq, ctx_len)
    y_lat = _matmul(yn, [ssd_out_w[0]], [0], d, tn=512, tm=512, out_dtype=BF16, name="ssd_out")

    g2_l, _ = mod_vecs(1, 2)
    sh4_l, _ = mod_vecs(1, 3)
    sc4_l, _ = mod_vecs(1, 4)
    x3_l, v_l = _resnorm(x2_l, y_lat, g2_l, norm2_w[1], sh4_l, sc4_l, n_batch=nb, seq=seq,
                         x_mode="slab", y_mode="row", xo_mode="slab", v_mode="slab", v_dtype=F32)
    moe1 = _moe(v_l, moe_router_w[1], moe_router_bias[1], moe_w_gate[1], moe_w_up[1],
                moe_w_down[1], shared_w_gate[1], shared_w_up[1], shared_w_down[1])
    g5_l, _ = mod_vecs(1, 5)
    _, out = _resnorm(x3_l, moe1, g5_l, final_norm_w, None, None, n_batch=nb, seq=seq, v_dtype=F32,
                      write_x=False)
    return out.reshape(nb, seq, d)
```

```python
import functools
import math

import jax
import jax.numpy as jnp
from jax import lax
from jax.experimental import pallas as pl
from jax.experimental.pallas import tpu as pltpu

F32 = jnp.float32
BF16 = jnp.bfloat16

GRID_W = 64
EPS = 1e-6
S5_GROUP = 16
S5_STATE = 64
SSD_HEADDIM = 64
SSD_STATE = 128
SSD_GROUPS = 8
SSD_CONV = 5
SSD_CHUNK = 128
N_EXPERTS = 64
TOP_K = 8
N_EXPERT_GROUPS = 8
TOPK_GROUPS = 4
ROUTED_SCALE = 2.5
EXPERT_BLOCK = 256

VMEM_LIMIT_BYTES = 56 * 1024 * 1024
LANES = 128
SUBLANES = 8


def _cparams(n_axes):
    return pltpu.CompilerParams(
        dimension_semantics=("arbitrary",) * n_axes,
        vmem_limit_bytes=VMEM_LIMIT_BYTES)


def _silu(v):
    return v * jax.nn.sigmoid(v)


def _dot(a, b):
    return jnp.dot(a, b, preferred_element_type=F32)


def _split3(a):
    hi = a.astype(BF16)
    r1 = a - hi.astype(F32)
    mid = r1.astype(BF16)
    lo = (r1 - mid.astype(F32)).astype(BF16)
    return hi, mid, lo


def _dot_exact_rhs(a, sel):
    hi, mid, lo = _split3(a)
    return _dot(hi, sel) + _dot(mid, sel) + _dot(lo, sel)


def _dot_exact_lhs(sel, a):
    hi, mid, lo = _split3(a)
    return _dot(sel, hi) + _dot(sel, mid) + _dot(sel, lo)


def _ada_kernel(c_ref, w_ref, b_ref, o_ref):
    c = _silu(c_ref[...])
    o_ref[0] = _dot(c.astype(BF16), w_ref[0].astype(BF16)) + b_ref[0]


def _ada(cond, ada_w, ada_b):
    depth, d, n = ada_w.shape
    tn = 1024
    rows = cond.shape[0]
    return pl.pallas_call(
        _ada_kernel,
        out_shape=jax.ShapeDtypeStruct((depth, rows, n), F32),
        grid=(depth, n // tn),
        in_specs=[pl.BlockSpec((rows, d), lambda l, j: (0, 0)),
                  pl.BlockSpec((1, d, tn), lambda l, j: (l, 0, j)),
                  pl.BlockSpec((1, 1, tn), lambda l, j: (l, 0, j))],
        out_specs=pl.BlockSpec((1, rows, tn), lambda l, j: (l, 0, j)),
        compiler_params=_cparams(2),
        name="ada",
    )(cond, ada_w, ada_b.reshape(depth, 1, n))


def _mm_kernel(*refs, n_w, has_bias, epilogue):
    x_ref = refs[0]
    w_refs = refs[1:1 + n_w]
    pos = 1 + n_w
    b_refs = refs[pos:pos + n_w] if has_bias else ()
    pos += n_w if has_bias else 0
    o_ref = refs[pos]
    wbf_refs = refs[pos + 1:pos + 1 + n_w]

    @pl.when(pl.program_id(1) == 0)
    def _():
        for w_ref, wbf in zip(w_refs, wbf_refs):
            wbf[...] = w_ref[...].astype(BF16)

    x = x_ref[...].astype(BF16)
    zs = []
    for k in range(n_w):
        z = _dot(x, wbf_refs[k][...])
        if has_bias:
            z = z + b_refs[k][...]
        zs.append(z)
    if epilogue is None:
        out = zs[0]
    elif epilogue == "softplus":
        out = jax.nn.softplus(zs[0])
    elif epilogue == "glu":
        out = zs[0] * jax.nn.sigmoid(zs[1])
    elif epilogue == "swiglu":
        out = _silu(zs[0]) * zs[1]
    o_ref[...] = out.astype(o_ref.dtype)


def _matmul(x, ws, col_offsets, n_out, *, tn, tm, biases=None, epilogue=None,
            out_dtype=F32, name="matmul"):
    m, k = x.shape
    n_w = len(ws)
    has_bias = biases is not None
    in_specs = [pl.BlockSpec((tm, k), lambda j, i: (i, 0))]
    for off in col_offsets:
        in_specs.append(pl.BlockSpec((k, tn), lambda j, i, off=off: (0, j + off)))
    args = [x] + list(ws)
    if has_bias:
        for off in col_offsets:
            in_specs.append(pl.BlockSpec((1, tn), lambda j, i, off=off: (0, j + off)))
        args += [b.reshape(1, -1) for b in biases]
    return pl.pallas_call(
        functools.partial(_mm_kernel, n_w=n_w, has_bias=has_bias, epilogue=epilogue),
        out_shape=jax.ShapeDtypeStruct((m, n_out), out_dtype),
        grid=(n_out // tn, m // tm),
        in_specs=in_specs,
        out_specs=pl.BlockSpec((tm, tn), lambda j, i: (i, j)),
        scratch_shapes=[pltpu.VMEM((k, tn), BF16) for _ in range(n_w)],
        compiler_params=_cparams(2),
        name=name,
    )(*args)


SLABS = 8
ROW_TILE = 256


def _get_piece(ref, mode, j, rows, d):
    if mode == "slab":
        return ref[:, j * d:(j + 1) * d]
    return ref[j * rows:(j + 1) * rows, :]


def _put_piece(ref, mode, j, rows, d, val):
    if mode == "slab":
        ref[:, j * d:(j + 1) * d] = val.astype(ref.dtype)
    else:
        ref[j * rows:(j + 1) * rows, :] = val.astype(ref.dtype)


def _resnorm_kernel(*refs, has_y, write_x, x_mode, y_mode, xo_mode, v_mode, modulate, rows, d):
    it = iter(refs)
    x_ref = next(it)
    y_ref = next(it) if has_y else None
    g_ref = next(it) if has_y else None
    nw_ref = next(it)
    sh_ref = next(it) if modulate else None
    sc_ref = next(it) if modulate else None
    xo_ref = next(it) if write_x else None
    v_ref = next(it)
    nw = nw_ref[...]
    for j in range(SLABS):
        x = _get_piece(x_ref, x_mode, j, rows, d)
        if has_y:
            y = _get_piece(y_ref, y_mode, j, rows, d).astype(F32)
            x = x + g_ref[0] * y
            if write_x:
                _put_piece(xo_ref, xo_mode, j, rows, d, x)
        v = x * lax.rsqrt(jnp.mean(x * x, axis=-1, keepdims=True) + EPS) * nw
        if modulate:
            v = v * (1.0 + sc_ref[0]) + sh_ref[0]
        _put_piece(v_ref, v_mode, j, rows, d, v)


def _resnorm(x, y, gate, norm_w, shift, scale, *, n_batch, seq, x_mode="row",
             y_mode="row", xo_mode="row", v_mode="row", v_dtype=BF16, write_x=True,
             x_tile0=0, y_tile0=0, vec_index=None, name="resnorm"):
    d = x.shape[1]
    n = n_batch * seq
    if vec_index is None:
        vec_index = lambda b, t: b
    has_y = y is not None
    write_x = write_x and has_y
    modulate = shift is not None
    grows = seq // GRID_W
    tiles_per_seq = seq // ROW_TILE
    slab_used = "slab" in (x_mode, y_mode, xo_mode, v_mode)
    rows = grows if slab_used else ROW_TILE // SLABS
    if slab_used:
        assert grows * SLABS == ROW_TILE

    def spec(mode, tile0=0):
        if mode == "slab":
            return pl.BlockSpec((grows, SLABS * d), lambda b, t: (b, t))
        return pl.BlockSpec((ROW_TILE, d), lambda b, t: (b * tiles_per_seq + t + tile0, 0))

    def view(a, mode):
        return a.reshape(a.shape[0] // GRID_W, GRID_W * d) if mode == "slab" else a

    vec = pl.BlockSpec((1, 1, d), lambda b, t: (vec_index(b, t), 0, 0))
    in_specs = [spec(x_mode, x_tile0)]
    args = [view(x, x_mode)]
    if has_y:
        in_specs += [spec(y_mode, y_tile0), vec]
        args += [view(y, y_mode), gate]
    in_specs.append(pl.BlockSpec((1, d), lambda b, t: (0, 0)))
    args.append(norm_w.reshape(1, d))
    if modulate:
        in_specs += [vec, vec]
        args += [shift, scale]
    out_shape, out_specs = [], []
    def out_struct(mode, dtype):
        shape = (n // GRID_W, GRID_W * d) if mode == "slab" else (n, d)
        return jax.ShapeDtypeStruct(shape, dtype)

    if write_x:
        out_shape.append(out_struct(xo_mode, F32))
        out_specs.append(spec(xo_mode))
    out_shape.append(out_struct(v_mode, v_dtype))
    out_specs.append(spec(v_mode))
    outs = pl.pallas_call(
        functools.partial(_resnorm_kernel, has_y=has_y, write_x=write_x, x_mode=x_mode, y_mode=y_mode,
                          xo_mode=xo_mode, v_mode=v_mode, modulate=modulate, rows=rows, d=d),
        out_shape=out_shape,
        grid=(n_batch, tiles_per_seq),
        in_specs=in_specs,
        out_specs=out_specs,
        compiler_params=_cparams(2),
        name=name,
    )(*args)
    outs = [o.reshape(n, d) for o in outs]
    return (outs[0], outs[1]) if write_x else (None, outs[0])


S5_T = 128
S5_CB = 128
S5_NS = (S5_CB // S5_GROUP) * S5_STATE


def _reverse_rows(v_bf16):
    t = v_bf16.shape[0]
    r = lax.broadcasted_iota(jnp.int32, (t, t), 0)
    c = lax.broadcasted_iota(jnp.int32, (t, t), 1)
    flip = jnp.where(r + c == t - 1, 1.0, 0.0).astype(BF16)
    return _dot(flip, v_bf16)


def _s5_prep_kernel(xf_ref, xb_ref, nw_ref, sh_ref, sc_ref, o_ref, *, nb, d):
    nw = nw_ref[...]
    for b in range(nb):
        for rev, x_ref in ((False, xf_ref), (True, xb_ref)):
            x = x_ref[b]
            v = x * lax.rsqrt(jnp.mean(x * x, axis=-1, keepdims=True) + EPS) * nw
            v = (v * (1.0 + sc_ref[b]) + sh_ref[b]).astype(BF16)
            if rev:
                v = _reverse_rows(v).astype(BF16)
            k = b + (nb if rev else 0)
            o_ref[:, k * d:(k + 1) * d] = v


def _s5_prep(x3, norm_w, shift, scale):
    nb, s, d = x3.shape
    nt = s // S5_T
    return pl.pallas_call(
        functools.partial(_s5_prep_kernel, nb=nb, d=d),
        out_shape=jax.ShapeDtypeStruct((s, 2 * nb * d), BF16),
        grid=(nt,),
        in_specs=[pl.BlockSpec((nb, S5_T, d), lambda t: (0, t, 0)),
                  pl.BlockSpec((nb, S5_T, d), lambda t: (0, nt - 1 - t, 0)),
                  pl.BlockSpec((1, d), lambda t: (0, 0)),
                  pl.BlockSpec((nb, 1, d), lambda t: (0, 0, 0)),
                  pl.BlockSpec((nb, 1, d), lambda t: (0, 0, 0))],
        out_specs=pl.BlockSpec((S5_T, 2 * nb * d), lambda t: (t, 0)),
        compiler_params=_cparams(1),
        name="s5_prep",
    )(x3, x3, norm_w.reshape(1, d), shift, scale)


def _s5_scan_kernel(u_ref, bc_ref, are_ref, aim_ref, cc_ref, s0_ref, y_ref, sf_ref,
                    bu_ref, st_ref, *, n_tiles):
    i = pl.program_id(1)
    ns = S5_NS

    @pl.when(i == 0)
    def _():
        st_ref[...] = s0_ref[0]

    u = u_ref[...].astype(F32)
    row = lax.broadcasted_iota(jnp.int32, u.shape, 0)
    is_fwd = jnp.bitwise_and(row, SUBLANES - 1) < (SUBLANES // 2)
    lhs = jnp.concatenate([jnp.where(is_fwd, u, 0.0), jnp.where(is_fwd, 0.0, u)], axis=1)
    bu_ref[...] = _dot(lhs.astype(BF16), bc_ref[0])
    a_re = are_ref[0]
    a_im = aim_ref[0]

    def step(t, carry):
        s_re, s_im = carry
        r = pl.multiple_of(t * SUBLANES, SUBLANES)
        n_re = a_re * s_re - a_im * s_im + bu_ref[pl.ds(r, SUBLANES), 0:ns]
        n_im = a_re * s_im + a_im * s_re + bu_ref[pl.ds(r, SUBLANES), ns:2 * ns]
        bu_ref[pl.ds(r, SUBLANES), 0:ns] = n_re
        bu_ref[pl.ds(r, SUBLANES), ns:2 * ns] = n_im
        return n_re, n_im

    s_re, s_im = lax.fori_loop(0, S5_T, step, (st_ref[:, 0:ns], st_ref[:, ns:2 * ns]))
    st_ref[:, 0:ns] = s_re
    st_ref[:, ns:2 * ns] = s_im
    y2 = _dot(bu_ref[...].astype(BF16), cc_ref[0])
    y_ref[...] = jnp.where(is_fwd, y2[:, 0:S5_CB], y2[:, S5_CB:2 * S5_CB])

    @pl.when(i == n_tiles - 1)
    def _():
        sf_ref[0] = st_ref[...]


def _s5_scan(u2r, bcat, a_re, a_im, ccat, s0):
    rows, d = u2r.shape
    nblk = d // S5_CB
    tr = S5_T * SUBLANES
    n_tiles = rows // tr
    ns2 = 2 * S5_NS
    return pl.pallas_call(
        functools.partial(_s5_scan_kernel, n_tiles=n_tiles),
        out_shape=(jax.ShapeDtypeStruct((rows, d), F32),
                   jax.ShapeDtypeStruct((nblk, SUBLANES, ns2), F32)),
        grid=(nblk, n_tiles),
        in_specs=[pl.BlockSpec((tr, S5_CB), lambda j, i: (i, j)),
                  pl.BlockSpec((1, 2 * S5_CB, ns2), lambda j, i: (j, 0, 0)),
                  pl.BlockSpec((1, SUBLANES, S5_NS), lambda j, i: (j, 0, 0)),
                  pl.BlockSpec((1, SUBLANES, S5_NS), lambda j, i: (j, 0, 0)),
                  pl.BlockSpec((1, ns2, 2 * S5_CB), lambda j, i: (j, 0, 0)),
                  pl.BlockSpec((1, SUBLANES, ns2), lambda j, i: (j, 0, 0))],
        out_specs=(pl.BlockSpec((tr, S5_CB), lambda j, i: (i, j)),
                   pl.BlockSpec((1, SUBLANES, ns2), lambda j, i: (j, 0, 0))),
        scratch_shapes=[pltpu.VMEM((tr, ns2), F32), pltpu.VMEM((SUBLANES, ns2), F32)],
        compiler_params=_cparams(2),
        name="s5_scan",
    )(u2r, bcat, a_re, a_im, ccat, s0)


def _s5_out_kernel(yf_ref, yb_ref, u_ref, skip_ref, o_ref):
    yb = yb_ref[...]
    hi = yb.astype(BF16)
    lo = (yb - hi.astype(F32)).astype(BF16)
    y = (skip_ref[...] * u_ref[...].astype(F32) + yf_ref[...]
         + _reverse_rows(hi) + _reverse_rows(lo))
    o_ref[...] = jax.nn.gelu(y).astype(o_ref.dtype)


def _s5_out(y2, u2, skip, nb, d):
    s = y2.shape[0]
    nt = s // S5_T
    return pl.pallas_call(
        _s5_out_kernel,
        out_shape=jax.ShapeDtypeStruct((nb * s, d), BF16),
        grid=(nb, nt),
        in_specs=[pl.BlockSpec((S5_T, d), lambda b, t: (t, b)),
                  pl.BlockSpec((S5_T, d), lambda b, t: (nt - 1 - t, nb + b)),
                  pl.BlockSpec((S5_T, d), lambda b, t: (t, b)),
                  pl.BlockSpec((1, d), lambda b, t: (0, 0))],
        out_specs=pl.BlockSpec((S5_T, d), lambda b, t: (b * nt + t, 0)),
        compiler_params=_cparams(2),
        name="s5_out",
    )(y2, y2, u2, skip.reshape(1, d))


def _s5_pack_params(lam_re, lam_im, log_step, b_re, b_im, c_re, c_im, nb):
    f32 = F32
    g = lam_re.shape[1]
    gpb = S5_CB // S5_GROUP
    nblk = g // gpb
    eye = jnp.eye(gpb, dtype=f32)
    a_re_rows, a_im_rows, b_parts, c_parts = [], [], [], []
    for dr in range(2):
        lr, li = lam_re[dr].astype(f32), lam_im[dr].astype(f32)
        br, bi = b_re[dr].astype(f32), b_im[dr].astype(f32)
        dt = jnp.exp(log_step[dr].astype(f32))[:, None]
        mag = jnp.exp(lr * dt)
        abar_re = mag * jnp.cos(li * dt)
        abar_im = mag * jnp.sin(li * dt)
        num_re = abar_re - 1.0
        num_im = abar_im
        den = lr * lr + li * li
        f_re = (num_re * lr + num_im * li) / den
        f_im = (num_im * lr - num_re * li) / den
        bbar_re = f_re[..., None] * br - f_im[..., None] * bi
        bbar_im = f_re[..., None] * bi + f_im[..., None] * br
        a_re_rows.append(jnp.broadcast_to(abar_re.reshape(nblk, 1, S5_NS), (nblk, nb, S5_NS)))
        a_im_rows.append(jnp.broadcast_to(abar_im.reshape(nblk, 1, S5_NS), (nblk, nb, S5_NS)))

        def blockdiag_in(bb):
            b4 = bb.reshape(nblk, gpb, S5_STATE, S5_GROUP)
            return jnp.einsum('jgpk,gh->jgkhp', b4, eye).reshape(nblk, S5_CB, S5_NS)

        def blockdiag_out(cc):
            c4 = cc.astype(f32).reshape(nblk, gpb, S5_GROUP, S5_STATE)
            return jnp.einsum('jgkp,gh->jgphk', c4, eye).reshape(nblk, S5_NS, S5_CB)

        b_parts.append(jnp.concatenate([blockdiag_in(bbar_re), blockdiag_in(bbar_im)], axis=2))
        c_parts.append(jnp.concatenate([blockdiag_out(c_re[dr]), -blockdiag_out(c_im[dr])], axis=1))
    a_re = jnp.concatenate(a_re_rows, axis=1)
    a_im = jnp.concatenate(a_im_rows, axis=1)
    bcat = jnp.concatenate(b_parts, axis=1).astype(BF16)
    ccat = jnp.concatenate(c_parts, axis=2).astype(BF16)
    return bcat, a_re, a_im, ccat


def _conv_kernel(x_ref, w_ref, b_ref, o_ref, pad_ref, *, seq):
    halo = SUBLANES
    zeros = jnp.zeros((halo, pad_ref.shape[1]), F32)
    pad_ref[0:halo, :] = zeros
    pad_ref[halo + seq:2 * halo + seq, :] = zeros
    pad_ref[halo:halo + seq, :] = x_ref[...]
    acc = jnp.zeros((seq, pad_ref.shape[1]), F32) + b_ref[...]
    for k in range(SSD_CONV):
        off = halo + k - SSD_CONV // 2
        acc = acc + w_ref[k:k + 1, :] * pad_ref[off:off + seq, :]
    o_ref[...] = _silu(acc).astype(o_ref.dtype)


def _conv_silu(xbc, conv_w, conv_b, seq, row_block_offset, n_seq, out_dtype=BF16):
    _, c = xbc.shape
    tc = 512
    return pl.pallas_call(
        functools.partial(_conv_kernel, seq=seq),
        out_shape=jax.ShapeDtypeStruct((n_seq * seq, c), out_dtype),
        grid=(n_seq, c // tc),
        in_specs=[pl.BlockSpec((seq, tc), lambda b, j: (b + row_block_offset, j)),
                  pl.BlockSpec((SSD_CONV, tc), lambda b, j: (0, j)),
                  pl.BlockSpec((1, tc), lambda b, j: (0, j))],
        out_specs=pl.BlockSpec((seq, tc), lambda b, j: (b, j)),
        scratch_shapes=[pltpu.VMEM((seq + 2 * SUBLANES, tc), F32)],
        compiler_params=_cparams(2),
        name="ssd_conv",
    )(xbc, conv_w, conv_b.reshape(1, c))


HPG = 8
GCH = HPG * SSD_HEADDIM
DTC = 2 * HPG


def _ssd_kernel(xl_ref, btl_ref, cl_ref, dtl_ref, dttl_ref,
                xc_ref, btc_ref, dtc_ref, dttc_ref,
                z_ref, acol_ref, arow_ref, skip_ref, nw_ref, o_ref,
                htf_ref, htb_ref, yf_ref, yb_ref, *, n_lat, n_ctx):
    q = SSD_CHUNK
    r_i = lax.broadcasted_iota(jnp.int32, (q, q), 0)
    c_i = lax.broadcasted_iota(jnp.int32, (q, q), 1)
    tril = jnp.where(c_i <= r_i, 1.0, 0.0).astype(BF16)
    triu = jnp.where(r_i <= c_i, 1.0, 0.0).astype(BF16)
    lower = c_i <= r_i
    upper = c_i >= r_i
    lane = lax.broadcasted_iota(jnp.int32, (q, 2 * SSD_HEADDIM), 1)
    left = lane < SSD_HEADDIM
    e_r = lax.broadcasted_iota(jnp.int32, (DTC, GCH), 0)
    e_c = lax.broadcasted_iota(jnp.int32, (DTC, GCH), 1)
    s_r = lax.broadcasted_iota(jnp.int32, (DTC, HPG * q), 0)
    s_c = lax.broadcasted_iota(jnp.int32, (DTC, HPG * q), 1)
    a_row = arow_ref[0]
    a_col = acol_ref[0]

    def chunk(x_ref, bt_ref, c_ref, dt_ref, dtt_ref, r0, dr, ht_ref, y_ref):
        head_of_ch = lax.shift_right_logical(e_c, int(math.log2(SSD_HEADDIM)))
        head_of_col = lax.shift_right_logical(s_c, int(math.log2(q)))
        expand = jnp.where(head_of_ch + dr * HPG == e_r, 1.0, 0.0).astype(BF16)
        bcast = jnp.where(head_of_col + dr * HPG == s_r, 1.0, 0.0).astype(BF16)
        x = x_ref[pl.ds(r0, q), :].astype(F32)
        bt = bt_ref[0, :, pl.ds(r0, q)]
        dt = dt_ref[0, pl.ds(r0, q), :]
        dtt = dtt_ref[0, :, pl.ds(r0, q)]
        la = dt * a_row
        lat = dtt * a_col
        cs = _dot_exact_lhs(tril, la)
        cst = _dot_exact_rhs(lat, triu)
        csx = _dot_exact_rhs(cs, expand)
        totx = csx[q - 1:q, :]
        dtx = _dot_exact_rhs(dt, expand)
        if dr == 1:
            rk, rkt = cs - la, cst - lat
            rkx = csx - _dot_exact_rhs(la, expand)
        else:
            rk, rkt, rkx = cs, cst, csx
        xd = x * dtx
        if dr == 0:
            w_state = jnp.exp(totx - rkx)
            w_off = jnp.exp(rkx)
        else:
            w_state = jnp.exp(rkx)
            w_off = jnp.exp(totx - rkx)
        h_old = ht_ref[...]
        ht_ref[...] = h_old * jnp.exp(totx) + _dot(bt, (xd * w_state).astype(BF16))
        if y_ref is None:
            return
        cm = c_ref[pl.ds(r0, q), :]
        cb = _dot(cm, bt)
        rkb = _dot_exact_rhs(rk, bcast)
        xdb = xd.astype(BF16)
        pieces = []
        for pair in range(HPG // 2):
            ms = []
            for hh in (2 * pair, 2 * pair + 1):
                col = dr * HPG + hh
                colv = rkb[:, hh * q:(hh + 1) * q]
                rowv = rkt[col:col + 1, :]
                if dr == 0:
                    seg = jnp.where(lower, colv - rowv, -1e30)
                else:
                    seg = jnp.where(upper, rowv - colv, -1e30)
                ms.append((cb * jnp.exp(seg)).astype(BF16))
            xp = xdb[:, pair * 2 * SSD_HEADDIM:(pair + 1) * 2 * SSD_HEADDIM]
            zero = jnp.zeros_like(xp)
            rhs = jnp.concatenate([jnp.where(left, xp, zero), jnp.where(left, zero, xp)], axis=0)
            pieces.append(_dot(jnp.concatenate(ms, axis=1), rhs))
        y_ref[pl.ds(r0, q), :] = (jnp.concatenate(pieces, axis=1)
                                  + _dot(cm, h_old.astype(BF16)) * w_off)

    htf_ref[...] = jnp.zeros_like(htf_ref)
    htb_ref[...] = jnp.zeros_like(htb_ref)

    def ctx_body(k, carry):
        rf = pl.multiple_of(k * q, q)
        rb = pl.multiple_of((n_ctx - 1 - k) * q, q)
        chunk(xc_ref, btc_ref, None, dtc_ref, dttc_ref, rf, 0, htf_ref, None)
        chunk(xc_ref, btc_ref, None, dtc_ref, dttc_ref, rb, 1, htb_ref, None)
        return carry

    lax.fori_loop(0, n_ctx, ctx_body, 0)

    def lat_body(k, carry):
        rf = pl.multiple_of(k * q, q)
        rb = pl.multiple_of((n_lat - 1 - k) * q, q)
        chunk(xl_ref, btl_ref, cl_ref, dtl_ref, dttl_ref, rf, 0, htf_ref, yf_ref)
        chunk(xl_ref, btl_ref, cl_ref, dtl_ref, dttl_ref, rb, 1, htb_ref, yb_ref)
        return carry

    lax.fori_loop(0, n_lat, lat_body, 0)

    def finish(k, carry):
        r0 = pl.multiple_of(k * q, q)
        x = xl_ref[pl.ds(r0, q), :].astype(F32)
        y = yf_ref[pl.ds(r0, q), :] + yb_ref[pl.ds(r0, q), :] + skip_ref[...] * x
        v = y * _silu(z_ref[pl.ds(r0, q), :].astype(F32))
        v = v * lax.rsqrt(jnp.mean(v * v, axis=-1, keepdims=True) + EPS) * nw_ref[...]
        o_ref[pl.ds(r0, q), :] = v.astype(o_ref.dtype)
        return carry

    lax.fori_loop(0, n_lat, finish, 0)


def _ssd_scan(xc_all, bt_all, z, dtg, dtgt, a_col, a_row, skip, norm_w, nb, seq, ctx_len):
    n_lat, n_ctx = seq // SSD_CHUNK, ctx_len // SSD_CHUNK
    inner = SSD_GROUPS * GCH
    xoff = 0
    boff = inner // SSD_STATE
    coff = boff + SSD_GROUPS
    cb0 = nb * seq // ctx_len
    in_specs = [
        pl.BlockSpec((seq, GCH), lambda b, g: (b, g)),
        pl.BlockSpec((1, SSD_STATE, seq), lambda b, g: (g, 0, b)),
        pl.BlockSpec((seq, SSD_STATE), lambda b, g: (b, coff + g)),
        pl.BlockSpec((1, seq, DTC), lambda b, g: (g, b, 0)),
        pl.BlockSpec((1, DTC, seq), lambda b, g: (g, 0, b)),
        pl.BlockSpec((ctx_len, GCH), lambda b, g: (cb0 + b, g)),
        pl.BlockSpec((1, SSD_STATE, ctx_len), lambda b, g: (g, 0, cb0 + b)),
        pl.BlockSpec((1, ctx_len, DTC), lambda b, g: (g, cb0 + b, 0)),
        pl.BlockSpec((1, DTC, ctx_len), lambda b, g: (g, 0, cb0 + b)),
        pl.BlockSpec((seq, GCH), lambda b, g: (b, g)),
        pl.BlockSpec((1, DTC, 1), lambda b, g: (g, 0, 0)),
        pl.BlockSpec((1, 1, DTC), lambda b, g: (g, 0, 0)),
        pl.BlockSpec((1, GCH), lambda b, g: (0, g)),
        pl.BlockSpec((1, GCH), lambda b, g: (0, g)),
    ]
    del xoff
    return pl.pallas_call(
        functools.partial(_ssd_kernel, n_lat=n_lat, n_ctx=n_ctx),
        out_shape=jax.ShapeDtypeStruct((nb * seq, inner), BF16),
        grid=(nb, SSD_GROUPS),
        in_specs=in_specs,
        out_specs=pl.BlockSpec((seq, GCH), lambda b, g: (b, g)),
        scratch_shapes=[pltpu.VMEM((SSD_STATE, GCH), F32), pltpu.VMEM((SSD_STATE, GCH), F32),
                        pltpu.VMEM((seq, GCH), F32), pltpu.VMEM((seq, GCH), F32)],
        compiler_params=_cparams(2),
        name="ssd_scan",
    )(xc_all, bt_all, xc_all, dtg, dtgt, xc_all, bt_all, dtg, dtgt,
      z, a_col, a_row, skip, norm_w)


def _expert_kernel(be_ref, nu_ref, x_ref, wg_ref, wu_ref, wd_ref, o_ref, wgb, wub, wdb):
    i = pl.program_id(0)
    prev = be_ref[jnp.maximum(i - 1, 0)]
    changed = jnp.logical_or(i == 0, be_ref[i] != prev)

    @pl.when(changed)
    def _():
        wgb[...] = wg_ref[0, 0].astype(BF16)
        wub[...] = wu_ref[0, 0].astype(BF16)
        wdb[...] = wd_ref[0, 0].astype(BF16)

    @pl.when(i < nu_ref[0])
    def _():
        x = x_ref[...].astype(BF16)
        h = _silu(_dot(x, wgb[...])) * _dot(x, wub[...])
        o_ref[...] = _dot(h.astype(BF16), wdb[...]).astype(o_ref.dtype)

    @pl.when(i >= nu_ref[0])
    def _():
        o_ref[...] = jnp.zeros_like(o_ref)


def _experts(x_sorted, block_e, n_used, w_gate, w_up, w_down, layer):
    cap, d = x_sorted.shape
    ff = w_gate.shape[3]
    n_blocks = cap // EXPERT_BLOCK

    def row_block(i, be, nu):
        return (jnp.minimum(i, nu[0] - 1), 0)

    grid_spec = pltpu.PrefetchScalarGridSpec(
        num_scalar_prefetch=2,
        grid=(n_blocks,),
        in_specs=[pl.BlockSpec((EXPERT_BLOCK, d), row_block),
                  pl.BlockSpec((1, 1, d, ff), lambda i, be, nu: (layer, be[i], 0, 0)),
                  pl.BlockSpec((1, 1, d, ff), lambda i, be, nu: (layer, be[i], 0, 0)),
                  pl.BlockSpec((1, 1, ff, d), lambda i, be, nu: (layer, be[i], 0, 0))],
        out_specs=pl.BlockSpec((EXPERT_BLOCK, d), lambda i, be, nu: (i, 0)),
        scratch_shapes=[pltpu.VMEM((d, ff), BF16), pltpu.VMEM((d, ff), BF16),
                        pltpu.VMEM((ff, d), BF16)])
    return pl.pallas_call(
        _expert_kernel,
        out_shape=jax.ShapeDtypeStruct((cap, d), F32),
        grid_spec=grid_spec,
        compiler_params=_cparams(1),
        name="moe_experts",
    )(block_e, n_used, x_sorted, w_gate, w_up, w_down)


DISPATCH_T = 256
COMBINE_T = 128


def _dispatch_kernel(pe_ref, pd_ref, dest_ref, v_ref, xs_ref, zero_buf, sem):
    t_rows = v_ref.shape[0]

    @pl.when(pl.program_id(0) == 0)
    def _():
        zero_buf[...] = jnp.zeros_like(zero_buf)

        def fill(e, carry):
            @pl.when(pd_ref[e] > 0)
            def _():
                start = pl.multiple_of(pe_ref[e] - EXPERT_BLOCK, EXPERT_BLOCK)
                pltpu.make_async_copy(zero_buf, xs_ref.at[pl.ds(start, EXPERT_BLOCK)], sem).start()
            return carry

        def drain(e, carry):
            @pl.when(pd_ref[e] > 0)
            def _():
                pltpu.make_async_copy(zero_buf, xs_ref.at[pl.ds(0, EXPERT_BLOCK)], sem).wait()
            return carry

        lax.fori_loop(0, N_EXPERTS, fill, 0)
        lax.fori_loop(0, N_EXPERTS, drain, 0)

        n_blocks = xs_ref.shape[0] // EXPERT_BLOCK
        first_unused = pe_ref[N_EXPERTS - 1] // EXPERT_BLOCK

        def fill_tail(b, carry):
            start = pl.multiple_of(b * EXPERT_BLOCK, EXPERT_BLOCK)
            pltpu.make_async_copy(zero_buf, xs_ref.at[pl.ds(start, EXPERT_BLOCK)], sem).start()
            return carry

        def drain_tail(b, carry):
            pltpu.make_async_copy(zero_buf, xs_ref.at[pl.ds(0, EXPERT_BLOCK)], sem).wait()
            return carry

        lax.fori_loop(first_unused, n_blocks, fill_tail, 0)
        lax.fori_loop(first_unused, n_blocks, drain_tail, 0)

    def issue(t, carry):
        for k in range(TOP_K):
            pltpu.make_async_copy(v_ref.at[pl.ds(t, 1)],
                                  xs_ref.at[pl.ds(dest_ref[k, t], 1)], sem).start()
        return carry

    lax.fori_loop(0, t_rows, issue, 0)
    for k in range(TOP_K):
        pltpu.make_async_copy(v_ref, xs_ref.at[pl.ds(0, t_rows)], sem).wait()


def _dispatch(v, dest, pad_end, padded, cap):
    n, d = v.shape
    grid_spec = pltpu.PrefetchScalarGridSpec(
        num_scalar_prefetch=2,
        grid=(n // DISPATCH_T,),
        in_specs=[pl.BlockSpec((TOP_K, DISPATCH_T), lambda i, pe, pd: (0, i),
                               memory_space=pltpu.SMEM),
                  pl.BlockSpec((DISPATCH_T, d), lambda i, pe, pd: (i, 0))],
        out_specs=pl.BlockSpec(memory_space=pl.ANY),
        scratch_shapes=[pltpu.VMEM((EXPERT_BLOCK, d), F32), pltpu.SemaphoreType.DMA(())])
    return pl.pallas_call(
        _dispatch_kernel,
        out_shape=jax.ShapeDtypeStruct((cap, d), F32),
        grid_spec=grid_spec,
        compiler_params=_cparams(1),
        name="moe_dispatch",
    )(pad_end, padded, dest, v)


def _combine_kernel(dest_ref, y_ref, gate_ref, sh_ref, o_ref, buf, sem):
    t_rows = o_ref.shape[0]

    def issue(t, carry):
        for k in range(TOP_K):
            pltpu.make_async_copy(y_ref.at[pl.ds(dest_ref[k, t], 1)],
                                  buf.at[k, pl.ds(t, 1)], sem).start()
        return carry

    lax.fori_loop(0, t_rows, issue, 0)
    for k in range(TOP_K):
        pltpu.make_async_copy(y_ref.at[pl.ds(0, t_rows)], buf.at[k], sem).wait()
    acc = sh_ref[...]
    for k in range(TOP_K):
        acc = acc + gate_ref[:, k:k + 1] * buf[k]
    o_ref[...] = acc


def _combine(y_sorted, dest, gate_t, shared):
    n, d = shared.shape
    return pl.pallas_call(
        _combine_kernel,
        out_shape=jax.ShapeDtypeStruct((n, d), F32),
        grid=(n // COMBINE_T,),
        in_specs=[pl.BlockSpec((TOP_K, COMBINE_T), lambda i: (0, i), memory_space=pltpu.SMEM),
                  pl.BlockSpec(memory_space=pl.ANY),
                  pl.BlockSpec((COMBINE_T, TOP_K), lambda i: (i, 0)),
                  pl.BlockSpec((COMBINE_T, d), lambda i: (i, 0))],
        out_specs=pl.BlockSpec((COMBINE_T, d), lambda i: (i, 0)),
        scratch_shapes=[pltpu.VMEM((TOP_K, COMBINE_T, d), F32), pltpu.SemaphoreType.DMA(())],
        compiler_params=_cparams(1),
        name="moe_combine",
    )(dest, y_sorted, gate_t, shared)


ROUTE_T = 512
GROUP_SIZE = N_EXPERTS // N_EXPERT_GROUPS


def _route_kernel(v_ref, rwh_ref, rwl_ref, bias_ref, eid_ref, rank_ref, gate_ref, cnt_ref,
                  carry_ref, *, n_tiles):
    i = pl.program_id(0)
    t = v_ref.shape[0]
    ng, gs = N_EXPERT_GROUPS, GROUP_SIZE
    neg = -jnp.inf

    @pl.when(i == 0)
    def _():
        carry_ref[...] = jnp.zeros_like(carry_ref)

    v = v_ref[...].astype(BF16)
    nt_dims = (((1,), (1,)), ((), ()))
    logits = (lax.dot_general(rwh_ref[...], v, nt_dims, preferred_element_type=F32)
              + lax.dot_general(rwl_ref[...], v, nt_dims, preferred_element_type=F32))
    scores = jax.nn.sigmoid(logits)
    biased = scores + bias_ref[:, 0:1]
    x3 = biased.reshape(ng, gs, t)
    s3 = scores.reshape(ng, gs, t)
    mi = lax.broadcasted_iota(jnp.int32, (ng, gs, t), 1).astype(F32)
    fi = lax.broadcasted_iota(jnp.int32, (ng, gs, t), 0).astype(F32) * gs + mi
    gi = lax.broadcasted_iota(jnp.int32, (ng, 1, t), 0).astype(F32)

    m1 = jnp.max(x3, axis=1, keepdims=True)
    i1 = jnp.min(jnp.where(x3 == m1, mi, float(gs)), axis=1, keepdims=True)
    m2 = jnp.max(jnp.where(mi == i1, neg, x3), axis=1, keepdims=True)
    cur = m1 + m2
    gmask = jnp.zeros((ng, 1, t), F32)
    for _ in range(TOPK_GROUPS):
        gm = jnp.max(cur, axis=0, keepdims=True)
        idx = jnp.min(jnp.where(cur == gm, gi, float(ng)), axis=0, keepdims=True)
        hit = gi == idx
        gmask = jnp.where(hit, 1.0, gmask)
        cur = jnp.where(hit, neg, cur)

    cand = jnp.where(gmask > 0.0, x3, neg)
    sel = jnp.zeros((ng, gs, t), F32)
    eids = []
    for _ in range(TOP_K):
        m = jnp.max(jnp.max(cand, axis=1, keepdims=True), axis=0, keepdims=True)
        idx = jnp.where(cand == m, fi, float(N_EXPERTS))
        idx = jnp.min(jnp.min(idx, axis=1, keepdims=True), axis=0, keepdims=True)
        hit = fi == idx
        sel = jnp.where(hit, 1.0, sel)
        cand = jnp.where(hit, neg, cand)
        eids.append(idx)

    selr = sel.reshape(N_EXPERTS, t)
    r_i = lax.broadcasted_iota(jnp.int32, (t, t), 0)
    c_i = lax.broadcasted_iota(jnp.int32, (t, t), 1)
    before = jnp.where(r_i < c_i, 1.0, 0.0).astype(BF16)
    rank = _dot(selr.astype(BF16), before) + carry_ref[:, 0:1]
    carry_ref[...] = carry_ref[...] + jnp.sum(selr, axis=1, keepdims=True)
    rank3 = rank.reshape(ng, gs, t)

    gsel = sel * s3
    denom = jnp.sum(jnp.sum(gsel, axis=1, keepdims=True), axis=0, keepdims=True)
    gate3 = gsel / denom * ROUTED_SCALE

    def pick(a3, hit):
        return jnp.sum(jnp.sum(jnp.where(hit, a3, 0.0), axis=1, keepdims=True), axis=0,
                       keepdims=True).reshape(1, t)

    for k in range(TOP_K):
        hit = fi == eids[k]
        eid_ref[k:k + 1, :] = eids[k].reshape(1, t).astype(jnp.int32)
        rank_ref[k:k + 1, :] = pick(rank3, hit).astype(jnp.int32)
        gate_ref[k:k + 1, :] = pick(gate3, hit)

    @pl.when(i == n_tiles - 1)
    def _():
        cnt_ref[...] = carry_ref[...]


def _route(v, router_w, router_bias):
    n, d = v.shape
    n_tiles = n // ROUTE_T
    rwt = router_w.astype(F32).T
    rwh = rwt.astype(BF16)
    rwl = (rwt - rwh.astype(F32)).astype(BF16)
    bias = jnp.broadcast_to(router_bias.astype(F32)[:, None], (N_EXPERTS, LANES))
    slot = pl.BlockSpec((TOP_K, ROUTE_T), lambda i: (0, i))
    full = pl.BlockSpec((N_EXPERTS, d), lambda i: (0, 0))
    return pl.pallas_call(
        functools.partial(_route_kernel, n_tiles=n_tiles),
        out_shape=(jax.ShapeDtypeStruct((TOP_K, n), jnp.int32),
                   jax.ShapeDtypeStruct((TOP_K, n), jnp.int32),
                   jax.ShapeDtypeStruct((TOP_K, n), F32),
                   jax.ShapeDtypeStruct((N_EXPERTS, LANES), F32)),
        grid=(n_tiles,),
        in_specs=[pl.BlockSpec((ROUTE_T, d), lambda i: (i, 0)), full, full,
                  pl.BlockSpec((N_EXPERTS, LANES), lambda i: (0, 0))],
        out_specs=(slot, slot, slot, pl.BlockSpec((N_EXPERTS, LANES), lambda i: (0, 0))),
        scratch_shapes=[pltpu.VMEM((N_EXPERTS, LANES), F32)],
        compiler_params=_cparams(1),
        name="moe_route",
    )(v, rwh, rwl, bias)


def _slot_rows_kernel(ps_ref, eid_ref, rank_ref, o_ref):
    eid = eid_ref[...]
    acc = rank_ref[...]
    for e in range(N_EXPERTS):
        acc = acc + jnp.where(eid == e, ps_ref[e], 0)
    o_ref[...] = acc


def _slot_rows(eid, rank, pad_start):
    k, n = eid.shape
    whole = lambda i, ps: (0, 0)
    grid_spec = pltpu.PrefetchScalarGridSpec(
        num_scalar_prefetch=1, grid=(1,),
        in_specs=[pl.BlockSpec((k, n), whole), pl.BlockSpec((k, n), whole)],
        out_specs=pl.BlockSpec((k, n), whole))
    return pl.pallas_call(
        _slot_rows_kernel,
        out_shape=jax.ShapeDtypeStruct((k, n), jnp.int32),
        grid_spec=grid_spec,
        compiler_params=_cparams(1),
        name="moe_slot_rows",
    )(pad_start, eid, rank)


def _moe(v, layer, router_w, router_bias, w_gate, w_up, w_down, sw_gate, sw_up, sw_down):
    n, d = v.shape
    eid, rank, gate, cnt = _route(v, router_w, router_bias)
    nk = n * TOP_K
    cap = -(-nk // EXPERT_BLOCK) * EXPERT_BLOCK + N_EXPERTS * EXPERT_BLOCK
    n_blocks = cap // EXPERT_BLOCK
    counts = cnt[:, 0].astype(jnp.int32)
    padded = (counts + EXPERT_BLOCK - 1) // EXPERT_BLOCK * EXPERT_BLOCK
    pad_end = jnp.cumsum(padded)
    pad_start = pad_end - padded
    dest = _slot_rows(eid, rank, pad_start.astype(jnp.int32))
    block_start = jnp.arange(n_blocks, dtype=jnp.int32) * EXPERT_BLOCK
    block_e = jnp.minimum(jnp.sum((pad_end[None, :] <= block_start[:, None]).astype(jnp.int32), axis=1),
                          N_EXPERTS - 1).astype(jnp.int32)
    n_used = (pad_end[-1] // EXPERT_BLOCK).astype(jnp.int32).reshape(1)
    x_sorted = _dispatch(v, dest, pad_end.astype(jnp.int32), padded.astype(jnp.int32), cap)
    y_sorted = _experts(x_sorted, block_e, n_used, w_gate, w_up, w_down, layer)
    hs = _matmul(v, [sw_gate, sw_up], [0, 0], sw_gate.shape[1], tn=sw_gate.shape[1], tm=512,
                 epilogue="swiglu", out_dtype=BF16, name="shared_up")
    shared = _matmul(hs, [sw_down], [0], d, tn=1024, tm=512, name="shared_down")
    return _combine(y_sorted, dest, gate.T, shared)


def kernel(x, c, ctx, c_ctx, ada_w, ada_b, norm1_w, norm2_w, s5_lambda_re, s5_lambda_im, s5_log_step, s5_b_re, s5_b_im, s5_c_re, s5_c_im, s5_d, s5_glu_w, s5_glu_b, ssd_in_w, ssd_conv_w, ssd_conv_b, ssd_dt_bias, ssd_a_log, ssd_d, ssd_norm_w, ssd_out_w, moe_router_w, moe_router_bias, moe_w_gate, moe_w_up, moe_w_down, shared_w_gate, shared_w_up, shared_w_down, final_norm_w):
    nb, seq, d = x.shape
    ctx_len = ctx.shape[1]
    n_lat = nb * seq
    n_ctx = nb * ctx_len

    cond = jnp.concatenate([c, c_ctx[None, :], jnp.zeros((SUBLANES - nb - 1, d), F32)], axis=0)
    mods = _ada(cond, ada_w, ada_b)

    def mod_vecs(layer, k):
        m = mods[layer, :, k * d:(k + 1) * d]
        lat = m[:nb].reshape(nb, 1, d)
        cx = jnp.broadcast_to(m[nb].reshape(1, 1, d), (nb, 1, d))
        return lat, cx

    x_lat = x.reshape(n_lat, d)
    x_ctx = ctx.reshape(n_ctx, d)

    sh_l, sh_c = mod_vecs(0, 0)
    sc_l, sc_c = mod_vecs(0, 1)
    u2_c = _s5_prep(ctx, norm1_w[0], sh_c, sc_c)
    u2_l = _s5_prep(x, norm1_w[0], sh_l, sc_l)
    bcat, a_re, a_im, ccat = _s5_pack_params(
        s5_lambda_re[0], s5_lambda_im[0], s5_log_step[0], s5_b_re[0], s5_b_im[0],
        s5_c_re[0], s5_c_im[0], nb)
    nblk = d // S5_CB
    s0 = jnp.zeros((nblk, SUBLANES, 2 * S5_NS), F32)
    y2_c, s_ctx = _s5_scan(u2_c.reshape(ctx_len * 2 * nb, d), bcat, a_re, a_im, ccat, s0)
    y2_l, _ = _s5_scan(u2_l.reshape(seq * 2 * nb, d), bcat, a_re, a_im, ccat, s_ctx)
    g_l = _s5_out(y2_l.reshape(seq, 2 * nb * d), u2_l, s5_d[0], nb, d)
    g_c = _s5_out(y2_c.reshape(ctx_len, 2 * nb * d), u2_c, s5_d[0], nb, d)
    g_all = jnp.concatenate([g_l, g_c], axis=0)
    half = s5_glu_w.shape[2] // 2
    tn = 1024
    glu = _matmul(g_all, [s5_glu_w[0], s5_glu_w[0]], [0, half // tn], half, tn=tn, tm=512,
                  biases=[s5_glu_b[0], s5_glu_b[0]], epilogue="glu", out_dtype=BF16, name="s5_glu")

    n_all = n_lat + n_ctx
    lat_tiles = seq // ROW_TILE

    def all_vecs(layer, k):
        return mods[layer, :nb + 1, k * d:(k + 1) * d].reshape(nb + 1, 1, d)

    def all_index(b, t):
        return jnp.minimum(t // lat_tiles, nb)

    xs0 = jnp.concatenate([x_lat, x_ctx], axis=0)
    xs1, v_all = _resnorm(xs0, glu, all_vecs(0, 2), norm2_w[0], all_vecs(0, 3), all_vecs(0, 4),
                          n_batch=1, seq=n_all, vec_index=all_index, v_dtype=F32)
    moe0 = _moe(v_all, 0, moe_router_w[0], moe_router_bias[0], moe_w_gate, moe_w_up,
                moe_w_down, shared_w_gate[0], shared_w_up[0], shared_w_down[0])

    g5_l, g5_c = mod_vecs(0, 5)
    sh_l, sh_c = mod_vecs(1, 0)
    sc_l, sc_c = mod_vecs(1, 1)
    x2_l, u_l = _resnorm(xs1, moe0, g5_l, norm1_w[1], sh_l, sc_l, n_batch=nb, seq=seq,
                         x_mode="slab", y_mode="slab", xo_mode="slab", v_mode="row")
    ctx_tile0 = n_lat // ROW_TILE
    _, u_c = _resnorm(xs1, moe0, g5_c, norm1_w[1], sh_c, sc_c, n_batch=nb, seq=ctx_len,
                      write_x=False, x_tile0=ctx_tile0, y_tile0=ctx_tile0)
    u_all = jnp.concatenate([u_l, u_c], axis=0)
    in_w = ssd_in_w[0]
    inner = ssd_out_w.shape[1]
    conv_dim = ssd_conv_w.shape[2]
    heads = inner // SSD_HEADDIM
    tn = 1024
    z_all = _matmul(u_all, [in_w], [0], inner, tn=tn, tm=512, out_dtype=BF16, name="ssd_in_z")
    xbc = _matmul(u_all, [in_w], [inner // tn], conv_dim, tn=tn, tm=512, name="ssd_in_xbc")
    dt_bias = ssd_dt_bias[0].reshape(-1)
    dt_all = _matmul(u_all, [in_w], [(inner + conv_dim) // LANES], 2 * heads, tn=LANES, tm=512,
                     biases=[jnp.pad(dt_bias, (inner + conv_dim, 0))], epilogue="softplus",
                     name="ssd_in_dt")
    xc_l = _conv_silu(xbc, ssd_conv_w[0], ssd_conv_b[0], seq, 0, nb)
    xc_c = _conv_silu(xbc, ssd_conv_w[0], ssd_conv_b[0], ctx_len, n_lat // ctx_len, nb)
    xc_all = jnp.concatenate([xc_l, xc_c], axis=0)
    rows = n_lat + n_ctx
    bt_all = xc_all[:, inner:inner + SSD_GROUPS * SSD_STATE].reshape(rows, SSD_GROUPS, SSD_STATE)
    bt_all = bt_all.transpose(1, 2, 0)
    dtg = dt_all.reshape(rows, 2, SSD_GROUPS, HPG).transpose(2, 0, 1, 3).reshape(SSD_GROUPS, rows, DTC)
    dtgt = dtg.transpose(0, 2, 1)
    a = -jnp.exp(ssd_a_log[0].astype(F32))
    a_g = a.reshape(2, SSD_GROUPS, HPG).transpose(1, 0, 2).reshape(SSD_GROUPS, DTC)
    skip = jnp.repeat(ssd_d[0].astype(F32), SSD_HEADDIM).reshape(1, inner)
    yn = _ssd_scan(xc_all, bt_all, z_all, dtg, dtgt, a_g.reshape(SSD_GROUPS, DTC, 1),
                   a_g.reshape(SSD_GROUPS, 1, DTC), skip, ssd_norm_w[0].reshape(1, inner),
                   nb, seq, ctx_len)
    y_lat = _matmul(yn, [ssd_out_w[0]], [0], d, tn=512, tm=512, out_dtype=BF16, name="ssd_out")

    g2_l, _ = mod_vecs(1, 2)
    sh4_l, _ = mod_vecs(1, 3)
    sc4_l, _ = mod_vecs(1, 4)
    x3_l, v_l = _resnorm(x2_l, y_lat, g2_l, norm2_w[1], sh4_l, sc4_l, n_batch=nb, seq=seq,
                         x_mode="slab", y_mode="row", xo_mode="slab", v_mode="slab", v_dtype=F32)
    moe1 = _moe(v_l, 1, moe_router_w[1], moe_router_bias[1], moe_w_gate, moe_w_up,
                moe_w_down, shared_w_gate[1], shared_w_up[1], shared_w_down[1])
    g5_l, _ = mod_vecs(1, 5)
    _, out = _resnorm(x3_l, moe1, g5_l, final_norm_w, None, None, n_batch=nb, seq=seq, v_dtype=F32,
                      write_x=False)
    return out.reshape(nb, seq, d)
```

```python
import functools
import math

import jax
import jax.numpy as jnp
from jax import lax
from jax.experimental import pallas as pl
from jax.experimental.pallas import tpu as pltpu

F32 = jnp.float32
BF16 = jnp.bfloat16

GRID_W = 64
EPS = 1e-6
S5_GROUP = 16
S5_STATE = 64
SSD_HEADDIM = 64
SSD_STATE = 128
SSD_GROUPS = 8
SSD_CONV = 5
SSD_CHUNK = 128
N_EXPERTS = 64
TOP_K = 8
N_EXPERT_GROUPS = 8
TOPK_GROUPS = 4
ROUTED_SCALE = 2.5
EXPERT_BLOCK = 256

VMEM_LIMIT_BYTES = 56 * 1024 * 1024
LANES = 128
SUBLANES = 8


def _cparams(n_axes):
    return pltpu.CompilerParams(
        dimension_semantics=("arbitrary",) * n_axes,
        vmem_limit_bytes=VMEM_LIMIT_BYTES)


def _silu(v):
    return v * jax.nn.sigmoid(v)


def _dot(a, b):
    return jnp.dot(a, b, preferred_element_type=F32)


def _split3(a):
    hi = a.astype(BF16)
    r1 = a - hi.astype(F32)
    mid = r1.astype(BF16)
    lo = (r1 - mid.astype(F32)).astype(BF16)
    return hi, mid, lo


def _dot_exact_rhs(a, sel):
    hi, mid, lo = _split3(a)
    return _dot(hi, sel) + _dot(mid, sel) + _dot(lo, sel)


def _dot_exact_lhs(sel, a):
    hi, mid, lo = _split3(a)
    return _dot(sel, hi) + _dot(sel, mid) + _dot(sel, lo)


def _ada_kernel(c_ref, w_ref, b_ref, o_ref):
    c = _silu(c_ref[...])
    o_ref[0] = _dot(c.astype(BF16), w_ref[0].astype(BF16)) + b_ref[0]


def _ada(cond, ada_w, ada_b):
    depth, d, n = ada_w.shape
    tn = 1024
    rows = cond.shape[0]
    return pl.pallas_call(
        _ada_kernel,
        out_shape=jax.ShapeDtypeStruct((depth, rows, n), F32),
        grid=(depth, n // tn),
        in_specs=[pl.BlockSpec((rows, d), lambda l, j: (0, 0)),
                  pl.BlockSpec((1, d, tn), lambda l, j: (l, 0, j)),
                  pl.BlockSpec((1, 1, tn), lambda l, j: (l, 0, j))],
        out_specs=pl.BlockSpec((1, rows, tn), lambda l, j: (l, 0, j)),
        compiler_params=_cparams(2),
        name="ada",
    )(cond, ada_w, ada_b.reshape(depth, 1, n))


def _mm_kernel(*refs, n_w, has_bias, epilogue):
    x_ref = refs[0]
    w_refs = refs[1:1 + n_w]
    pos = 1 + n_w
    b_refs = refs[pos:pos + n_w] if has_bias else ()
    pos += n_w if has_bias else 0
    o_ref = refs[pos]
    wbf_refs = refs[pos + 1:pos + 1 + n_w]

    @pl.when(pl.program_id(1) == 0)
    def _():
        for w_ref, wbf in zip(w_refs, wbf_refs):
            wbf[...] = w_ref[...].astype(BF16)

    x = x_ref[...].astype(BF16)
    zs = []
    for k in range(n_w):
        z = _dot(x, wbf_refs[k][...])
        if has_bias:
            z = z + b_refs[k][...]
        zs.append(z)
    if epilogue is None:
        out = zs[0]
    elif epilogue == "softplus":
        out = jax.nn.softplus(zs[0])
    elif epilogue == "glu":
        out = zs[0] * jax.nn.sigmoid(zs[1])
    elif epilogue == "swiglu":
        out = _silu(zs[0]) * zs[1]
    o_ref[...] = out.astype(o_ref.dtype)


def _matmul(x, ws, col_offsets, n_out, *, tn, tm, biases=None, epilogue=None,
            out_dtype=F32, name="matmul"):
    m, k = x.shape
    n_w = len(ws)
    has_bias = biases is not None
    in_specs = [pl.BlockSpec((tm, k), lambda j, i: (i, 0))]
    for off in col_offsets:
        in_specs.append(pl.BlockSpec((k, tn), lambda j, i, off=off: (0, j + off)))
    args = [x] + list(ws)
    if has_bias:
        for off in col_offsets:
            in_specs.append(pl.BlockSpec((1, tn), lambda j, i, off=off: (0, j + off)))
        args += [b.reshape(1, -1) for b in biases]
    return pl.pallas_call(
        functools.partial(_mm_kernel, n_w=n_w, has_bias=has_bias, epilogue=epilogue),
        out_shape=jax.ShapeDtypeStruct((m, n_out), out_dtype),
        grid=(n_out // tn, m // tm),
        in_specs=in_specs,
        out_specs=pl.BlockSpec((tm, tn), lambda j, i: (i, j)),
        scratch_shapes=[pltpu.VMEM((k, tn), BF16) for _ in range(n_w)],
        compiler_params=_cparams(2),
        name=name,
    )(*args)


SLABS = 8
ROW_TILE = 256


def _get_piece(ref, mode, j, rows, d):
    if mode == "slab":
        return ref[:, j * d:(j + 1) * d]
    return ref[j * rows:(j + 1) * rows, :]


def _put_piece(ref, mode, j, rows, d, val):
    if mode == "slab":
        ref[:, j * d:(j + 1) * d] = val.astype(ref.dtype)
    else:
        ref[j * rows:(j + 1) * rows, :] = val.astype(ref.dtype)


def _resnorm_kernel(*refs, has_y, write_x, x_mode, y_mode, xo_mode, v_mode, modulate, rows, d):
    it = iter(refs)
    x_ref = next(it)
    y_ref = next(it) if has_y else None
    g_ref = next(it) if has_y else None
    nw_ref = next(it)
    sh_ref = next(it) if modulate else None
    sc_ref = next(it) if modulate else None
    xo_ref = next(it) if write_x else None
    v_ref = next(it)
    nw = nw_ref[...]
    for j in range(SLABS):
        x = _get_piece(x_ref, x_mode, j, rows, d)
        if has_y:
            y = _get_piece(y_ref, y_mode, j, rows, d).astype(F32)
            x = x + g_ref[0] * y
            if write_x:
                _put_piece(xo_ref, xo_mode, j, rows, d, x)
        v = x * lax.rsqrt(jnp.mean(x * x, axis=-1, keepdims=True) + EPS) * nw
        if modulate:
            v = v * (1.0 + sc_ref[0]) + sh_ref[0]
        _put_piece(v_ref, v_mode, j, rows, d, v)


def _resnorm(x, y, gate, norm_w, shift, scale, *, n_batch, seq, x_mode="row",
             y_mode="row", xo_mode="row", v_mode="row", v_dtype=BF16, write_x=True,
             x_tile0=0, y_tile0=0, vec_index=None, name="resnorm"):
    d = x.shape[1]
    n = n_batch * seq
    if vec_index is None:
        vec_index = lambda b, t: b
    has_y = y is not None
    write_x = write_x and has_y
    modulate = shift is not None
    grows = seq // GRID_W
    tiles_per_seq = seq // ROW_TILE
    slab_used = "slab" in (x_mode, y_mode, xo_mode, v_mode)
    rows = grows if slab_used else ROW_TILE // SLABS
    if slab_used:
        assert grows * SLABS == ROW_TILE

    def spec(mode, tile0=0):
        if mode == "slab":
            return pl.BlockSpec((grows, SLABS * d), lambda b, t: (b, t))
        return pl.BlockSpec((ROW_TILE, d), lambda b, t: (b * tiles_per_seq + t + tile0, 0))

    def view(a, mode):
        return a.reshape(a.shape[0] // GRID_W, GRID_W * d) if mode == "slab" else a

    vec = pl.BlockSpec((1, 1, d), lambda b, t: (vec_index(b, t), 0, 0))
    in_specs = [spec(x_mode, x_tile0)]
    args = [view(x, x_mode)]
    if has_y:
        in_specs += [spec(y_mode, y_tile0), vec]
        args += [view(y, y_mode), gate]
    in_specs.append(pl.BlockSpec((1, d), lambda b, t: (0, 0)))
    args.append(norm_w.reshape(1, d))
    if modulate:
        in_specs += [vec, vec]
        args += [shift, scale]
    out_shape, out_specs = [], []
    def out_struct(mode, dtype):
        shape = (n // GRID_W, GRID_W * d) if mode == "slab" else (n, d)
        return jax.ShapeDtypeStruct(shape, dtype)

    if write_x:
        out_shape.append(out_struct(xo_mode, F32))
        out_specs.append(spec(xo_mode))
    out_shape.append(out_struct(v_mode, v_dtype))
    out_specs.append(spec(v_mode))
    outs = pl.pallas_call(
        functools.partial(_resnorm_kernel, has_y=has_y, write_x=write_x, x_mode=x_mode, y_mode=y_mode,
                          xo_mode=xo_mode, v_mode=v_mode, modulate=modulate, rows=rows, d=d),
        out_shape=out_shape,
        grid=(n_batch, tiles_per_seq),
        in_specs=in_specs,
        out_specs=out_specs,
        compiler_params=_cparams(2),
        name=name,
    )(*args)
    outs = [o.reshape(n, d) for o in outs]
    return (outs[0], outs[1]) if write_x else (None, outs[0])


S5_T = 128
S5_CB = 128
S5_NS = (S5_CB // S5_GROUP) * S5_STATE


def _reverse_rows(v_bf16):
    t = v_bf16.shape[0]
    r = lax.broadcasted_iota(jnp.int32, (t, t), 0)
    c = lax.broadcasted_iota(jnp.int32, (t, t), 1)
    flip = jnp.where(r + c == t - 1, 1.0, 0.0).astype(BF16)
    return _dot(flip, v_bf16)


def _s5_prep_kernel(xf_ref, xb_ref, nw_ref, sh_ref, sc_ref, o_ref, *, nb, d):
    nw = nw_ref[...]
    for b in range(nb):
        for rev, x_ref in ((False, xf_ref), (True, xb_ref)):
            x = x_ref[b]
            v = x * lax.rsqrt(jnp.mean(x * x, axis=-1, keepdims=True) + EPS) * nw
            v = (v * (1.0 + sc_ref[b]) + sh_ref[b]).astype(BF16)
            if rev:
                v = _reverse_rows(v).astype(BF16)
            k = b + (nb if rev else 0)
            o_ref[:, k * d:(k + 1) * d] = v


def _s5_prep(x3, norm_w, shift, scale):
    nb, s, d = x3.shape
    nt = s // S5_T
    return pl.pallas_call(
        functools.partial(_s5_prep_kernel, nb=nb, d=d),
        out_shape=jax.ShapeDtypeStruct((s, 2 * nb * d), BF16),
        grid=(nt,),
        in_specs=[pl.BlockSpec((nb, S5_T, d), lambda t: (0, t, 0)),
                  pl.BlockSpec((nb, S5_T, d), lambda t: (0, nt - 1 - t, 0)),
                  pl.BlockSpec((1, d), lambda t: (0, 0)),
                  pl.BlockSpec((nb, 1, d), lambda t: (0, 0, 0)),
                  pl.BlockSpec((nb, 1, d), lambda t: (0, 0, 0))],
        out_specs=pl.BlockSpec((S5_T, 2 * nb * d), lambda t: (t, 0)),
        compiler_params=_cparams(1),
        name="s5_prep",
    )(x3, x3, norm_w.reshape(1, d), shift, scale)


def _s5_scan_kernel(u_ref, bc_ref, are_ref, aim_ref, cc_ref, s0_ref, y_ref, sf_ref,
                    bu_ref, st_ref, *, n_tiles):
    i = pl.program_id(1)
    ns = S5_NS

    @pl.when(i == 0)
    def _():
        st_ref[...] = s0_ref[0]

    u = u_ref[...].astype(F32)
    row = lax.broadcasted_iota(jnp.int32, u.shape, 0)
    is_fwd = jnp.bitwise_and(row, SUBLANES - 1) < (SUBLANES // 2)
    lhs = jnp.concatenate([jnp.where(is_fwd, u, 0.0), jnp.where(is_fwd, 0.0, u)], axis=1)
    bu_ref[...] = _dot(lhs.astype(BF16), bc_ref[0])
    a_re = are_ref[0]
    a_im = aim_ref[0]

    def step(t, carry):
        s_re, s_im = carry
        r = pl.multiple_of(t * SUBLANES, SUBLANES)
        n_re = a_re * s_re - a_im * s_im + bu_ref[pl.ds(r, SUBLANES), 0:ns]
        n_im = a_re * s_im + a_im * s_re + bu_ref[pl.ds(r, SUBLANES), ns:2 * ns]
        bu_ref[pl.ds(r, SUBLANES), 0:ns] = n_re
        bu_ref[pl.ds(r, SUBLANES), ns:2 * ns] = n_im
        return n_re, n_im

    s_re, s_im = lax.fori_loop(0, S5_T, step, (st_ref[:, 0:ns], st_ref[:, ns:2 * ns]))
    st_ref[:, 0:ns] = s_re
    st_ref[:, ns:2 * ns] = s_im
    y2 = _dot(bu_ref[...].astype(BF16), cc_ref[0])
    y_ref[...] = jnp.where(is_fwd, y2[:, 0:S5_CB], y2[:, S5_CB:2 * S5_CB])

    @pl.when(i == n_tiles - 1)
    def _():
        sf_ref[0] = st_ref[...]


def _s5_scan(u2r, bcat, a_re, a_im, ccat, s0):
    rows, d = u2r.shape
    nblk = d // S5_CB
    tr = S5_T * SUBLANES
    n_tiles = rows // tr
    ns2 = 2 * S5_NS
    return pl.pallas_call(
        functools.partial(_s5_scan_kernel, n_tiles=n_tiles),
        out_shape=(jax.ShapeDtypeStruct((rows, d), F32),
                   jax.ShapeDtypeStruct((nblk, SUBLANES, ns2), F32)),
        grid=(nblk, n_tiles),
        in_specs=[pl.BlockSpec((tr, S5_CB), lambda j, i: (i, j)),
                  pl.BlockSpec((1, 2 * S5_CB, ns2), lambda j, i: (j, 0, 0)),
                  pl.BlockSpec((1, SUBLANES, S5_NS), lambda j, i: (j, 0, 0)),
                  pl.BlockSpec((1, SUBLANES, S5_NS), lambda j, i: (j, 0, 0)),
                  pl.BlockSpec((1, ns2, 2 * S5_CB), lambda j, i: (j, 0, 0)),
                  pl.BlockSpec((1, SUBLANES, ns2), lambda j, i: (j, 0, 0))],
        out_specs=(pl.BlockSpec((tr, S5_CB), lambda j, i: (i, j)),
                   pl.BlockSpec((1, SUBLANES, ns2), lambda j, i: (j, 0, 0))),
        scratch_shapes=[pltpu.VMEM((tr, ns2), F32), pltpu.VMEM((SUBLANES, ns2), F32)],
        compiler_params=_cparams(2),
        name="s5_scan",
    )(u2r, bcat, a_re, a_im, ccat, s0)


def _s5_out_kernel(yf_ref, yb_ref, u_ref, skip_ref, o_ref):
    yb = yb_ref[...]
    hi = yb.astype(BF16)
    lo = (yb - hi.astype(F32)).astype(BF16)
    y = (skip_ref[...] * u_ref[...].astype(F32) + yf_ref[...]
         + _reverse_rows(hi) + _reverse_rows(lo))
    o_ref[...] = jax.nn.gelu(y).astype(o_ref.dtype)


def _s5_out(y2, u2, skip, nb, d):
    s = y2.shape[0]
    nt = s // S5_T
    return pl.pallas_call(
        _s5_out_kernel,
        out_shape=jax.ShapeDtypeStruct((nb * s, d), BF16),
        grid=(nb, nt),
        in_specs=[pl.BlockSpec((S5_T, d), lambda b, t: (t, b)),
                  pl.BlockSpec((S5_T, d), lambda b, t: (nt - 1 - t, nb + b)),
                  pl.BlockSpec((S5_T, d), lambda b, t: (t, b)),
                  pl.BlockSpec((1, d), lambda b, t: (0, 0))],
        out_specs=pl.BlockSpec((S5_T, d), lambda b, t: (b * nt + t, 0)),
        compiler_params=_cparams(2),
        name="s5_out",
    )(y2, y2, u2, skip.reshape(1, d))


def _s5_pack_params(lam_re, lam_im, log_step, b_re, b_im, c_re, c_im, nb):
    f32 = F32
    g = lam_re.shape[1]
    gpb = S5_CB // S5_GROUP
    nblk = g // gpb
    eye = jnp.eye(gpb, dtype=f32)
    a_re_rows, a_im_rows, b_parts, c_parts = [], [], [], []
    for dr in range(2):
        lr, li = lam_re[dr].astype(f32), lam_im[dr].astype(f32)
        br, bi = b_re[dr].astype(f32), b_im[dr].astype(f32)
        dt = jnp.exp(log_step[dr].astype(f32))[:, None]
        mag = jnp.exp(lr * dt)
        abar_re = mag * jnp.cos(li * dt)
        abar_im = mag * jnp.sin(li * dt)
        num_re = abar_re - 1.0
        num_im = abar_im
        den = lr * lr + li * li
        f_re = (num_re * lr + num_im * li) / den
        f_im = (num_im * lr - num_re * li) / den
        bbar_re = f_re[..., None] * br - f_im[..., None] * bi
        bbar_im = f_re[..., None] * bi + f_im[..., None] * br
        a_re_rows.append(jnp.broadcast_to(abar_re.reshape(nblk, 1, S5_NS), (nblk, nb, S5_NS)))
        a_im_rows.append(jnp.broadcast_to(abar_im.reshape(nblk, 1, S5_NS), (nblk, nb, S5_NS)))

        def blockdiag_in(bb):
            b4 = bb.reshape(nblk, gpb, S5_STATE, S5_GROUP)
            return jnp.einsum('jgpk,gh->jgkhp', b4, eye).reshape(nblk, S5_CB, S5_NS)

        def blockdiag_out(cc):
            c4 = cc.astype(f32).reshape(nblk, gpb, S5_GROUP, S5_STATE)
            return jnp.einsum('jgkp,gh->jgphk', c4, eye).reshape(nblk, S5_NS, S5_CB)

        b_parts.append(jnp.concatenate([blockdiag_in(bbar_re), blockdiag_in(bbar_im)], axis=2))
        c_parts.append(jnp.concatenate([blockdiag_out(c_re[dr]), -blockdiag_out(c_im[dr])], axis=1))
    a_re = jnp.concatenate(a_re_rows, axis=1)
    a_im = jnp.concatenate(a_im_rows, axis=1)
    bcat = jnp.concatenate(b_parts, axis=1).astype(BF16)
    ccat = jnp.concatenate(c_parts, axis=2).astype(BF16)
    return bcat, a_re, a_im, ccat


def _conv_kernel(x_ref, w_ref, b_ref, o_ref, pad_ref, *, seq):
    halo = SUBLANES
    zeros = jnp.zeros((halo, pad_ref.shape[1]), F32)
    pad_ref[0:halo, :] = zeros
    pad_ref[halo + seq:2 * halo + seq, :] = zeros
    pad_ref[halo:halo + seq, :] = x_ref[...]
    acc = jnp.zeros((seq, pad_ref.shape[1]), F32) + b_ref[...]
    for k in range(SSD_CONV):
        off = halo + k - SSD_CONV // 2
        acc = acc + w_ref[k:k + 1, :] * pad_ref[off:off + seq, :]
    o_ref[...] = _silu(acc).astype(o_ref.dtype)


def _conv_silu(xbc, conv_w, conv_b, seq, row_block_offset, n_seq, out_dtype=BF16):
    _, c = xbc.shape
    tc = 512
    return pl.pallas_call(
        functools.partial(_conv_kernel, seq=seq),
        out_shape=jax.ShapeDtypeStruct((n_seq * seq, c), out_dtype),
        grid=(n_seq, c // tc),
        in_specs=[pl.BlockSpec((seq, tc), lambda b, j: (b + row_block_offset, j)),
                  pl.BlockSpec((SSD_CONV, tc), lambda b, j: (0, j)),
                  pl.BlockSpec((1, tc), lambda b, j: (0, j))],
        out_specs=pl.BlockSpec((seq, tc), lambda b, j: (b, j)),
        scratch_shapes=[pltpu.VMEM((seq + 2 * SUBLANES, tc), F32)],
        compiler_params=_cparams(2),
        name="ssd_conv",
    )(xbc, conv_w, conv_b.reshape(1, c))


HPG = 8
GCH = HPG * SSD_HEADDIM
DTC = 2 * HPG


def _ssd_kernel(xl_ref, btl_ref, cl_ref, dtl_ref, dttl_ref,
                xc_ref, btc_ref, dtc_ref, dttc_ref,
                z_ref, acol_ref, arow_ref, skip_ref, nw_ref, o_ref,
                htf_ref, htb_ref, yf_ref, yb_ref, *, n_lat, n_ctx):
    q = SSD_CHUNK
    r_i = lax.broadcasted_iota(jnp.int32, (q, q), 0)
    c_i = lax.broadcasted_iota(jnp.int32, (q, q), 1)
    tril = jnp.where(c_i <= r_i, 1.0, 0.0).astype(BF16)
    triu = jnp.where(r_i <= c_i, 1.0, 0.0).astype(BF16)
    lower = c_i <= r_i
    upper = c_i >= r_i
    lane = lax.broadcasted_iota(jnp.int32, (q, 2 * SSD_HEADDIM), 1)
    left = lane < SSD_HEADDIM
    e_r = lax.broadcasted_iota(jnp.int32, (DTC, GCH), 0)
    e_c = lax.broadcasted_iota(jnp.int32, (DTC, GCH), 1)
    a_row = arow_ref[0]
    a_col = acol_ref[0]

    def chunk(x_ref, bt_ref, c_ref, dt_ref, dtt_ref, r0, dr, ht_ref, y_ref):
        head_of_ch = lax.shift_right_logical(e_c, int(math.log2(SSD_HEADDIM)))
        expand = jnp.where(head_of_ch + dr * HPG == e_r, 1.0, 0.0).astype(BF16)
        x = x_ref[pl.ds(r0, q), :].astype(F32)
        bt = bt_ref[0, :, pl.ds(r0, q)]
        dt = dt_ref[0, pl.ds(r0, q), :]
        dtt = dtt_ref[0, :, pl.ds(r0, q)]
        la = dt * a_row
        lat = dtt * a_col
        cs = _dot_exact_lhs(tril, la)
        cst = _dot_exact_rhs(lat, triu)
        total = cs[q - 1:q, :]
        if dr == 1:
            rk, rkt = cs - la, cst - lat
            e_off, e_state = jnp.exp(total - rk), jnp.exp(rk)
        else:
            rk, rkt = cs, cst
            e_off, e_state = jnp.exp(rk), jnp.exp(total - rk)
        tot8 = jnp.broadcast_to(jnp.exp(total), (SUBLANES, DTC))
        pieces3 = [p.astype(F32) for p in _split3(tot8)]
        wo = _dot(jnp.concatenate([e_off] + pieces3, axis=0).astype(BF16), expand)
        w_off = wo[0:q, :]
        e_tot = (wo[q:q + 1, :] + wo[q + SUBLANES:q + SUBLANES + 1, :]
                 + wo[q + 2 * SUBLANES:q + 2 * SUBLANES + 1, :])
        w_state = _dot(e_state.astype(BF16), expand)
        dtx = _dot(dt.astype(BF16), expand)
        xd = x * dtx
        h_old = ht_ref[...]
        ht_ref[...] = h_old * e_tot + _dot(bt, (xd * w_state).astype(BF16))
        if y_ref is None:
            return
        cm = c_ref[pl.ds(r0, q), :]
        cb = _dot(cm, bt)
        xdb = xd.astype(BF16)
        pieces = []
        for pair in range(HPG // 2):
            ms = []
            for hh in (2 * pair, 2 * pair + 1):
                col = dr * HPG + hh
                colv = jnp.broadcast_to(rk[:, col:col + 1], (q, q))
                rowv = rkt[col:col + 1, :]
                if dr == 0:
                    seg = jnp.where(lower, colv - rowv, -1e30)
                else:
                    seg = jnp.where(upper, rowv - colv, -1e30)
                ms.append((cb * jnp.exp(seg)).astype(BF16))
            xp = xdb[:, pair * 2 * SSD_HEADDIM:(pair + 1) * 2 * SSD_HEADDIM]
            zero = jnp.zeros_like(xp)
            rhs = jnp.concatenate([jnp.where(left, xp, zero), jnp.where(left, zero, xp)], axis=0)
            pieces.append(_dot(jnp.concatenate(ms, axis=1), rhs))
        y_ref[pl.ds(r0, q), :] = (jnp.concatenate(pieces, axis=1)
                                  + _dot(cm, h_old.astype(BF16)) * w_off)

    htf_ref[...] = jnp.zeros_like(htf_ref)
    htb_ref[...] = jnp.zeros_like(htb_ref)

    def ctx_body(k, carry):
        rf = pl.multiple_of(k * q, q)
        rb = pl.multiple_of((n_ctx - 1 - k) * q, q)
        chunk(xc_ref, btc_ref, None, dtc_ref, dttc_ref, rf, 0, htf_ref, None)
        chunk(xc_ref, btc_ref, None, dtc_ref, dttc_ref, rb, 1, htb_ref, None)
        return carry

    lax.fori_loop(0, n_ctx, ctx_body, 0)

    def lat_body(k, carry):
        rf = pl.multiple_of(k * q, q)
        rb = pl.multiple_of((n_lat - 1 - k) * q, q)
        chunk(xl_ref, btl_ref, cl_ref, dtl_ref, dttl_ref, rf, 0, htf_ref, yf_ref)
        chunk(xl_ref, btl_ref, cl_ref, dtl_ref, dttl_ref, rb, 1, htb_ref, yb_ref)
        return carry

    lax.fori_loop(0, n_lat, lat_body, 0)

    def finish(k, carry):
        r0 = pl.multiple_of(k * q, q)
        x = xl_ref[pl.ds(r0, q), :].astype(F32)
        y = yf_ref[pl.ds(r0, q), :] + yb_ref[pl.ds(r0, q), :] + skip_ref[...] * x
        v = y * _silu(z_ref[pl.ds(r0, q), :].astype(F32))
        v = v * lax.rsqrt(jnp.mean(v * v, axis=-1, keepdims=True) + EPS) * nw_ref[...]
        o_ref[pl.ds(r0, q), :] = v.astype(o_ref.dtype)
        return carry

    lax.fori_loop(0, n_lat, finish, 0)


def _ssd_scan(xc_l, xc_c, bt_l, bt_c, z, dtg, dtgt, a_col, a_row, skip, norm_w, nb, seq, ctx_len):
    n_lat, n_ctx = seq // SSD_CHUNK, ctx_len // SSD_CHUNK
    inner = SSD_GROUPS * GCH
    coff = inner // SSD_STATE + SSD_GROUPS
    cb0 = nb * seq // ctx_len
    in_specs = [
        pl.BlockSpec((seq, GCH), lambda b, g: (b, g)),
        pl.BlockSpec((1, SSD_STATE, seq), lambda b, g: (g, 0, b)),
        pl.BlockSpec((seq, SSD_STATE), lambda b, g: (b, coff + g)),
        pl.BlockSpec((1, seq, DTC), lambda b, g: (g, b, 0)),
        pl.BlockSpec((1, DTC, seq), lambda b, g: (g, 0, b)),
        pl.BlockSpec((ctx_len, GCH), lambda b, g: (b, g)),
        pl.BlockSpec((1, SSD_STATE, ctx_len), lambda b, g: (g, 0, b)),
        pl.BlockSpec((1, ctx_len, DTC), lambda b, g: (g, cb0 + b, 0)),
        pl.BlockSpec((1, DTC, ctx_len), lambda b, g: (g, 0, cb0 + b)),
        pl.BlockSpec((seq, GCH), lambda b, g: (b, g)),
        pl.BlockSpec((1, DTC, 1), lambda b, g: (g, 0, 0)),
        pl.BlockSpec((1, 1, DTC), lambda b, g: (g, 0, 0)),
        pl.BlockSpec((1, GCH), lambda b, g: (0, g)),
        pl.BlockSpec((1, GCH), lambda b, g: (0, g)),
    ]
    return pl.pallas_call(
        functools.partial(_ssd_kernel, n_lat=n_lat, n_ctx=n_ctx),
        out_shape=jax.ShapeDtypeStruct((nb * seq, inner), BF16),
        grid=(nb, SSD_GROUPS),
        in_specs=in_specs,
        out_specs=pl.BlockSpec((seq, GCH), lambda b, g: (b, g)),
        scratch_shapes=[pltpu.VMEM((SSD_STATE, GCH), F32), pltpu.VMEM((SSD_STATE, GCH), F32),
                        pltpu.VMEM((seq, GCH), F32), pltpu.VMEM((seq, GCH), F32)],
        compiler_params=_cparams(2),
        name="ssd_scan",
    )(xc_l, bt_l, xc_l, dtg, dtgt, xc_c, bt_c, dtg, dtgt, z, a_col, a_row, skip, norm_w)


def _expert_kernel(be_ref, nu_ref, first_ref, slot_ref, next_ref, x_ref, wg_hbm, wu_hbm, wd_hbm,
                   o_ref, stage_g, stage_u, stage_d, wgb, wub, wdb, sem, *, layer):
    i = pl.program_id(0)

    def weight_copies(e, s):
        return (pltpu.make_async_copy(wg_hbm.at[layer, e], stage_g.at[s], sem.at[s, 0]),
                pltpu.make_async_copy(wu_hbm.at[layer, e], stage_u.at[s], sem.at[s, 1]),
                pltpu.make_async_copy(wd_hbm.at[layer, e], stage_d.at[s], sem.at[s, 2]))

    @pl.when(i == 0)
    def _():
        for c in weight_copies(be_ref[0], 0):
            c.start()

    @pl.when(jnp.logical_and(i < nu_ref[0], first_ref[i] == 1))
    def _():
        s = slot_ref[i]

        @pl.when(next_ref[i] >= 0)
        def _():
            for c in weight_copies(next_ref[i], 1 - s):
                c.start()

        for c in weight_copies(be_ref[i], s):
            c.wait()
        wgb[...] = stage_g[s].astype(BF16)
        wub[...] = stage_u[s].astype(BF16)
        wdb[...] = stage_d[s].astype(BF16)

    @pl.when(i < nu_ref[0])
    def _():
        x = x_ref[...].astype(BF16)
        h = _silu(_dot(x, wgb[...])) * _dot(x, wub[...])
        o_ref[...] = _dot(h.astype(BF16), wdb[...]).astype(o_ref.dtype)

    @pl.when(i >= nu_ref[0])
    def _():
        o_ref[...] = jnp.zeros_like(o_ref)


def _experts(x_sorted, block_e, n_used, first, slot, next_e, w_gate, w_up, w_down, layer):
    cap, d = x_sorted.shape
    ff = w_gate.shape[3]
    n_blocks = cap // EXPERT_BLOCK

    def row_block(i, be, nu, fi, sl, nx):
        return (jnp.minimum(i, nu[0] - 1), 0)

    hbm = pl.BlockSpec(memory_space=pl.ANY)
    grid_spec = pltpu.PrefetchScalarGridSpec(
        num_scalar_prefetch=5,
        grid=(n_blocks,),
        in_specs=[pl.BlockSpec((EXPERT_BLOCK, d), row_block), hbm, hbm, hbm],
        out_specs=pl.BlockSpec((EXPERT_BLOCK, d), lambda i, be, nu, fi, sl, nx: (i, 0)),
        scratch_shapes=[pltpu.VMEM((2, d, ff), F32), pltpu.VMEM((2, d, ff), F32),
                        pltpu.VMEM((2, ff, d), F32),
                        pltpu.VMEM((d, ff), BF16), pltpu.VMEM((d, ff), BF16),
                        pltpu.VMEM((ff, d), BF16), pltpu.SemaphoreType.DMA((2, 3))])
    return pl.pallas_call(
        functools.partial(_expert_kernel, layer=layer),
        out_shape=jax.ShapeDtypeStruct((cap, d), F32),
        grid_spec=grid_spec,
        compiler_params=_cparams(1),
        name="moe_experts",
    )(block_e, n_used, first, slot, next_e, x_sorted, w_gate, w_up, w_down)


DISPATCH_T = 256
COMBINE_T = 128


def _dispatch_kernel(pe_ref, pd_ref, dest_ref, v_ref, xs_ref, zero_buf, sem):
    t_rows = v_ref.shape[0]

    @pl.when(pl.program_id(0) == 0)
    def _():
        zero_buf[...] = jnp.zeros_like(zero_buf)

        def fill(e, carry):
            @pl.when(pd_ref[e] > 0)
            def _():
                start = pl.multiple_of(pe_ref[e] - EXPERT_BLOCK, EXPERT_BLOCK)
                pltpu.make_async_copy(zero_buf, xs_ref.at[pl.ds(start, EXPERT_BLOCK)], sem).start()
            return carry

        def drain(e, carry):
            @pl.when(pd_ref[e] > 0)
            def _():
                pltpu.make_async_copy(zero_buf, xs_ref.at[pl.ds(0, EXPERT_BLOCK)], sem).wait()
            return carry

        lax.fori_loop(0, N_EXPERTS, fill, 0)
        lax.fori_loop(0, N_EXPERTS, drain, 0)

        n_blocks = xs_ref.shape[0] // EXPERT_BLOCK
        first_unused = pe_ref[N_EXPERTS - 1] // EXPERT_BLOCK

        def fill_tail(b, carry):
            start = pl.multiple_of(b * EXPERT_BLOCK, EXPERT_BLOCK)
            pltpu.make_async_copy(zero_buf, xs_ref.at[pl.ds(start, EXPERT_BLOCK)], sem).start()
            return carry

        def drain_tail(b, carry):
            pltpu.make_async_copy(zero_buf, xs_ref.at[pl.ds(0, EXPERT_BLOCK)], sem).wait()
            return carry

        lax.fori_loop(first_unused, n_blocks, fill_tail, 0)
        lax.fori_loop(first_unused, n_blocks, drain_tail, 0)

    def issue(t, carry):
        for k in range(TOP_K):
            pltpu.make_async_copy(v_ref.at[pl.ds(t, 1)],
                                  xs_ref.at[pl.ds(dest_ref[k, t], 1)], sem).start()
        return carry

    lax.fori_loop(0, t_rows, issue, 0)
    for k in range(TOP_K):
        pltpu.make_async_copy(v_ref, xs_ref.at[pl.ds(0, t_rows)], sem).wait()


def _dispatch(v, dest, pad_end, padded, cap):
    n, d = v.shape
    grid_spec = pltpu.PrefetchScalarGridSpec(
        num_scalar_prefetch=2,
        grid=(n // DISPATCH_T,),
        in_specs=[pl.BlockSpec((TOP_K, DISPATCH_T), lambda i, pe, pd: (0, i),
                               memory_space=pltpu.SMEM),
                  pl.BlockSpec((DISPATCH_T, d), lambda i, pe, pd: (i, 0))],
        out_specs=pl.BlockSpec(memory_space=pl.ANY),
        scratch_shapes=[pltpu.VMEM((EXPERT_BLOCK, d), F32), pltpu.SemaphoreType.DMA(())])
    return pl.pallas_call(
        _dispatch_kernel,
        out_shape=jax.ShapeDtypeStruct((cap, d), F32),
        grid_spec=grid_spec,
        compiler_params=_cparams(1),
        name="moe_dispatch",
    )(pad_end, padded, dest, v)


def _combine_kernel(dest_ref, y_ref, gate_ref, sh_ref, o_ref, buf, sem):
    t_rows = o_ref.shape[0]

    def issue(t, carry):
        for k in range(TOP_K):
            pltpu.make_async_copy(y_ref.at[pl.ds(dest_ref[k, t], 1)],
                                  buf.at[k, pl.ds(t, 1)], sem).start()
        return carry

    lax.fori_loop(0, t_rows, issue, 0)
    for k in range(TOP_K):
        pltpu.make_async_copy(y_ref.at[pl.ds(0, t_rows)], buf.at[k], sem).wait()
    acc = sh_ref[...]
    for k in range(TOP_K):
        acc = acc + gate_ref[:, k:k + 1] * buf[k]
    o_ref[...] = acc


def _combine(y_sorted, dest, gate_t, shared):
    n, d = shared.shape
    return pl.pallas_call(
        _combine_kernel,
        out_shape=jax.ShapeDtypeStruct((n, d), F32),
        grid=(n // COMBINE_T,),
        in_specs=[pl.BlockSpec((TOP_K, COMBINE_T), lambda i: (0, i), memory_space=pltpu.SMEM),
                  pl.BlockSpec(memory_space=pl.ANY),
                  pl.BlockSpec((COMBINE_T, TOP_K), lambda i: (i, 0)),
                  pl.BlockSpec((COMBINE_T, d), lambda i: (i, 0))],
        out_specs=pl.BlockSpec((COMBINE_T, d), lambda i: (i, 0)),
        scratch_shapes=[pltpu.VMEM((TOP_K, COMBINE_T, d), F32), pltpu.SemaphoreType.DMA(())],
        compiler_params=_cparams(1),
        name="moe_combine",
    )(dest, y_sorted, gate_t, shared)


ROUTE_T = 512
GROUP_SIZE = N_EXPERTS // N_EXPERT_GROUPS


def _route_kernel(v_ref, rwh_ref, rwl_ref, bias_ref, eid_ref, rank_ref, gate_ref, cnt_ref,
                  carry_ref, *, n_tiles):
    i = pl.program_id(0)
    t = v_ref.shape[0]
    ng, gs = N_EXPERT_GROUPS, GROUP_SIZE
    neg = -jnp.inf

    @pl.when(i == 0)
    def _():
        carry_ref[...] = jnp.zeros_like(carry_ref)

    v = v_ref[...].astype(BF16)
    nt_dims = (((1,), (1,)), ((), ()))
    logits = (lax.dot_general(rwh_ref[...], v, nt_dims, preferred_element_type=F32)
              + lax.dot_general(rwl_ref[...], v, nt_dims, preferred_element_type=F32))
    scores = jax.nn.sigmoid(logits)
    biased = scores + bias_ref[:, 0:1]
    x3 = biased.reshape(ng, gs, t)
    s3 = scores.reshape(ng, gs, t)
    mi = lax.broadcasted_iota(jnp.int32, (ng, gs, t), 1).astype(F32)
    fi = lax.broadcasted_iota(jnp.int32, (ng, gs, t), 0).astype(F32) * gs + mi
    gi = lax.broadcasted_iota(jnp.int32, (ng, 1, t), 0).astype(F32)

    m1 = jnp.max(x3, axis=1, keepdims=True)
    i1 = jnp.min(jnp.where(x3 == m1, mi, float(gs)), axis=1, keepdims=True)
    m2 = jnp.max(jnp.where(mi == i1, neg, x3), axis=1, keepdims=True)
    cur = m1 + m2
    gmask = jnp.zeros((ng, 1, t), F32)
    for _ in range(TOPK_GROUPS):
        gm = jnp.max(cur, axis=0, keepdims=True)
        idx = jnp.min(jnp.where(cur == gm, gi, float(ng)), axis=0, keepdims=True)
        hit = gi == idx
        gmask = jnp.where(hit, 1.0, gmask)
        cur = jnp.where(hit, neg, cur)

    cand = jnp.where(gmask > 0.0, x3, neg)
    sel = jnp.zeros((ng, gs, t), F32)
    eids = []
    for _ in range(TOP_K):
        m = jnp.max(jnp.max(cand, axis=1, keepdims=True), axis=0, keepdims=True)
        idx = jnp.where(cand == m, fi, float(N_EXPERTS))
        idx = jnp.min(jnp.min(idx, axis=1, keepdims=True), axis=0, keepdims=True)
        hit = fi == idx
        sel = jnp.where(hit, 1.0, sel)
        cand = jnp.where(hit, neg, cand)
        eids.append(idx)

    selr = sel.reshape(N_EXPERTS, t)
    r_i = lax.broadcasted_iota(jnp.int32, (t, t), 0)
    c_i = lax.broadcasted_iota(jnp.int32, (t, t), 1)
    before = jnp.where(r_i < c_i, 1.0, 0.0).astype(BF16)
    rank = _dot(selr.astype(BF16), before) + carry_ref[:, 0:1]
    carry_ref[...] = carry_ref[...] + jnp.sum(selr, axis=1, keepdims=True)
    rank3 = rank.reshape(ng, gs, t)

    gsel = sel * s3
    denom = jnp.sum(jnp.sum(gsel, axis=1, keepdims=True), axis=0, keepdims=True)
    gate3 = gsel / denom * ROUTED_SCALE

    def pick(a3, hit):
        return jnp.sum(jnp.sum(jnp.where(hit, a3, 0.0), axis=1, keepdims=True), axis=0,
                       keepdims=True).reshape(1, t)

    for k in range(TOP_K):
        hit = fi == eids[k]
        eid_ref[k:k + 1, :] = eids[k].reshape(1, t).astype(jnp.int32)
        rank_ref[k:k + 1, :] = pick(rank3, hit).astype(jnp.int32)
        gate_ref[k:k + 1, :] = pick(gate3, hit)

    @pl.when(i == n_tiles - 1)
    def _():
        cnt_ref[...] = carry_ref[...]


def _route(v, router_w, router_bias):
    n, d = v.shape
    n_tiles = n // ROUTE_T
    rwt = router_w.astype(F32).T
    rwh = rwt.astype(BF16)
    rwl = (rwt - rwh.astype(F32)).astype(BF16)
    bias = jnp.broadcast_to(router_bias.astype(F32)[:, None], (N_EXPERTS, LANES))
    slot = pl.BlockSpec((TOP_K, ROUTE_T), lambda i: (0, i))
    full = pl.BlockSpec((N_EXPERTS, d), lambda i: (0, 0))
    return pl.pallas_call(
        functools.partial(_route_kernel, n_tiles=n_tiles),
        out_shape=(jax.ShapeDtypeStruct((TOP_K, n), jnp.int32),
                   jax.ShapeDtypeStruct((TOP_K, n), jnp.int32),
                   jax.ShapeDtypeStruct((TOP_K, n), F32),
                   jax.ShapeDtypeStruct((N_EXPERTS, LANES), F32)),
        grid=(n_tiles,),
        in_specs=[pl.BlockSpec((ROUTE_T, d), lambda i: (i, 0)), full, full,
                  pl.BlockSpec((N_EXPERTS, LANES), lambda i: (0, 0))],
        out_specs=(slot, slot, slot, pl.BlockSpec((N_EXPERTS, LANES), lambda i: (0, 0))),
        scratch_shapes=[pltpu.VMEM((N_EXPERTS, LANES), F32)],
        compiler_params=_cparams(1),
        name="moe_route",
    )(v, rwh, rwl, bias)


def _slot_rows_kernel(ps_ref, eid_ref, rank_ref, o_ref):
    eid = eid_ref[...]
    acc = rank_ref[...]
    for e in range(N_EXPERTS):
        acc = acc + jnp.where(eid == e, ps_ref[e], 0)
    o_ref[...] = acc


def _slot_rows(eid, rank, pad_start):
    k, n = eid.shape
    whole = lambda i, ps: (0, 0)
    grid_spec = pltpu.PrefetchScalarGridSpec(
        num_scalar_prefetch=1, grid=(1,),
        in_specs=[pl.BlockSpec((k, n), whole), pl.BlockSpec((k, n), whole)],
        out_specs=pl.BlockSpec((k, n), whole))
    return pl.pallas_call(
        _slot_rows_kernel,
        out_shape=jax.ShapeDtypeStruct((k, n), jnp.int32),
        grid_spec=grid_spec,
        compiler_params=_cparams(1),
        name="moe_slot_rows",
    )(pad_start, eid, rank)


def _moe(v, layer, router_w, router_bias, w_gate, w_up, w_down, sw_gate, sw_up, sw_down):
    n, d = v.shape
    eid, rank, gate, cnt = _route(v, router_w, router_bias)
    nk = n * TOP_K
    cap = -(-nk // EXPERT_BLOCK) * EXPERT_BLOCK + N_EXPERTS * EXPERT_BLOCK
    n_blocks = cap // EXPERT_BLOCK
    counts = cnt[:, 0].astype(jnp.int32)
    padded = (counts + EXPERT_BLOCK - 1) // EXPERT_BLOCK * EXPERT_BLOCK
    pad_end = jnp.cumsum(padded)
    pad_start = pad_end - padded
    dest = _slot_rows(eid, rank, pad_start.astype(jnp.int32))
    block_start = jnp.arange(n_blocks, dtype=jnp.int32) * EXPERT_BLOCK
    block_e = jnp.minimum(jnp.sum((pad_end[None, :] <= block_start[:, None]).astype(jnp.int32), axis=1),
                          N_EXPERTS - 1).astype(jnp.int32)
    n_used = (pad_end[-1] // EXPERT_BLOCK).astype(jnp.int32).reshape(1)
    block_id = jnp.arange(n_blocks, dtype=jnp.int32)
    first = ((pad_start[block_e] == block_start) & (block_id < n_used[0])).astype(jnp.int32)
    slot = ((jnp.cumsum(first) - 1) % 2).astype(jnp.int32)
    experts = jnp.arange(N_EXPERTS, dtype=jnp.int32)
    later = (experts[None, :] > experts[:, None]) & (padded[None, :] > 0)
    next_of = jnp.min(jnp.where(later, experts[None, :], N_EXPERTS), axis=1)
    next_of = jnp.where(next_of == N_EXPERTS, -1, next_of).astype(jnp.int32)
    next_e = next_of[block_e]
    x_sorted = _dispatch(v, dest, pad_end.astype(jnp.int32), padded.astype(jnp.int32), cap)
    y_sorted = _experts(x_sorted, block_e, n_used, first, slot, next_e, w_gate, w_up, w_down, layer)
    hs = _matmul(v, [sw_gate, sw_up], [0, 0], sw_gate.shape[1], tn=sw_gate.shape[1], tm=512,
                 epilogue="swiglu", out_dtype=BF16, name="shared_up")
    shared = _matmul(hs, [sw_down], [0], d, tn=1024, tm=512, name="shared_down")
    return _combine(y_sorted, dest, gate.T, shared)


def kernel(x, c, ctx, c_ctx, ada_w, ada_b, norm1_w, norm2_w, s5_lambda_re, s5_lambda_im, s5_log_step, s5_b_re, s5_b_im, s5_c_re, s5_c_im, s5_d, s5_glu_w, s5_glu_b, ssd_in_w, ssd_conv_w, ssd_conv_b, ssd_dt_bias, ssd_a_log, ssd_d, ssd_norm_w, ssd_out_w, moe_router_w, moe_router_bias, moe_w_gate, moe_w_up, moe_w_down, shared_w_gate, shared_w_up, shared_w_down, final_norm_w):
    nb, seq, d = x.shape
    ctx_len = ctx.shape[1]
    n_lat = nb * seq
    n_ctx = nb * ctx_len

    cond = jnp.concatenate([c, c_ctx[None, :], jnp.zeros((SUBLANES - nb - 1, d), F32)], axis=0)
    mods = _ada(cond, ada_w, ada_b)

    def mod_vecs(layer, k):
        m = mods[layer, :, k * d:(k + 1) * d]
        lat = m[:nb].reshape(nb, 1, d)
        cx = jnp.broadcast_to(m[nb].reshape(1, 1, d), (nb, 1, d))
        return lat, cx

    x_lat = x.reshape(n_lat, d)
    x_ctx = ctx.reshape(n_ctx, d)

    sh_l, sh_c = mod_vecs(0, 0)
    sc_l, sc_c = mod_vecs(0, 1)
    u2_c = _s5_prep(ctx, norm1_w[0], sh_c, sc_c)
    u2_l = _s5_prep(x, norm1_w[0], sh_l, sc_l)
    bcat, a_re, a_im, ccat = _s5_pack_params(
        s5_lambda_re[0], s5_lambda_im[0], s5_log_step[0], s5_b_re[0], s5_b_im[0],
        s5_c_re[0], s5_c_im[0], nb)
    nblk = d // S5_CB
    s0 = jnp.zeros((nblk, SUBLANES, 2 * S5_NS), F32)
    y2_c, s_ctx = _s5_scan(u2_c.reshape(ctx_len * 2 * nb, d), bcat, a_re, a_im, ccat, s0)
    y2_l, _ = _s5_scan(u2_l.reshape(seq * 2 * nb, d), bcat, a_re, a_im, ccat, s_ctx)
    g_l = _s5_out(y2_l.reshape(seq, 2 * nb * d), u2_l, s5_d[0], nb, d)
    g_c = _s5_out(y2_c.reshape(ctx_len, 2 * nb * d), u2_c, s5_d[0], nb, d)
    g_all = jnp.concatenate([g_l, g_c], axis=0)
    half = s5_glu_w.shape[2] // 2
    tn = 1024
    glu = _matmul(g_all, [s5_glu_w[0], s5_glu_w[0]], [0, half // tn], half, tn=tn, tm=512,
                  biases=[s5_glu_b[0], s5_glu_b[0]], epilogue="glu", out_dtype=BF16, name="s5_glu")

    n_all = n_lat + n_ctx
    lat_tiles = seq // ROW_TILE

    def all_vecs(layer, k):
        return mods[layer, :nb + 1, k * d:(k + 1) * d].reshape(nb + 1, 1, d)

    def all_index(b, t):
        return jnp.minimum(t // lat_tiles, nb)

    xs0 = jnp.concatenate([x_lat, x_ctx], axis=0)
    xs1, v_all = _resnorm(xs0, glu, all_vecs(0, 2), norm2_w[0], all_vecs(0, 3), all_vecs(0, 4),
                          n_batch=1, seq=n_all, vec_index=all_index, v_dtype=F32)
    moe0 = _moe(v_all, 0, moe_router_w[0], moe_router_bias[0], moe_w_gate, moe_w_up,
                moe_w_down, shared_w_gate[0], shared_w_up[0], shared_w_down[0])

    g5_l, g5_c = mod_vecs(0, 5)
    sh_l, sh_c = mod_vecs(1, 0)
    sc_l, sc_c = mod_vecs(1, 1)
    x2_l, u_l = _resnorm(xs1, moe0, g5_l, norm1_w[1], sh_l, sc_l, n_batch=nb, seq=seq,
                         x_mode="slab", y_mode="slab", xo_mode="row", v_mode="row")
    ctx_tile0 = n_lat // ROW_TILE
    _, u_c = _resnorm(xs1, moe0, g5_c, norm1_w[1], sh_c, sc_c, n_batch=nb, seq=ctx_len,
                      write_x=False, x_tile0=ctx_tile0, y_tile0=ctx_tile0)
    u_all = jnp.concatenate([u_l, u_c], axis=0)
    in_w = ssd_in_w[0]
    inner = ssd_out_w.shape[1]
    conv_dim = ssd_conv_w.shape[2]
    heads = inner // SSD_HEADDIM
    tn = 1024
    z_all = _matmul(u_all, [in_w], [0], inner, tn=tn, tm=512, out_dtype=BF16, name="ssd_in_z")
    xbc = _matmul(u_all, [in_w], [inner // tn], conv_dim, tn=tn, tm=512, name="ssd_in_xbc")
    dt_bias = ssd_dt_bias[0].reshape(-1)
    dt_all = _matmul(u_all, [in_w], [(inner + conv_dim) // LANES], 2 * heads, tn=LANES, tm=512,
                     biases=[jnp.pad(dt_bias, (inner + conv_dim, 0))], epilogue="softplus",
                     name="ssd_in_dt")
    xc_l = _conv_silu(xbc, ssd_conv_w[0], ssd_conv_b[0], seq, 0, nb)
    xc_c = _conv_silu(xbc, ssd_conv_w[0], ssd_conv_b[0], ctx_len, n_lat // ctx_len, nb)
    rows = n_lat + n_ctx

    def b_transposed(xc):
        bm = xc[:, inner:inner + SSD_GROUPS * SSD_STATE]
        return bm.reshape(xc.shape[0], SSD_GROUPS, SSD_STATE).transpose(1, 2, 0)

    dtg = dt_all.reshape(rows, 2, SSD_GROUPS, HPG).transpose(2, 0, 1, 3).reshape(SSD_GROUPS, rows, DTC)
    dtgt = dtg.transpose(0, 2, 1)
    a = -jnp.exp(ssd_a_log[0].astype(F32))
    a_g = a.reshape(2, SSD_GROUPS, HPG).transpose(1, 0, 2).reshape(SSD_GROUPS, DTC)
    skip = jnp.repeat(ssd_d[0].astype(F32), SSD_HEADDIM).reshape(1, inner)
    yn = _ssd_scan(xc_l, xc_c, b_transposed(xc_l), b_transposed(xc_c), z_all, dtg, dtgt,
                   a_g.reshape(SSD_GROUPS, DTC, 1),
                   a_g.reshape(SSD_GROUPS, 1, DTC), skip, ssd_norm_w[0].reshape(1, inner),
                   nb, seq, ctx_len)
    y_lat = _matmul(yn, [ssd_out_w[0]], [0], d, tn=512, tm=512, out_dtype=BF16, name="ssd_out")

    g2_l, _ = mod_vecs(1, 2)
    sh4_l, _ = mod_vecs(1, 3)
    sc4_l, _ = mod_vecs(1, 4)
    x3_l, v_l = _resnorm(x2_l, y_lat, g2_l, norm2_w[1], sh4_l, sc4_l, n_batch=nb, seq=seq,
                         v_dtype=F32)
    moe1 = _moe(v_l, 1, moe_router_w[1], moe_router_bias[1], moe_w_gate, moe_w_up,
                moe_w_down, shared_w_gate[1], shared_w_up[1], shared_w_down[1])
    g5_l, _ = mod_vecs(1, 5)
    _, out = _resnorm(x3_l, moe1, g5_l, final_norm_w, None, None, n_batch=nb, seq=seq, v_dtype=F32,
                      write_x=False, v_mode="slab")
    return out.reshape(nb, seq, d)
```

```python
import functools
import math

import jax
import jax.numpy as jnp
from jax import lax
from jax.experimental import pallas as pl
from jax.experimental.pallas import tpu as pltpu

F32 = jnp.float32
BF16 = jnp.bfloat16

GRID_W = 64
EPS = 1e-6
S5_GROUP = 16
S5_STATE = 64
SSD_HEADDIM = 64
SSD_STATE = 128
SSD_GROUPS = 8
SSD_CONV = 5
SSD_CHUNK = 128
N_EXPERTS = 64
TOP_K = 8
N_EXPERT_GROUPS = 8
TOPK_GROUPS = 4
ROUTED_SCALE = 2.5
EXPERT_BLOCK = 256

VMEM_LIMIT_BYTES = 56 * 1024 * 1024
LANES = 128
SUBLANES = 8


def _cparams(n_axes):
    return pltpu.CompilerParams(
        dimension_semantics=("arbitrary",) * n_axes,
        vmem_limit_bytes=VMEM_LIMIT_BYTES)


def _silu(v):
    return v * jax.nn.sigmoid(v)


def _dot(a, b):
    return jnp.dot(a, b, preferred_element_type=F32)


def _split3(a):
    hi = a.astype(BF16)
    r1 = a - hi.astype(F32)
    mid = r1.astype(BF16)
    lo = (r1 - mid.astype(F32)).astype(BF16)
    return hi, mid, lo


def _dot_exact_rhs(a, sel):
    hi, mid, lo = _split3(a)
    return _dot(hi, sel) + _dot(mid, sel) + _dot(lo, sel)


def _dot_exact_lhs(sel, a):
    hi, mid, lo = _split3(a)
    return _dot(sel, hi) + _dot(sel, mid) + _dot(sel, lo)


def _ada_kernel(c_ref, w_ref, b_ref, o_ref):
    c = _silu(c_ref[...])
    o_ref[0] = _dot(c.astype(BF16), w_ref[0].astype(BF16)) + b_ref[0]


def _ada(cond, ada_w, ada_b):
    depth, d, n = ada_w.shape
    tn = 1024
    rows = cond.shape[0]
    return pl.pallas_call(
        _ada_kernel,
        out_shape=jax.ShapeDtypeStruct((depth, rows, n), F32),
        grid=(depth, n // tn),
        in_specs=[pl.BlockSpec((rows, d), lambda l, j: (0, 0)),
                  pl.BlockSpec((1, d, tn), lambda l, j: (l, 0, j)),
                  pl.BlockSpec((1, 1, tn), lambda l, j: (l, 0, j))],
        out_specs=pl.BlockSpec((1, rows, tn), lambda l, j: (l, 0, j)),
        compiler_params=_cparams(2),
        name="ada",
    )(cond, ada_w, ada_b.reshape(depth, 1, n))


def _mm_kernel(*refs, n_w, has_bias, epilogue):
    x_ref = refs[0]
    w_refs = refs[1:1 + n_w]
    pos = 1 + n_w
    b_refs = refs[pos:pos + n_w] if has_bias else ()
    pos += n_w if has_bias else 0
    o_ref = refs[pos]
    wbf_refs = refs[pos + 1:pos + 1 + n_w]

    @pl.when(pl.program_id(1) == 0)
    def _():
        for w_ref, wbf in zip(w_refs, wbf_refs):
            wbf[...] = w_ref[...].astype(BF16)

    x = x_ref[...].astype(BF16)
    zs = []
    for k in range(n_w):
        z = _dot(x, wbf_refs[k][...])
        if has_bias:
            z = z + b_refs[k][...]
        zs.append(z)
    if epilogue is None:
        out = zs[0]
    elif epilogue == "softplus":
        out = jax.nn.softplus(zs[0])
    elif epilogue == "glu":
        out = zs[0] * jax.nn.sigmoid(zs[1])
    elif epilogue == "swiglu":
        out = _silu(zs[0]) * zs[1]
    o_ref[...] = out.astype(o_ref.dtype)


def _matmul(x, ws, col_offsets, n_out, *, tn, tm, biases=None, epilogue=None,
            out_dtype=F32, name="matmul"):
    m, k = x.shape
    n_w = len(ws)
    has_bias = biases is not None
    in_specs = [pl.BlockSpec((tm, k), lambda j, i: (i, 0))]
    for off in col_offsets:
        in_specs.append(pl.BlockSpec((k, tn), lambda j, i, off=off: (0, j + off)))
    args = [x] + list(ws)
    if has_bias:
        for off in col_offsets:
            in_specs.append(pl.BlockSpec((1, tn), lambda j, i, off=off: (0, j + off)))
        args += [b.reshape(1, -1) for b in biases]
    return pl.pallas_call(
        functools.partial(_mm_kernel, n_w=n_w, has_bias=has_bias, epilogue=epilogue),
        out_shape=jax.ShapeDtypeStruct((m, n_out), out_dtype),
        grid=(n_out // tn, m // tm),
        in_specs=in_specs,
        out_specs=pl.BlockSpec((tm, tn), lambda j, i: (i, j)),
        scratch_shapes=[pltpu.VMEM((k, tn), BF16) for _ in range(n_w)],
        compiler_params=_cparams(2),
        name=name,
    )(*args)


SLABS = 8
ROW_TILE = 256


def _get_piece(ref, mode, j, rows, d):
    if mode == "slab":
        return ref[:, j * d:(j + 1) * d]
    return ref[j * rows:(j + 1) * rows, :]


def _put_piece(ref, mode, j, rows, d, val):
    if mode == "slab":
        ref[:, j * d:(j + 1) * d] = val.astype(ref.dtype)
    else:
        ref[j * rows:(j + 1) * rows, :] = val.astype(ref.dtype)


def _resnorm_kernel(*refs, has_y, write_x, x_mode, y_mode, xo_mode, v_mode, modulate, rows, d):
    it = iter(refs)
    x_ref = next(it)
    y_ref = next(it) if has_y else None
    g_ref = next(it) if has_y else None
    nw_ref = next(it)
    sh_ref = next(it) if modulate else None
    sc_ref = next(it) if modulate else None
    xo_ref = next(it) if write_x else None
    v_ref = next(it)
    nw = nw_ref[...]
    for j in range(SLABS):
        x = _get_piece(x_ref, x_mode, j, rows, d)
        if has_y:
            y = _get_piece(y_ref, y_mode, j, rows, d).astype(F32)
            x = x + g_ref[0] * y
            if write_x:
                _put_piece(xo_ref, xo_mode, j, rows, d, x)
        v = x * lax.rsqrt(jnp.mean(x * x, axis=-1, keepdims=True) + EPS) * nw
        if modulate:
            v = v * (1.0 + sc_ref[0]) + sh_ref[0]
        _put_piece(v_ref, v_mode, j, rows, d, v)


def _resnorm(x, y, gate, norm_w, shift, scale, *, n_batch, seq, x_mode="row",
             y_mode="row", xo_mode="row", v_mode="row", v_dtype=BF16, write_x=True,
             x_tile0=0, y_tile0=0, vec_index=None, name="resnorm"):
    d = x.shape[1]
    n = n_batch * seq
    if vec_index is None:
        vec_index = lambda b, t: b
    has_y = y is not None
    write_x = write_x and has_y
    modulate = shift is not None
    grows = seq // GRID_W
    tiles_per_seq = seq // ROW_TILE
    slab_used = "slab" in (x_mode, y_mode, xo_mode, v_mode)
    rows = grows if slab_used else ROW_TILE // SLABS
    if slab_used:
        assert grows * SLABS == ROW_TILE

    def spec(mode, tile0=0):
        if mode == "slab":
            return pl.BlockSpec((grows, SLABS * d), lambda b, t: (b, t))
        return pl.BlockSpec((ROW_TILE, d), lambda b, t: (b * tiles_per_seq + t + tile0, 0))

    def view(a, mode):
        return a.reshape(a.shape[0] // GRID_W, GRID_W * d) if mode == "slab" else a

    vec = pl.BlockSpec((1, 1, d), lambda b, t: (vec_index(b, t), 0, 0))
    in_specs = [spec(x_mode, x_tile0)]
    args = [view(x, x_mode)]
    if has_y:
        in_specs += [spec(y_mode, y_tile0), vec]
        args += [view(y, y_mode), gate]
    in_specs.append(pl.BlockSpec((1, d), lambda b, t: (0, 0)))
    args.append(norm_w.reshape(1, d))
    if modulate:
        in_specs += [vec, vec]
        args += [shift, scale]
    out_shape, out_specs = [], []
    def out_struct(mode, dtype):
        shape = (n // GRID_W, GRID_W * d) if mode == "slab" else (n, d)
        return jax.ShapeDtypeStruct(shape, dtype)

    if write_x:
        out_shape.append(out_struct(xo_mode, F32))
        out_specs.append(spec(xo_mode))
    out_shape.append(out_struct(v_mode, v_dtype))
    out_specs.append(spec(v_mode))
    outs = pl.pallas_call(
        functools.partial(_resnorm_kernel, has_y=has_y, write_x=write_x, x_mode=x_mode, y_mode=y_mode,
                          xo_mode=xo_mode, v_mode=v_mode, modulate=modulate, rows=rows, d=d),
        out_shape=out_shape,
        grid=(n_batch, tiles_per_seq),
        in_specs=in_specs,
        out_specs=out_specs,
        compiler_params=_cparams(2),
        name=name,
    )(*args)
    outs = [o.reshape(n, d) for o in outs]
    return (outs[0], outs[1]) if write_x else (None, outs[0])


S5_T = 128
S5_CB = 128
S5_NS = (S5_CB // S5_GROUP) * S5_STATE


def _reverse_rows(v_bf16):
    t = v_bf16.shape[0]
    r = lax.broadcasted_iota(jnp.int32, (t, t), 0)
    c = lax.broadcasted_iota(jnp.int32, (t, t), 1)
    flip = jnp.where(r + c == t - 1, 1.0, 0.0).astype(BF16)
    return _dot(flip, v_bf16)


def _s5_prep_kernel(xf_ref, xb_ref, nw_ref, sh_ref, sc_ref, o_ref, *, nb, d):
    nw = nw_ref[...]
    for b in range(nb):
        for rev, x_ref in ((False, xf_ref), (True, xb_ref)):
            x = x_ref[b]
            v = x * lax.rsqrt(jnp.mean(x * x, axis=-1, keepdims=True) + EPS) * nw
            v = (v * (1.0 + sc_ref[b]) + sh_ref[b]).astype(BF16)
            if rev:
                v = _reverse_rows(v).astype(BF16)
            k = b + (nb if rev else 0)
            o_ref[:, k * d:(k + 1) * d] = v


def _s5_prep(x3, norm_w, shift, scale):
    nb, s, d = x3.shape
    nt = s // S5_T
    return pl.pallas_call(
        functools.partial(_s5_prep_kernel, nb=nb, d=d),
        out_shape=jax.ShapeDtypeStruct((s, 2 * nb * d), BF16),
        grid=(nt,),
        in_specs=[pl.BlockSpec((nb, S5_T, d), lambda t: (0, t, 0)),
                  pl.BlockSpec((nb, S5_T, d), lambda t: (0, nt - 1 - t, 0)),
                  pl.BlockSpec((1, d), lambda t: (0, 0)),
                  pl.BlockSpec((nb, 1, d), lambda t: (0, 0, 0)),
                  pl.BlockSpec((nb, 1, d), lambda t: (0, 0, 0))],
        out_specs=pl.BlockSpec((S5_T, 2 * nb * d), lambda t: (t, 0)),
        compiler_params=_cparams(1),
        name="s5_prep",
    )(x3, x3, norm_w.reshape(1, d), shift, scale)


S5_BPS = 2


def _s5_scan_kernel(u_ref, bc_ref, are_ref, aim_ref, cc_ref, s0_ref, y_ref, sf_ref,
                    bu_ref, st_ref, *, n_tiles):
    i = pl.program_id(1)
    ns = S5_NS

    @pl.when(i == 0)
    def _():
        st_ref[...] = s0_ref[...]

    row = lax.broadcasted_iota(jnp.int32, (u_ref.shape[0], S5_CB), 0)
    is_fwd = jnp.bitwise_and(row, SUBLANES - 1) < (SUBLANES // 2)
    for blk in range(S5_BPS):
        u = u_ref[:, blk * S5_CB:(blk + 1) * S5_CB].astype(F32)
        lhs = jnp.concatenate([jnp.where(is_fwd, u, 0.0), jnp.where(is_fwd, 0.0, u)], axis=1)
        bu_ref[blk] = _dot(lhs.astype(BF16), bc_ref[blk])
    coef = [(are_ref[blk], aim_ref[blk]) for blk in range(S5_BPS)]

    def step(t, carry):
        r = pl.multiple_of(t * SUBLANES, SUBLANES)
        out = []
        for blk in range(S5_BPS):
            a_re, a_im = coef[blk]
            s_re, s_im = carry[2 * blk], carry[2 * blk + 1]
            n_re = a_re * s_re - a_im * s_im + bu_ref[blk, pl.ds(r, SUBLANES), 0:ns]
            n_im = a_re * s_im + a_im * s_re + bu_ref[blk, pl.ds(r, SUBLANES), ns:2 * ns]
            bu_ref[blk, pl.ds(r, SUBLANES), 0:ns] = n_re
            bu_ref[blk, pl.ds(r, SUBLANES), ns:2 * ns] = n_im
            out += [n_re, n_im]
        return tuple(out)

    init = []
    for blk in range(S5_BPS):
        init += [st_ref[blk, :, 0:ns], st_ref[blk, :, ns:2 * ns]]
    fin = lax.fori_loop(0, S5_T, step, tuple(init))
    for blk in range(S5_BPS):
        st_ref[blk, :, 0:ns] = fin[2 * blk]
        st_ref[blk, :, ns:2 * ns] = fin[2 * blk + 1]
        y2 = _dot(bu_ref[blk].astype(BF16), cc_ref[blk])
        y_ref[:, blk * S5_CB:(blk + 1) * S5_CB] = jnp.where(is_fwd, y2[:, 0:S5_CB],
                                                            y2[:, S5_CB:2 * S5_CB])

    @pl.when(i == n_tiles - 1)
    def _():
        sf_ref[...] = st_ref[...]


def _s5_scan(u2r, bcat, a_re, a_im, ccat, s0):
    rows, d = u2r.shape
    nblk = d // S5_CB
    tr = S5_T * SUBLANES
    n_tiles = rows // tr
    ns2 = 2 * S5_NS
    w = S5_BPS * S5_CB
    per_block = lambda j, i: (j, 0, 0)
    return pl.pallas_call(
        functools.partial(_s5_scan_kernel, n_tiles=n_tiles),
        out_shape=(jax.ShapeDtypeStruct((rows, d), F32),
                   jax.ShapeDtypeStruct((nblk, SUBLANES, ns2), F32)),
        grid=(nblk // S5_BPS, n_tiles),
        in_specs=[pl.BlockSpec((tr, w), lambda j, i: (i, j)),
                  pl.BlockSpec((S5_BPS, 2 * S5_CB, ns2), per_block),
                  pl.BlockSpec((S5_BPS, SUBLANES, S5_NS), per_block),
                  pl.BlockSpec((S5_BPS, SUBLANES, S5_NS), per_block),
                  pl.BlockSpec((S5_BPS, ns2, 2 * S5_CB), per_block),
                  pl.BlockSpec((S5_BPS, SUBLANES, ns2), per_block)],
        out_specs=(pl.BlockSpec((tr, w), lambda j, i: (i, j)),
                   pl.BlockSpec((S5_BPS, SUBLANES, ns2), per_block)),
        scratch_shapes=[pltpu.VMEM((S5_BPS, tr, ns2), F32),
                        pltpu.VMEM((S5_BPS, SUBLANES, ns2), F32)],
        compiler_params=_cparams(2),
        name="s5_scan",
    )(u2r, bcat, a_re, a_im, ccat, s0)


def _s5_out_kernel(yf_ref, yb_ref, u_ref, skip_ref, o_ref):
    yb = yb_ref[...]
    hi = yb.astype(BF16)
    lo = (yb - hi.astype(F32)).astype(BF16)
    y = (skip_ref[...] * u_ref[...].astype(F32) + yf_ref[...]
         + _reverse_rows(hi) + _reverse_rows(lo))
    o_ref[...] = jax.nn.gelu(y).astype(o_ref.dtype)


def _s5_out(y2, u2, skip, nb, d):
    s = y2.shape[0]
    nt = s // S5_T
    return pl.pallas_call(
        _s5_out_kernel,
        out_shape=jax.ShapeDtypeStruct((nb * s, d), BF16),
        grid=(nb, nt),
        in_specs=[pl.BlockSpec((S5_T, d), lambda b, t: (t, b)),
                  pl.BlockSpec((S5_T, d), lambda b, t: (nt - 1 - t, nb + b)),
                  pl.BlockSpec((S5_T, d), lambda b, t: (t, b)),
                  pl.BlockSpec((1, d), lambda b, t: (0, 0))],
        out_specs=pl.BlockSpec((S5_T, d), lambda b, t: (b * nt + t, 0)),
        compiler_params=_cparams(2),
        name="s5_out",
    )(y2, y2, u2, skip.reshape(1, d))


def _s5_pack_params(lam_re, lam_im, log_step, b_re, b_im, c_re, c_im, nb):
    f32 = F32
    g = lam_re.shape[1]
    gpb = S5_CB // S5_GROUP
    nblk = g // gpb
    eye = jnp.eye(gpb, dtype=f32)
    a_re_rows, a_im_rows, b_parts, c_parts = [], [], [], []
    for dr in range(2):
        lr, li = lam_re[dr].astype(f32), lam_im[dr].astype(f32)
        br, bi = b_re[dr].astype(f32), b_im[dr].astype(f32)
        dt = jnp.exp(log_step[dr].astype(f32))[:, None]
        mag = jnp.exp(lr * dt)
        abar_re = mag * jnp.cos(li * dt)
        abar_im = mag * jnp.sin(li * dt)
        num_re = abar_re - 1.0
        num_im = abar_im
        den = lr * lr + li * li
        f_re = (num_re * lr + num_im * li) / den
        f_im = (num_im * lr - num_re * li) / den
        bbar_re = f_re[..., None] * br - f_im[..., None] * bi
        bbar_im = f_re[..., None] * bi + f_im[..., None] * br
        a_re_rows.append(jnp.broadcast_to(abar_re.reshape(nblk, 1, S5_NS), (nblk, nb, S5_NS)))
        a_im_rows.append(jnp.broadcast_to(abar_im.reshape(nblk, 1, S5_NS), (nblk, nb, S5_NS)))

        def blockdiag_in(bb):
            b4 = bb.reshape(nblk, gpb, S5_STATE, S5_GROUP)
            return jnp.einsum('jgpk,gh->jgkhp', b4, eye).reshape(nblk, S5_CB, S5_NS)

        def blockdiag_out(cc):
            c4 = cc.astype(f32).reshape(nblk, gpb, S5_GROUP, S5_STATE)
            return jnp.einsum('jgkp,gh->jgphk', c4, eye).reshape(nblk, S5_NS, S5_CB)

        b_parts.append(jnp.concatenate([blockdiag_in(bbar_re), blockdiag_in(bbar_im)], axis=2))
        c_parts.append(jnp.concatenate([blockdiag_out(c_re[dr]), -blockdiag_out(c_im[dr])], axis=1))
    a_re = jnp.concatenate(a_re_rows, axis=1)
    a_im = jnp.concatenate(a_im_rows, axis=1)
    bcat = jnp.concatenate(b_parts, axis=1).astype(BF16)
    ccat = jnp.concatenate(c_parts, axis=2).astype(BF16)
    return bcat, a_re, a_im, ccat


def _conv_kernel(x_ref, w_ref, b_ref, o_ref, pad_ref, *, seq):
    halo = SUBLANES
    zeros = jnp.zeros((halo, pad_ref.shape[1]), F32)
    pad_ref[0:halo, :] = zeros
    pad_ref[halo + seq:2 * halo + seq, :] = zeros
    pad_ref[halo:halo + seq, :] = x_ref[...]
    acc = jnp.zeros((seq, pad_ref.shape[1]), F32) + b_ref[...]
    for k in range(SSD_CONV):
        off = halo + k - SSD_CONV // 2
        acc = acc + w_ref[k:k + 1, :] * pad_ref[off:off + seq, :]
    o_ref[...] = _silu(acc).astype(o_ref.dtype)


def _conv_silu(xbc, conv_w, conv_b, seq, row_block_offset, n_seq, out_dtype=BF16):
    _, c = xbc.shape
    tc = 512
    return pl.pallas_call(
        functools.partial(_conv_kernel, seq=seq),
        out_shape=jax.ShapeDtypeStruct((n_seq * seq, c), out_dtype),
        grid=(n_seq, c // tc),
        in_specs=[pl.BlockSpec((seq, tc), lambda b, j: (b + row_block_offset, j)),
                  pl.BlockSpec((SSD_CONV, tc), lambda b, j: (0, j)),
                  pl.BlockSpec((1, tc), lambda b, j: (0, j))],
        out_specs=pl.BlockSpec((seq, tc), lambda b, j: (b, j)),
        scratch_shapes=[pltpu.VMEM((seq + 2 * SUBLANES, tc), F32)],
        compiler_params=_cparams(2),
        name="ssd_conv",
    )(xbc, conv_w, conv_b.reshape(1, c))


HPG = 8
GCH = HPG * SSD_HEADDIM
DTC = 2 * HPG


def _ssd_kernel(xl_ref, btl_ref, cl_ref, dtl_ref, dttl_ref,
                xc_ref, btc_ref, dtc_ref, dttc_ref,
                z_ref, acol_ref, arow_ref, skip_ref, nw_ref, o_ref,
                htf_ref, htb_ref, yf_ref, yb_ref, *, n_lat, n_ctx):
    q = SSD_CHUNK
    r_i = lax.broadcasted_iota(jnp.int32, (q, q), 0)
    c_i = lax.broadcasted_iota(jnp.int32, (q, q), 1)
    tril = jnp.where(c_i <= r_i, 1.0, 0.0).astype(BF16)
    triu = jnp.where(r_i <= c_i, 1.0, 0.0).astype(BF16)
    lower = c_i <= r_i
    upper = c_i >= r_i
    lane = lax.broadcasted_iota(jnp.int32, (q, 2 * SSD_HEADDIM), 1)
    left = lane < SSD_HEADDIM
    e_r = lax.broadcasted_iota(jnp.int32, (DTC, GCH), 0)
    e_c = lax.broadcasted_iota(jnp.int32, (DTC, GCH), 1)
    a_row = arow_ref[0]
    a_col = acol_ref[0]

    def chunk(x_ref, bt_ref, c_ref, dt_ref, dtt_ref, r0, dr, ht_ref, y_ref):
        head_of_ch = lax.shift_right_logical(e_c, int(math.log2(SSD_HEADDIM)))
        expand = jnp.where(head_of_ch + dr * HPG == e_r, 1.0, 0.0).astype(BF16)
        x = x_ref[pl.ds(r0, q), :].astype(F32)
        bt = bt_ref[0, :, pl.ds(r0, q)]
        dt = dt_ref[0, pl.ds(r0, q), :]
        dtt = dtt_ref[0, :, pl.ds(r0, q)]
        la = dt * a_row
        lat = dtt * a_col
        cs = _dot_exact_lhs(tril, la)
        cst = _dot_exact_rhs(lat, triu)
        total = cs[q - 1:q, :]
        if dr == 1:
            rk, rkt = cs - la, cst - lat
            e_off, e_state = jnp.exp(total - rk), jnp.exp(rk)
        else:
            rk, rkt = cs, cst
            e_off, e_state = jnp.exp(rk), jnp.exp(total - rk)
        tot8 = jnp.broadcast_to(jnp.exp(total), (SUBLANES, DTC))
        pieces3 = [p.astype(F32) for p in _split3(tot8)]
        wo = _dot(jnp.concatenate([e_off] + pieces3, axis=0).astype(BF16), expand)
        w_off = wo[0:q, :]
        e_tot = (wo[q:q + 1, :] + wo[q + SUBLANES:q + SUBLANES + 1, :]
                 + wo[q + 2 * SUBLANES:q + 2 * SUBLANES + 1, :])
        w_state = _dot(e_state.astype(BF16), expand)
        dtx = _dot(dt.astype(BF16), expand)
        xd = x * dtx
        h_old = ht_ref[...]
        ht_ref[...] = h_old * e_tot + _dot(bt, (xd * w_state).astype(BF16))
        if y_ref is None:
            return
        cm = c_ref[pl.ds(r0, q), :]
        cb = _dot(cm, bt)
        xdb = xd.astype(BF16)
        pieces = []
        for pair in range(HPG // 2):
            ms = []
            for hh in (2 * pair, 2 * pair + 1):
                col = dr * HPG + hh
                colv = jnp.broadcast_to(rk[:, col:col + 1], (q, q))
                rowv = rkt[col:col + 1, :]
                if dr == 0:
                    seg = jnp.where(lower, colv - rowv, -1e30)
                else:
                    seg = jnp.where(upper, rowv - colv, -1e30)
                ms.append((cb * jnp.exp(seg)).astype(BF16))
            xp = xdb[:, pair * 2 * SSD_HEADDIM:(pair + 1) * 2 * SSD_HEADDIM]
            zero = jnp.zeros_like(xp)
            rhs = jnp.concatenate([jnp.where(left, xp, zero), jnp.where(left, zero, xp)], axis=0)
            pieces.append(_dot(jnp.concatenate(ms, axis=1), rhs))
        y_ref[pl.ds(r0, q), :] = (jnp.concatenate(pieces, axis=1)
                                  + _dot(cm, h_old.astype(BF16)) * w_off)

    htf_ref[...] = jnp.zeros_like(htf_ref)
    htb_ref[...] = jnp.zeros_like(htb_ref)

    def ctx_body(k, carry):
        rf = pl.multiple_of(k * q, q)
        rb = pl.multiple_of((n_ctx - 1 - k) * q, q)
        chunk(xc_ref, btc_ref, None, dtc_ref, dttc_ref, rf, 0, htf_ref, None)
        chunk(xc_ref, btc_ref, None, dtc_ref, dttc_ref, rb, 1, htb_ref, None)
        return carry

    lax.fori_loop(0, n_ctx, ctx_body, 0)

    def lat_body(k, carry):
        rf = pl.multiple_of(k * q, q)
        rb = pl.multiple_of((n_lat - 1 - k) * q, q)
        chunk(xl_ref, btl_ref, cl_ref, dtl_ref, dttl_ref, rf, 0, htf_ref, yf_ref)
        chunk(xl_ref, btl_ref, cl_ref, dtl_ref, dttl_ref, rb, 1, htb_ref, yb_ref)
        return carry

    lax.fori_loop(0, n_lat, lat_body, 0)

    def finish(k, carry):
        r0 = pl.multiple_of(k * q, q)
        x = xl_ref[pl.ds(r0, q), :].astype(F32)
        y = yf_ref[pl.ds(r0, q), :] + yb_ref[pl.ds(r0, q), :] + skip_ref[...] * x
        v = y * _silu(z_ref[pl.ds(r0, q), :].astype(F32))
        v = v * lax.rsqrt(jnp.mean(v * v, axis=-1, keepdims=True) + EPS) * nw_ref[...]
        o_ref[pl.ds(r0, q), :] = v.astype(o_ref.dtype)
        return carry

    lax.fori_loop(0, n_lat, finish, 0)


def _ssd_scan(xc_l, xc_c, bt_l, bt_c, z, dtg, dtgt, a_col, a_row, skip, norm_w, nb, seq, ctx_len):
    n_lat, n_ctx = seq // SSD_CHUNK, ctx_len // SSD_CHUNK
    inner = SSD_GROUPS * GCH
    coff = inner // SSD_STATE + SSD_GROUPS
    cb0 = nb * seq // ctx_len
    in_specs = [
        pl.BlockSpec((seq, GCH), lambda b, g: (b, g)),
        pl.BlockSpec((1, SSD_STATE, seq), lambda b, g: (g, 0, b)),
        pl.BlockSpec((seq, SSD_STATE), lambda b, g: (b, coff + g)),
        pl.BlockSpec((1, seq, DTC), lambda b, g: (g, b, 0)),
        pl.BlockSpec((1, DTC, seq), lambda b, g: (g, 0, b)),
        pl.BlockSpec((ctx_len, GCH), lambda b, g: (b, g)),
        pl.BlockSpec((1, SSD_STATE, ctx_len), lambda b, g: (g, 0, b)),
        pl.BlockSpec((1, ctx_len, DTC), lambda b, g: (g, cb0 + b, 0)),
        pl.BlockSpec((1, DTC, ctx_len), lambda b, g: (g, 0, cb0 + b)),
        pl.BlockSpec((seq, GCH), lambda b, g: (b, g)),
        pl.BlockSpec((1, DTC, 1), lambda b, g: (g, 0, 0)),
        pl.BlockSpec((1, 1, DTC), lambda b, g: (g, 0, 0)),
        pl.BlockSpec((1, GCH), lambda b, g: (0, g)),
        pl.BlockSpec((1, GCH), lambda b, g: (0, g)),
    ]
    return pl.pallas_call(
        functools.partial(_ssd_kernel, n_lat=n_lat, n_ctx=n_ctx),
        out_shape=jax.ShapeDtypeStruct((nb * seq, inner), BF16),
        grid=(nb, SSD_GROUPS),
        in_specs=in_specs,
        out_specs=pl.BlockSpec((seq, GCH), lambda b, g: (b, g)),
        scratch_shapes=[pltpu.VMEM((SSD_STATE, GCH), F32), pltpu.VMEM((SSD_STATE, GCH), F32),
                        pltpu.VMEM((seq, GCH), F32), pltpu.VMEM((seq, GCH), F32)],
        compiler_params=_cparams(2),
        name="ssd_scan",
    )(xc_l, bt_l, xc_l, dtg, dtgt, xc_c, bt_c, dtg, dtgt, z, a_col, a_row, skip, norm_w)


def _expert_kernel(be_ref, nu_ref, first_ref, slot_ref, next_ref, x_ref, wg_hbm, wu_hbm, wd_hbm,
                   o_ref, stage_g, stage_u, stage_d, wgb, wub, wdb, sem, *, layer):
    i = pl.program_id(0)

    def weight_copies(e, s):
        return (pltpu.make_async_copy(wg_hbm.at[layer, e], stage_g.at[s], sem.at[s, 0]),
                pltpu.make_async_copy(wu_hbm.at[layer, e], stage_u.at[s], sem.at[s, 1]),
                pltpu.make_async_copy(wd_hbm.at[layer, e], stage_d.at[s], sem.at[s, 2]))

    @pl.when(i == 0)
    def _():
        for c in weight_copies(be_ref[0], 0):
            c.start()

    @pl.when(jnp.logical_and(i < nu_ref[0], first_ref[i] == 1))
    def _():
        s = slot_ref[i]

        @pl.when(next_ref[i] >= 0)
        def _():
            for c in weight_copies(next_ref[i], 1 - s):
                c.start()

        for c in weight_copies(be_ref[i], s):
            c.wait()
        wgb[...] = stage_g[s].astype(BF16)
        wub[...] = stage_u[s].astype(BF16)
        wdb[...] = stage_d[s].astype(BF16)

    @pl.when(i < nu_ref[0])
    def _():
        lo, hi = _unpack_rows(x_ref[...])
        x = jnp.concatenate([lo.astype(BF16), hi.astype(BF16)], axis=1)
        h = _silu(_dot(x, wgb[...])) * _dot(x, wub[...])
        o_ref[...] = _pack_rows(_dot(h.astype(BF16), wdb[...]))

    @pl.when(i >= nu_ref[0])
    def _():
        o_ref[...] = jnp.zeros_like(o_ref)


def _experts(x_sorted, block_e, n_used, first, slot, next_e, w_gate, w_up, w_down, layer):
    cap, d = x_sorted.shape
    ff = w_gate.shape[3]
    n_blocks = cap // EXPERT_BLOCK

    def row_block(i, be, nu, fi, sl, nx):
        return (jnp.minimum(i, nu[0] - 1), 0)

    hbm = pl.BlockSpec(memory_space=pl.ANY)
    grid_spec = pltpu.PrefetchScalarGridSpec(
        num_scalar_prefetch=5,
        grid=(n_blocks,),
        in_specs=[pl.BlockSpec((EXPERT_BLOCK, d), row_block), hbm, hbm, hbm],
        out_specs=pl.BlockSpec((EXPERT_BLOCK, d), lambda i, be, nu, fi, sl, nx: (i, 0)),
        scratch_shapes=[pltpu.VMEM((2, 2 * d, ff), F32), pltpu.VMEM((2, 2 * d, ff), F32),
                        pltpu.VMEM((2, ff, 2 * d), F32),
                        pltpu.VMEM((2 * d, ff), BF16), pltpu.VMEM((2 * d, ff), BF16),
                        pltpu.VMEM((ff, 2 * d), BF16), pltpu.SemaphoreType.DMA((2, 3))])
    return pl.pallas_call(
        functools.partial(_expert_kernel, layer=layer),
        out_shape=jax.ShapeDtypeStruct((cap, d), jnp.uint32),
        grid_spec=grid_spec,
        compiler_params=_cparams(1),
        name="moe_experts",
    )(block_e, n_used, first, slot, next_e, x_sorted, w_gate, w_up, w_down)


DISPATCH_T = 256
COMBINE_T = 128


HIGH_HALF = 0xFFFF0000


def _pack_rows(a):
    half = a.shape[1] // 2
    lo = lax.bitcast_convert_type(a[:, :half].astype(BF16).astype(F32), jnp.uint32)
    hi = lax.bitcast_convert_type(a[:, half:].astype(BF16).astype(F32), jnp.uint32)
    return jnp.bitwise_or(jnp.bitwise_and(hi, jnp.uint32(HIGH_HALF)),
                          lax.shift_right_logical(lo, jnp.uint32(16)))


def _unpack_rows(w):
    lo = lax.bitcast_convert_type(lax.shift_left(w, jnp.uint32(16)), F32)
    hi = lax.bitcast_convert_type(jnp.bitwise_and(w, jnp.uint32(HIGH_HALF)), F32)
    return lo, hi


def _dispatch_kernel(pe_ref, pd_ref, dest_ref, v_ref, xs_ref, zero_buf, row_buf, sem):
    t_rows = v_ref.shape[0]
    row_buf[...] = _pack_rows(v_ref[...])

    @pl.when(pl.program_id(0) == 0)
    def _():
        zero_buf[...] = jnp.zeros_like(zero_buf)

        def fill(e, carry):
            @pl.when(pd_ref[e] > 0)
            def _():
                start = pl.multiple_of(pe_ref[e] - EXPERT_BLOCK, EXPERT_BLOCK)
                pltpu.make_async_copy(zero_buf, xs_ref.at[pl.ds(start, EXPERT_BLOCK)], sem).start()
            return carry

        def drain(e, carry):
            @pl.when(pd_ref[e] > 0)
            def _():
                pltpu.make_async_copy(zero_buf, xs_ref.at[pl.ds(0, EXPERT_BLOCK)], sem).wait()
            return carry

        lax.fori_loop(0, N_EXPERTS, fill, 0)
        lax.fori_loop(0, N_EXPERTS, drain, 0)

        n_blocks = xs_ref.shape[0] // EXPERT_BLOCK
        first_unused = pe_ref[N_EXPERTS - 1] // EXPERT_BLOCK

        def fill_tail(b, carry):
            start = pl.multiple_of(b * EXPERT_BLOCK, EXPERT_BLOCK)
            pltpu.make_async_copy(zero_buf, xs_ref.at[pl.ds(start, EXPERT_BLOCK)], sem).start()
            return carry

        def drain_tail(b, carry):
            pltpu.make_async_copy(zero_buf, xs_ref.at[pl.ds(0, EXPERT_BLOCK)], sem).wait()
            return carry

        lax.fori_loop(first_unused, n_blocks, fill_tail, 0)
        lax.fori_loop(first_unused, n_blocks, drain_tail, 0)

    def issue(t, carry):
        for k in range(TOP_K):
            pltpu.make_async_copy(row_buf.at[pl.ds(t, 1)],
                                  xs_ref.at[pl.ds(dest_ref[k, t], 1)], sem).start()
        return carry

    lax.fori_loop(0, t_rows, issue, 0)
    for k in range(TOP_K):
        pltpu.make_async_copy(row_buf, xs_ref.at[pl.ds(0, t_rows)], sem).wait()


def _dispatch(v, dest, pad_end, padded, cap):
    n, d = v.shape
    grid_spec = pltpu.PrefetchScalarGridSpec(
        num_scalar_prefetch=2,
        grid=(n // DISPATCH_T,),
        in_specs=[pl.BlockSpec((TOP_K, DISPATCH_T), lambda i, pe, pd: (0, i),
                               memory_space=pltpu.SMEM),
                  pl.BlockSpec((DISPATCH_T, d), lambda i, pe, pd: (i, 0))],
        out_specs=pl.BlockSpec(memory_space=pl.ANY),
        scratch_shapes=[pltpu.VMEM((EXPERT_BLOCK, d // 2), jnp.uint32),
                        pltpu.VMEM((DISPATCH_T, d // 2), jnp.uint32),
                        pltpu.SemaphoreType.DMA(())])
    return pl.pallas_call(
        _dispatch_kernel,
        out_shape=jax.ShapeDtypeStruct((cap, d // 2), jnp.uint32),
        grid_spec=grid_spec,
        compiler_params=_cparams(1),
        name="moe_dispatch",
    )(pad_end, padded, dest, v)


def _combine_kernel(dest_ref, y_ref, gate_ref, sh_ref, o_ref, buf, sem):
    t_rows = o_ref.shape[0]

    def issue(t, carry):
        for k in range(TOP_K):
            pltpu.make_async_copy(y_ref.at[pl.ds(dest_ref[k, t], 1)],
                                  buf.at[k, pl.ds(t, 1)], sem).start()
        return carry

    lax.fori_loop(0, t_rows, issue, 0)
    for k in range(TOP_K):
        pltpu.make_async_copy(y_ref.at[pl.ds(0, t_rows)], buf.at[k], sem).wait()
    half = o_ref.shape[1] // 2
    acc_lo = sh_ref[:, :half]
    acc_hi = sh_ref[:, half:]
    for k in range(TOP_K):
        lo, hi = _unpack_rows(buf[k])
        g = gate_ref[:, k:k + 1]
        acc_lo = acc_lo + g * lo
        acc_hi = acc_hi + g * hi
    o_ref[:, :half] = acc_lo
    o_ref[:, half:] = acc_hi


def _combine(y_sorted, dest, gate_t, shared):
    n, d = shared.shape
    return pl.pallas_call(
        _combine_kernel,
        out_shape=jax.ShapeDtypeStruct((n, d), F32),
        grid=(n // COMBINE_T,),
        in_specs=[pl.BlockSpec((TOP_K, COMBINE_T), lambda i: (0, i), memory_space=pltpu.SMEM),
                  pl.BlockSpec(memory_space=pl.ANY),
                  pl.BlockSpec((COMBINE_T, TOP_K), lambda i: (i, 0)),
                  pl.BlockSpec((COMBINE_T, d), lambda i: (i, 0))],
        out_specs=pl.BlockSpec((COMBINE_T, d), lambda i: (i, 0)),
        scratch_shapes=[pltpu.VMEM((TOP_K, COMBINE_T, d // 2), jnp.uint32),
                        pltpu.SemaphoreType.DMA(())],
        compiler_params=_cparams(1),
        name="moe_combine",
    )(dest, y_sorted, gate_t, shared)


ROUTE_T = 512
GROUP_SIZE = N_EXPERTS // N_EXPERT_GROUPS


def _route_kernel(v_ref, rwh_ref, rwl_ref, bias_ref, eid_ref, rank_ref, gate_ref, cnt_ref,
                  carry_ref, *, n_tiles):
    i = pl.program_id(0)
    t = v_ref.shape[0]
    ng, gs = N_EXPERT_GROUPS, GROUP_SIZE
    neg = -jnp.inf

    @pl.when(i == 0)
    def _():
        carry_ref[...] = jnp.zeros_like(carry_ref)

    v = v_ref[...].astype(BF16)
    nt_dims = (((1,), (1,)), ((), ()))
    logits = (lax.dot_general(rwh_ref[...], v, nt_dims, preferred_element_type=F32)
              + lax.dot_general(rwl_ref[...], v, nt_dims, preferred_element_type=F32))
    scores = jax.nn.sigmoid(logits)
    biased = scores + bias_ref[:, 0:1]
    x3 = biased.reshape(ng, gs, t)
    s3 = scores.reshape(ng, gs, t)
    mi = lax.broadcasted_iota(jnp.int32, (ng, gs, t), 1).astype(F32)
    fi = lax.broadcasted_iota(jnp.int32, (ng, gs, t), 0).astype(F32) * gs + mi
    gi = lax.broadcasted_iota(jnp.int32, (ng, 1, t), 0).astype(F32)

    m1 = jnp.max(x3, axis=1, keepdims=True)
    i1 = jnp.min(jnp.where(x3 == m1, mi, float(gs)), axis=1, keepdims=True)
    m2 = jnp.max(jnp.where(mi == i1, neg, x3), axis=1, keepdims=True)
    cur = m1 + m2
    gmask = jnp.zeros((ng, 1, t), F32)
    for _ in range(TOPK_GROUPS):
        gm = jnp.max(cur, axis=0, keepdims=True)
        idx = jnp.min(jnp.where(cur == gm, gi, float(ng)), axis=0, keepdims=True)
        hit = gi == idx
        gmask = jnp.where(hit, 1.0, gmask)
        cur = jnp.where(hit, neg, cur)

    cand = jnp.where(gmask > 0.0, x3, neg)
    sel = jnp.zeros((ng, gs, t), F32)
    eids = []
    for _ in range(TOP_K):
        m = jnp.max(jnp.max(cand, axis=1, keepdims=True), axis=0, keepdims=True)
        idx = jnp.where(cand == m, fi, float(N_EXPERTS))
        idx = jnp.min(jnp.min(idx, axis=1, keepdims=True), axis=0, keepdims=True)
        hit = fi == idx
        sel = jnp.where(hit, 1.0, sel)
        cand = jnp.where(hit, neg, cand)
        eids.append(idx)

    selr = sel.reshape(N_EXPERTS, t)
    r_i = lax.broadcasted_iota(jnp.int32, (t, t), 0)
    c_i = lax.broadcasted_iota(jnp.int32, (t, t), 1)
    before = jnp.where(r_i < c_i, 1.0, 0.0).astype(BF16)
    rank = _dot(selr.astype(BF16), before) + carry_ref[:, 0:1]
    carry_ref[...] = carry_ref[...] + jnp.sum(selr, axis=1, keepdims=True)
    rank3 = rank.reshape(ng, gs, t)

    gsel = sel * s3
    denom = jnp.sum(jnp.sum(gsel, axis=1, keepdims=True), axis=0, keepdims=True)
    gate3 = gsel / denom * ROUTED_SCALE

    def pick(a3, hit):
        return jnp.sum(jnp.sum(jnp.where(hit, a3, 0.0), axis=1, keepdims=True), axis=0,
                       keepdims=True).reshape(1, t)

    for k in range(TOP_K):
        hit = fi == eids[k]
        eid_ref[k:k + 1, :] = eids[k].reshape(1, t).astype(jnp.int32)
        rank_ref[k:k + 1, :] = pick(rank3, hit).astype(jnp.int32)
        gate_ref[k:k + 1, :] = pick(gate3, hit)

    @pl.when(i == n_tiles - 1)
    def _():
        cnt_ref[...] = carry_ref[...]


def _route(v, router_w, router_bias):
    n, d = v.shape
    n_tiles = n // ROUTE_T
    rwt = router_w.astype(F32).T
    rwh = rwt.astype(BF16)
    rwl = (rwt - rwh.astype(F32)).astype(BF16)
    bias = jnp.broadcast_to(router_bias.astype(F32)[:, None], (N_EXPERTS, LANES))
    slot = pl.BlockSpec((TOP_K, ROUTE_T), lambda i: (0, i))
    full = pl.BlockSpec((N_EXPERTS, d), lambda i: (0, 0))
    return pl.pallas_call(
        functools.partial(_route_kernel, n_tiles=n_tiles),
        out_shape=(jax.ShapeDtypeStruct((TOP_K, n), jnp.int32),
                   jax.ShapeDtypeStruct((TOP_K, n), jnp.int32),
                   jax.ShapeDtypeStruct((TOP_K, n), F32),
                   jax.ShapeDtypeStruct((N_EXPERTS, LANES), F32)),
        grid=(n_tiles,),
        in_specs=[pl.BlockSpec((ROUTE_T, d), lambda i: (i, 0)), full, full,
                  pl.BlockSpec((N_EXPERTS, LANES), lambda i: (0, 0))],
        out_specs=(slot, slot, slot, pl.BlockSpec((N_EXPERTS, LANES), lambda i: (0, 0))),
        scratch_shapes=[pltpu.VMEM((N_EXPERTS, LANES), F32)],
        compiler_params=_cparams(1),
        name="moe_route",
    )(v, rwh, rwl, bias)


def _slot_rows_kernel(ps_ref, eid_ref, rank_ref, o_ref):
    eid = eid_ref[...]
    acc = rank_ref[...]
    for e in range(N_EXPERTS):
        acc = acc + jnp.where(eid == e, ps_ref[e], 0)
    o_ref[...] = acc


def _slot_rows(eid, rank, pad_start):
    k, n = eid.shape
    whole = lambda i, ps: (0, 0)
    grid_spec = pltpu.PrefetchScalarGridSpec(
        num_scalar_prefetch=1, grid=(1,),
        in_specs=[pl.BlockSpec((k, n), whole), pl.BlockSpec((k, n), whole)],
        out_specs=pl.BlockSpec((k, n), whole))
    return pl.pallas_call(
        _slot_rows_kernel,
        out_shape=jax.ShapeDtypeStruct((k, n), jnp.int32),
        grid_spec=grid_spec,
        compiler_params=_cparams(1),
        name="moe_slot_rows",
    )(pad_start, eid, rank)


def _moe(v, layer, router_w, router_bias, w_gate, w_up, w_down, sw_gate, sw_up, sw_down):
    n, d = v.shape
    eid, rank, gate, cnt = _route(v, router_w, router_bias)
    nk = n * TOP_K
    cap = -(-nk // EXPERT_BLOCK) * EXPERT_BLOCK + N_EXPERTS * EXPERT_BLOCK
    n_blocks = cap // EXPERT_BLOCK
    counts = cnt[:, 0].astype(jnp.int32)
    padded = (counts + EXPERT_BLOCK - 1) // EXPERT_BLOCK * EXPERT_BLOCK
    pad_end = jnp.cumsum(padded)
    pad_start = pad_end - padded
    dest = _slot_rows(eid, rank, pad_start.astype(jnp.int32))
    block_start = jnp.arange(n_blocks, dtype=jnp.int32) * EXPERT_BLOCK
    block_e = jnp.minimum(jnp.sum((pad_end[None, :] <= block_start[:, None]).astype(jnp.int32), axis=1),
                          N_EXPERTS - 1).astype(jnp.int32)
    n_used = (pad_end[-1] // EXPERT_BLOCK).astype(jnp.int32).reshape(1)
    block_id = jnp.arange(n_blocks, dtype=jnp.int32)
    first = ((pad_start[block_e] == block_start) & (block_id < n_used[0])).astype(jnp.int32)
    slot = ((jnp.cumsum(first) - 1) % 2).astype(jnp.int32)
    experts = jnp.arange(N_EXPERTS, dtype=jnp.int32)
    later = (experts[None, :] > experts[:, None]) & (padded[None, :] > 0)
    next_of = jnp.min(jnp.where(later, experts[None, :], N_EXPERTS), axis=1)
    next_of = jnp.where(next_of == N_EXPERTS, -1, next_of).astype(jnp.int32)
    next_e = next_of[block_e]
    x_sorted = _dispatch(v, dest, pad_end.astype(jnp.int32), padded.astype(jnp.int32), cap)
    y_sorted = _experts(x_sorted, block_e, n_used, first, slot, next_e, w_gate, w_up, w_down, layer)
    hs = _matmul(v, [sw_gate, sw_up], [0, 0], sw_gate.shape[1], tn=sw_gate.shape[1], tm=512,
                 epilogue="swiglu", out_dtype=BF16, name="shared_up")
    shared = _matmul(hs, [sw_down], [0], d, tn=1024, tm=512, name="shared_down")
    return _combine(y_sorted, dest, gate.T, shared)


def kernel(x, c, ctx, c_ctx, ada_w, ada_b, norm1_w, norm2_w, s5_lambda_re, s5_lambda_im, s5_log_step, s5_b_re, s5_b_im, s5_c_re, s5_c_im, s5_d, s5_glu_w, s5_glu_b, ssd_in_w, ssd_conv_w, ssd_conv_b, ssd_dt_bias, ssd_a_log, ssd_d, ssd_norm_w, ssd_out_w, moe_router_w, moe_router_bias, moe_w_gate, moe_w_up, moe_w_down, shared_w_gate, shared_w_up, shared_w_down, final_norm_w):
    nb, seq, d = x.shape
    ctx_len = ctx.shape[1]
    n_lat = nb * seq
    n_ctx = nb * ctx_len

    cond = jnp.concatenate([c, c_ctx[None, :], jnp.zeros((SUBLANES - nb - 1, d), F32)], axis=0)
    mods = _ada(cond, ada_w, ada_b)

    def mod_vecs(layer, k):
        m = mods[layer, :, k * d:(k + 1) * d]
        lat = m[:nb].reshape(nb, 1, d)
        cx = jnp.broadcast_to(m[nb].reshape(1, 1, d), (nb, 1, d))
        return lat, cx

    x_lat = x.reshape(n_lat, d)
    x_ctx = ctx.reshape(n_ctx, d)

    sh_l, sh_c = mod_vecs(0, 0)
    sc_l, sc_c = mod_vecs(0, 1)
    u2_c = _s5_prep(ctx, norm1_w[0], sh_c, sc_c)
    u2_l = _s5_prep(x, norm1_w[0], sh_l, sc_l)
    bcat, a_re, a_im, ccat = _s5_pack_params(
        s5_lambda_re[0], s5_lambda_im[0], s5_log_step[0], s5_b_re[0], s5_b_im[0],
        s5_c_re[0], s5_c_im[0], nb)
    nblk = d // S5_CB
    s0 = jnp.zeros((nblk, SUBLANES, 2 * S5_NS), F32)
    y2_c, s_ctx = _s5_scan(u2_c.reshape(ctx_len * 2 * nb, d), bcat, a_re, a_im, ccat, s0)
    y2_l, _ = _s5_scan(u2_l.reshape(seq * 2 * nb, d), bcat, a_re, a_im, ccat, s_ctx)
    g_l = _s5_out(y2_l.reshape(seq, 2 * nb * d), u2_l, s5_d[0], nb, d)
    g_c = _s5_out(y2_c.reshape(ctx_len, 2 * nb * d), u2_c, s5_d[0], nb, d)
    g_all = jnp.concatenate([g_l, g_c], axis=0)
    half = s5_glu_w.shape[2] // 2
    tn = 1024
    glu = _matmul(g_all, [s5_glu_w[0], s5_glu_w[0]], [0, half // tn], half, tn=tn, tm=512,
                  biases=[s5_glu_b[0], s5_glu_b[0]], epilogue="glu", out_dtype=BF16, name="s5_glu")

    n_all = n_lat + n_ctx
    lat_tiles = seq // ROW_TILE

    def all_vecs(layer, k):
        return mods[layer, :nb + 1, k * d:(k + 1) * d].reshape(nb + 1, 1, d)

    def all_index(b, t):
        return jnp.minimum(t // lat_tiles, nb)

    xs0 = jnp.concatenate([x_lat, x_ctx], axis=0)
    xs1, v_all = _resnorm(xs0, glu, all_vecs(0, 2), norm2_w[0], all_vecs(0, 3), all_vecs(0, 4),
                          n_batch=1, seq=n_all, vec_index=all_index, v_dtype=F32)
    moe0 = _moe(v_all, 0, moe_router_w[0], moe_router_bias[0], moe_w_gate, moe_w_up,
                moe_w_down, shared_w_gate[0], shared_w_up[0], shared_w_down[0])

    g5_l, g5_c = mod_vecs(0, 5)
    sh_l, sh_c = mod_vecs(1, 0)
    sc_l, sc_c = mod_vecs(1, 1)
    x2_l, u_l = _resnorm(xs1, moe0, g5_l, norm1_w[1], sh_l, sc_l, n_batch=nb, seq=seq,
                         x_mode="slab", y_mode="slab", xo_mode="row", v_mode="row")
    ctx_tile0 = n_lat // ROW_TILE
    _, u_c = _resnorm(xs1, moe0, g5_c, norm1_w[1], sh_c, sc_c, n_batch=nb, seq=ctx_len,
                      write_x=False, x_tile0=ctx_tile0, y_tile0=ctx_tile0)
    u_all = jnp.concatenate([u_l, u_c], axis=0)
    in_w = ssd_in_w[0]
    inner = ssd_out_w.shape[1]
    conv_dim = ssd_conv_w.shape[2]
    heads = inner // SSD_HEADDIM
    tn = 1024
    z_all = _matmul(u_all, [in_w], [0], inner, tn=tn, tm=512, out_dtype=BF16, name="ssd_in_z")
    xbc = _matmul(u_all, [in_w], [inner // tn], conv_dim, tn=tn, tm=512, name="ssd_in_xbc")
    dt_bias = ssd_dt_bias[0].reshape(-1)
    dt_all = _matmul(u_all, [in_w], [(inner + conv_dim) // LANES], 2 * heads, tn=LANES, tm=512,
                     biases=[jnp.pad(dt_bias, (inner + conv_dim, 0))], epilogue="softplus",
                     name="ssd_in_dt")
    xc_l = _conv_silu(xbc, ssd_conv_w[0], ssd_conv_b[0], seq, 0, nb)
    xc_c = _conv_silu(xbc, ssd_conv_w[0], ssd_conv_b[0], ctx_len, n_lat // ctx_len, nb)
    rows = n_lat + n_ctx

    def b_transposed(xc):
        bm = xc[:, inner:inner + SSD_GROUPS * SSD_STATE]
        return bm.reshape(xc.shape[0], SSD_GROUPS, SSD_STATE).transpose(1, 2, 0)

    dtg = dt_all.reshape(rows, 2, SSD_GROUPS, HPG).transpose(2, 0, 1, 3).reshape(SSD_GROUPS, rows, DTC)
    dtgt = dtg.transpose(0, 2, 1)
    a = -jnp.exp(ssd_a_log[0].astype(F32))
    a_g = a.reshape(2, SSD_GROUPS, HPG).transpose(1, 0, 2).reshape(SSD_GROUPS, DTC)
    skip = jnp.repeat(ssd_d[0].astype(F32), SSD_HEADDIM).reshape(1, inner)
    yn = _ssd_scan(xc_l, xc_c, b_transposed(xc_l), b_transposed(xc_c), z_all, dtg, dtgt,
                   a_g.reshape(SSD_GROUPS, DTC, 1),
                   a_g.reshape(SSD_GROUPS, 1, DTC), skip, ssd_norm_w[0].reshape(1, inner),
                   nb, seq, ctx_len)
    y_lat = _matmul(yn, [ssd_out_w[0]], [0], d, tn=512, tm=512, out_dtype=BF16, name="ssd_out")

    g2_l, _ = mod_vecs(1, 2)
    sh4_l, _ = mod_vecs(1, 3)
    sc4_l, _ = mod_vecs(1, 4)
    x3_l, v_l = _resnorm(x2_l, y_lat, g2_l, norm2_w[1], sh4_l, sc4_l, n_batch=nb, seq=seq,
                         v_dtype=F32)
    moe1 = _moe(v_l, 1, moe_router_w[1], moe_router_bias[1], moe_w_gate, moe_w_up,
                moe_w_down, shared_w_gate[1], shared_w_up[1], shared_w_down[1])
    g5_l, _ = mod_vecs(1, 5)
    _, out = _resnorm(x3_l, moe1, g5_l, final_norm_w, None, None, n_batch=nb, seq=seq, v_dtype=F32,
                      write_x=False, v_mode="slab")
    return out.reshape(nb, seq, d)
```

```python
import functools
import math

import jax
import jax.numpy as jnp
from jax import lax
from jax.experimental import pallas as pl
from jax.experimental.pallas import tpu as pltpu

F32 = jnp.float32
BF16 = jnp.bfloat16

GRID_W = 64
EPS = 1e-6
S5_GROUP = 16
S5_STATE = 64
SSD_HEADDIM = 64
SSD_STATE = 128
SSD_GROUPS = 8
SSD_CONV = 5
SSD_CHUNK = 128
N_EXPERTS = 64
TOP_K = 8
N_EXPERT_GROUPS = 8
TOPK_GROUPS = 4
ROUTED_SCALE = 2.5
EXPERT_BLOCK = 256

VMEM_LIMIT_BYTES = 56 * 1024 * 1024
LANES = 128
SUBLANES = 8


def _cparams(n_axes):
    return pltpu.CompilerParams(
        dimension_semantics=("arbitrary",) * n_axes,
        vmem_limit_bytes=VMEM_LIMIT_BYTES)


def _silu(v):
    return v * jax.nn.sigmoid(v)


def _dot(a, b):
    return jnp.dot(a, b, preferred_element_type=F32)


def _split3(a):
    hi = a.astype(BF16)
    r1 = a - hi.astype(F32)
    mid = r1.astype(BF16)
    lo = (r1 - mid.astype(F32)).astype(BF16)
    return hi, mid, lo


def _dot_exact_rhs(a, sel):
    hi, mid, lo = _split3(a)
    return _dot(hi, sel) + _dot(mid, sel) + _dot(lo, sel)


def _dot_exact_lhs(sel, a):
    hi, mid, lo = _split3(a)
    return _dot(sel, hi) + _dot(sel, mid) + _dot(sel, lo)


def _ada_kernel(c_ref, w_ref, b_ref, o_ref):
    c = _silu(c_ref[...])
    o_ref[0] = _dot(c.astype(BF16), w_ref[0].astype(BF16)) + b_ref[0]


def _ada(cond, ada_w, ada_b):
    depth, d, n = ada_w.shape
    tn = 1024
    rows = cond.shape[0]
    return pl.pallas_call(
        _ada_kernel,
        out_shape=jax.ShapeDtypeStruct((depth, rows, n), F32),
        grid=(depth, n // tn),
        in_specs=[pl.BlockSpec((rows, d), lambda l, j: (0, 0)),
                  pl.BlockSpec((1, d, tn), lambda l, j: (l, 0, j)),
                  pl.BlockSpec((1, 1, tn), lambda l, j: (l, 0, j))],
        out_specs=pl.BlockSpec((1, rows, tn), lambda l, j: (l, 0, j)),
        compiler_params=_cparams(2),
        name="ada",
    )(cond, ada_w, ada_b.reshape(depth, 1, n))


def _mm_kernel(*refs, n_w, has_bias, epilogue):
    x_ref = refs[0]
    w_refs = refs[1:1 + n_w]
    pos = 1 + n_w
    b_refs = refs[pos:pos + n_w] if has_bias else ()
    pos += n_w if has_bias else 0
    o_ref = refs[pos]
    wbf_refs = refs[pos + 1:pos + 1 + n_w]

    @pl.when(pl.program_id(1) == 0)
    def _():
        for w_ref, wbf in zip(w_refs, wbf_refs):
            wbf[...] = w_ref[...].astype(BF16)

    x = x_ref[...].astype(BF16)
    zs = []
    for k in range(n_w):
        z = _dot(x, wbf_refs[k][...])
        if has_bias:
            z = z + b_refs[k][...]
        zs.append(z)
    if epilogue is None:
        out = zs[0]
    elif epilogue == "softplus":
        out = jax.nn.softplus(zs[0])
    elif epilogue == "glu":
        out = zs[0] * jax.nn.sigmoid(zs[1])
    elif epilogue == "swiglu":
        out = _silu(zs[0]) * zs[1]
    o_ref[...] = out.astype(o_ref.dtype)


def _matmul(x, ws, col_offsets, n_out, *, tn, tm, biases=None, epilogue=None,
            out_dtype=F32, name="matmul"):
    m, k = x.shape
    n_w = len(ws)
    has_bias = biases is not None
    in_specs = [pl.BlockSpec((tm, k), lambda j, i: (i, 0))]
    for off in col_offsets:
        in_specs.append(pl.BlockSpec((k, tn), lambda j, i, off=off: (0, j + off)))
    args = [x] + list(ws)
    if has_bias:
        for off in col_offsets:
            in_specs.append(pl.BlockSpec((1, tn), lambda j, i, off=off: (0, j + off)))
        args += [b.reshape(1, -1) for b in biases]
    return pl.pallas_call(
        functools.partial(_mm_kernel, n_w=n_w, has_bias=has_bias, epilogue=epilogue),
        out_shape=jax.ShapeDtypeStruct((m, n_out), out_dtype),
        grid=(n_out // tn, m // tm),
        in_specs=in_specs,
        out_specs=pl.BlockSpec((tm, tn), lambda j, i: (i, j)),
        scratch_shapes=[pltpu.VMEM((k, tn), BF16) for _ in range(n_w)],
        compiler_params=_cparams(2),
        name=name,
    )(*args)


SLABS = 8
ROW_TILE = 256


def _get_piece(ref, mode, j, rows, d):
    if mode == "slab":
        return ref[:, j * d:(j + 1) * d]
    return ref[j * rows:(j + 1) * rows, :]


def _put_piece(ref, mode, j, rows, d, val):
    if mode == "slab":
        ref[:, j * d:(j + 1) * d] = val.astype(ref.dtype)
    else:
        ref[j * rows:(j + 1) * rows, :] = val.astype(ref.dtype)


def _resnorm_kernel(*refs, has_y, write_x, x_mode, y_mode, xo_mode, v_mode, modulate, rows, d):
    it = iter(refs)
    x_ref = next(it)
    y_ref = next(it) if has_y else None
    g_ref = next(it) if has_y else None
    nw_ref = next(it)
    sh_ref = next(it) if modulate else None
    sc_ref = next(it) if modulate else None
    xo_ref = next(it) if write_x else None
    v_ref = next(it)
    nw = nw_ref[...]
    for j in range(SLABS):
        x = _get_piece(x_ref, x_mode, j, rows, d)
        if has_y:
            y = _get_piece(y_ref, y_mode, j, rows, d).astype(F32)
            x = x + g_ref[0] * y
            if write_x:
                _put_piece(xo_ref, xo_mode, j, rows, d, x)
        v = x * lax.rsqrt(jnp.mean(x * x, axis=-1, keepdims=True) + EPS) * nw
        if modulate:
            v = v * (1.0 + sc_ref[0]) + sh_ref[0]
        _put_piece(v_ref, v_mode, j, rows, d, v)


def _resnorm(x, y, gate, norm_w, shift, scale, *, n_batch, seq, x_mode="row",
             y_mode="row", xo_mode="row", v_mode="row", v_dtype=BF16, write_x=True,
             x_tile0=0, y_tile0=0, vec_index=None, name="resnorm"):
    d = x.shape[1]
    n = n_batch * seq
    if vec_index is None:
        vec_index = lambda b, t: b
    has_y = y is not None
    write_x = write_x and has_y
    modulate = shift is not None
    grows = seq // GRID_W
    tiles_per_seq = seq // ROW_TILE
    slab_used = "slab" in (x_mode, y_mode, xo_mode, v_mode)
    rows = grows if slab_used else ROW_TILE // SLABS
    if slab_used:
        assert grows * SLABS == ROW_TILE

    def spec(mode, tile0=0):
        if mode == "slab":
            return pl.BlockSpec((grows, SLABS * d), lambda b, t: (b, t))
        return pl.BlockSpec((ROW_TILE, d), lambda b, t: (b * tiles_per_seq + t + tile0, 0))

    def view(a, mode):
        return a.reshape(a.shape[0] // GRID_W, GRID_W * d) if mode == "slab" else a

    vec = pl.BlockSpec((1, 1, d), lambda b, t: (vec_index(b, t), 0, 0))
    in_specs = [spec(x_mode, x_tile0)]
    args = [view(x, x_mode)]
    if has_y:
        in_specs += [spec(y_mode, y_tile0), vec]
        args += [view(y, y_mode), gate]
    in_specs.append(pl.BlockSpec((1, d), lambda b, t: (0, 0)))
    args.append(norm_w.reshape(1, d))
    if modulate:
        in_specs += [vec, vec]
        args += [shift, scale]
    out_shape, out_specs = [], []
    def out_struct(mode, dtype):
        shape = (n // GRID_W, GRID_W * d) if mode == "slab" else (n, d)
        return jax.ShapeDtypeStruct(shape, dtype)

    if write_x:
        out_shape.append(out_struct(xo_mode, F32))
        out_specs.append(spec(xo_mode))
    out_shape.append(out_struct(v_mode, v_dtype))
    out_specs.append(spec(v_mode))
    outs = pl.pallas_call(
        functools.partial(_resnorm_kernel, has_y=has_y, write_x=write_x, x_mode=x_mode, y_mode=y_mode,
                          xo_mode=xo_mode, v_mode=v_mode, modulate=modulate, rows=rows, d=d),
        out_shape=out_shape,
        grid=(n_batch, tiles_per_seq),
        in_specs=in_specs,
        out_specs=out_specs,
        compiler_params=_cparams(2),
        name=name,
    )(*args)
    outs = [o.reshape(n, d) for o in outs]
    return (outs[0], outs[1]) if write_x else (None, outs[0])


S5_T = 128
S5_CB = 128
S5_NS = (S5_CB // S5_GROUP) * S5_STATE


def _reverse_rows(v_bf16):
    t = v_bf16.shape[0]
    r = lax.broadcasted_iota(jnp.int32, (t, t), 0)
    c = lax.broadcasted_iota(jnp.int32, (t, t), 1)
    flip = jnp.where(r + c == t - 1, 1.0, 0.0).astype(BF16)
    return _dot(flip, v_bf16)


def _s5_prep_kernel(xf_ref, xb_ref, nw_ref, sh_ref, sc_ref, o_ref, *, nb, d):
    nw = nw_ref[...]
    for b in range(nb):
        for rev, x_ref in ((False, xf_ref), (True, xb_ref)):
            x = x_ref[b]
            v = x * lax.rsqrt(jnp.mean(x * x, axis=-1, keepdims=True) + EPS) * nw
            v = (v * (1.0 + sc_ref[b]) + sh_ref[b]).astype(BF16)
            if rev:
                v = _reverse_rows(v).astype(BF16)
            k = b + (nb if rev else 0)
            o_ref[:, k * d:(k + 1) * d] = v


def _s5_prep(x3, norm_w, shift, scale):
    nb, s, d = x3.shape
    nt = s // S5_T
    return pl.pallas_call(
        functools.partial(_s5_prep_kernel, nb=nb, d=d),
        out_shape=jax.ShapeDtypeStruct((s, 2 * nb * d), BF16),
        grid=(nt,),
        in_specs=[pl.BlockSpec((nb, S5_T, d), lambda t: (0, t, 0)),
                  pl.BlockSpec((nb, S5_T, d), lambda t: (0, nt - 1 - t, 0)),
                  pl.BlockSpec((1, d), lambda t: (0, 0)),
                  pl.BlockSpec((nb, 1, d), lambda t: (0, 0, 0)),
                  pl.BlockSpec((nb, 1, d), lambda t: (0, 0, 0))],
        out_specs=pl.BlockSpec((S5_T, 2 * nb * d), lambda t: (t, 0)),
        compiler_params=_cparams(1),
        name="s5_prep",
    )(x3, x3, norm_w.reshape(1, d), shift, scale)


S5_BPS = 2


def _s5_scan_kernel(u_ref, bc_ref, are_ref, aim_ref, cc_ref, s0_ref, y_ref, sf_ref,
                    bu_ref, st_ref, *, n_tiles):
    i = pl.program_id(1)
    ns = S5_NS

    @pl.when(i == 0)
    def _():
        st_ref[...] = s0_ref[...]

    row = lax.broadcasted_iota(jnp.int32, (u_ref.shape[0], S5_CB), 0)
    is_fwd = jnp.bitwise_and(row, SUBLANES - 1) < (SUBLANES // 2)
    for blk in range(S5_BPS):
        u = u_ref[:, blk * S5_CB:(blk + 1) * S5_CB].astype(F32)
        lhs = jnp.concatenate([jnp.where(is_fwd, u, 0.0), jnp.where(is_fwd, 0.0, u)], axis=1)
        bu_ref[blk] = _dot(lhs.astype(BF16), bc_ref[blk])
    coef = [(are_ref[blk], aim_ref[blk]) for blk in range(S5_BPS)]

    def step(t, carry):
        r = pl.multiple_of(t * SUBLANES, SUBLANES)
        out = []
        for blk in range(S5_BPS):
            a_re, a_im = coef[blk]
            s_re, s_im = carry[2 * blk], carry[2 * blk + 1]
            n_re = a_re * s_re - a_im * s_im + bu_ref[blk, pl.ds(r, SUBLANES), 0:ns]
            n_im = a_re * s_im + a_im * s_re + bu_ref[blk, pl.ds(r, SUBLANES), ns:2 * ns]
            bu_ref[blk, pl.ds(r, SUBLANES), 0:ns] = n_re
            bu_ref[blk, pl.ds(r, SUBLANES), ns:2 * ns] = n_im
            out += [n_re, n_im]
        return tuple(out)

    init = []
    for blk in range(S5_BPS):
        init += [st_ref[blk, :, 0:ns], st_ref[blk, :, ns:2 * ns]]
    fin = lax.fori_loop(0, S5_T, step, tuple(init))
    for blk in range(S5_BPS):
        st_ref[blk, :, 0:ns] = fin[2 * blk]
        st_ref[blk, :, ns:2 * ns] = fin[2 * blk + 1]
        y2 = _dot(bu_ref[blk].astype(BF16), cc_ref[blk])
        y_ref[:, blk * S5_CB:(blk + 1) * S5_CB] = jnp.where(is_fwd, y2[:, 0:S5_CB],
                                                            y2[:, S5_CB:2 * S5_CB])

    @pl.when(i == n_tiles - 1)
    def _():
        sf_ref[...] = st_ref[...]


def _s5_scan(u2r, bcat, a_re, a_im, ccat, s0):
    rows, d = u2r.shape
    nblk = d // S5_CB
    tr = S5_T * SUBLANES
    n_tiles = rows // tr
    ns2 = 2 * S5_NS
    w = S5_BPS * S5_CB
    per_block = lambda j, i: (j, 0, 0)
    return pl.pallas_call(
        functools.partial(_s5_scan_kernel, n_tiles=n_tiles),
        out_shape=(jax.ShapeDtypeStruct((rows, d), F32),
                   jax.ShapeDtypeStruct((nblk, SUBLANES, ns2), F32)),
        grid=(nblk // S5_BPS, n_tiles),
        in_specs=[pl.BlockSpec((tr, w), lambda j, i: (i, j)),
                  pl.BlockSpec((S5_BPS, 2 * S5_CB, ns2), per_block),
                  pl.BlockSpec((S5_BPS, SUBLANES, S5_NS), per_block),
                  pl.BlockSpec((S5_BPS, SUBLANES, S5_NS), per_block),
                  pl.BlockSpec((S5_BPS, ns2, 2 * S5_CB), per_block),
                  pl.BlockSpec((S5_BPS, SUBLANES, ns2), per_block)],
        out_specs=(pl.BlockSpec((tr, w), lambda j, i: (i, j)),
                   pl.BlockSpec((S5_BPS, SUBLANES, ns2), per_block)),
        scratch_shapes=[pltpu.VMEM((S5_BPS, tr, ns2), F32),
                        pltpu.VMEM((S5_BPS, SUBLANES, ns2), F32)],
        compiler_params=_cparams(2),
        name="s5_scan",
    )(u2r, bcat, a_re, a_im, ccat, s0)


def _s5_out_kernel(yf_ref, yb_ref, u_ref, skip_ref, o_ref):
    yb = yb_ref[...]
    hi = yb.astype(BF16)
    lo = (yb - hi.astype(F32)).astype(BF16)
    y = (skip_ref[...] * u_ref[...].astype(F32) + yf_ref[...]
         + _reverse_rows(hi) + _reverse_rows(lo))
    o_ref[...] = jax.nn.gelu(y).astype(o_ref.dtype)


def _s5_out(y2, u2, skip, nb, d):
    s = y2.shape[0]
    nt = s // S5_T
    return pl.pallas_call(
        _s5_out_kernel,
        out_shape=jax.ShapeDtypeStruct((nb * s, d), BF16),
        grid=(nb, nt),
        in_specs=[pl.BlockSpec((S5_T, d), lambda b, t: (t, b)),
                  pl.BlockSpec((S5_T, d), lambda b, t: (nt - 1 - t, nb + b)),
                  pl.BlockSpec((S5_T, d), lambda b, t: (t, b)),
                  pl.BlockSpec((1, d), lambda b, t: (0, 0))],
        out_specs=pl.BlockSpec((S5_T, d), lambda b, t: (b * nt + t, 0)),
        compiler_params=_cparams(2),
        name="s5_out",
    )(y2, y2, u2, skip.reshape(1, d))


def _s5_pack_params(lam_re, lam_im, log_step, b_re, b_im, c_re, c_im, nb):
    f32 = F32
    g = lam_re.shape[1]
    gpb = S5_CB // S5_GROUP
    nblk = g // gpb
    eye = jnp.eye(gpb, dtype=f32)
    a_re_rows, a_im_rows, b_parts, c_parts = [], [], [], []
    for dr in range(2):
        lr, li = lam_re[dr].astype(f32), lam_im[dr].astype(f32)
        br, bi = b_re[dr].astype(f32), b_im[dr].astype(f32)
        dt = jnp.exp(log_step[dr].astype(f32))[:, None]
        mag = jnp.exp(lr * dt)
        abar_re = mag * jnp.cos(li * dt)
        abar_im = mag * jnp.sin(li * dt)
        num_re = abar_re - 1.0
        num_im = abar_im
        den = lr * lr + li * li
        f_re = (num_re * lr + num_im * li) / den
        f_im = (num_im * lr - num_re * li) / den
        bbar_re = f_re[..., None] * br - f_im[..., None] * bi
        bbar_im = f_re[..., None] * bi + f_im[..., None] * br
        a_re_rows.append(jnp.broadcast_to(abar_re.reshape(nblk, 1, S5_NS), (nblk, nb, S5_NS)))
        a_im_rows.append(jnp.broadcast_to(abar_im.reshape(nblk, 1, S5_NS), (nblk, nb, S5_NS)))

        def blockdiag_in(bb):
            b4 = bb.reshape(nblk, gpb, S5_STATE, S5_GROUP)
            return jnp.einsum('jgpk,gh->jgkhp', b4, eye).reshape(nblk, S5_CB, S5_NS)

        def blockdiag_out(cc):
            c4 = cc.astype(f32).reshape(nblk, gpb, S5_GROUP, S5_STATE)
            return jnp.einsum('jgkp,gh->jgphk', c4, eye).reshape(nblk, S5_NS, S5_CB)

        b_parts.append(jnp.concatenate([blockdiag_in(bbar_re), blockdiag_in(bbar_im)], axis=2))
        c_parts.append(jnp.concatenate([blockdiag_out(c_re[dr]), -blockdiag_out(c_im[dr])], axis=1))
    a_re = jnp.concatenate(a_re_rows, axis=1)
    a_im = jnp.concatenate(a_im_rows, axis=1)
    bcat = jnp.concatenate(b_parts, axis=1).astype(BF16)
    ccat = jnp.concatenate(c_parts, axis=2).astype(BF16)
    return bcat, a_re, a_im, ccat


def _conv_kernel(x_ref, w_ref, b_ref, o_ref, pad_ref, *, seq):
    halo = SUBLANES
    zeros = jnp.zeros((halo, pad_ref.shape[1]), F32)
    pad_ref[0:halo, :] = zeros
    pad_ref[halo + seq:2 * halo + seq, :] = zeros
    pad_ref[halo:halo + seq, :] = x_ref[...]
    acc = jnp.zeros((seq, pad_ref.shape[1]), F32) + b_ref[...]
    for k in range(SSD_CONV):
        off = halo + k - SSD_CONV // 2
        acc = acc + w_ref[k:k + 1, :] * pad_ref[off:off + seq, :]
    o_ref[...] = _silu(acc).astype(o_ref.dtype)


def _conv_silu(xbc, conv_w, conv_b, seq, row_block_offset, n_seq, out_dtype=BF16):
    _, c = xbc.shape
    tc = 512
    return pl.pallas_call(
        functools.partial(_conv_kernel, seq=seq),
        out_shape=jax.ShapeDtypeStruct((n_seq * seq, c), out_dtype),
        grid=(n_seq, c // tc),
        in_specs=[pl.BlockSpec((seq, tc), lambda b, j: (b + row_block_offset, j)),
                  pl.BlockSpec((SSD_CONV, tc), lambda b, j: (0, j)),
                  pl.BlockSpec((1, tc), lambda b, j: (0, j))],
        out_specs=pl.BlockSpec((seq, tc), lambda b, j: (b, j)),
        scratch_shapes=[pltpu.VMEM((seq + 2 * SUBLANES, tc), F32)],
        compiler_params=_cparams(2),
        name="ssd_conv",
    )(xbc, conv_w, conv_b.reshape(1, c))


HPG = 8
GCH = HPG * SSD_HEADDIM
DTC = 2 * HPG


def _ssd_kernel(xl_ref, btl_ref, cl_ref, dtl_ref, dttl_ref,
                xc_ref, btc_ref, dtc_ref, dttc_ref,
                z_ref, acol_ref, arow_ref, skip_ref, nw_ref, o_ref,
                htf_ref, htb_ref, yf_ref, yb_ref, *, n_lat, n_ctx):
    q = SSD_CHUNK
    r_i = lax.broadcasted_iota(jnp.int32, (q, q), 0)
    c_i = lax.broadcasted_iota(jnp.int32, (q, q), 1)
    tril = jnp.where(c_i <= r_i, 1.0, 0.0).astype(BF16)
    triu = jnp.where(r_i <= c_i, 1.0, 0.0).astype(BF16)
    lower = c_i <= r_i
    upper = c_i >= r_i
    lane = lax.broadcasted_iota(jnp.int32, (q, 2 * SSD_HEADDIM), 1)
    left = lane < SSD_HEADDIM
    e_r = lax.broadcasted_iota(jnp.int32, (DTC, GCH), 0)
    e_c = lax.broadcasted_iota(jnp.int32, (DTC, GCH), 1)
    a_row = arow_ref[0]
    a_col = acol_ref[0]

    def chunk(x_ref, bt_ref, c_ref, dt_ref, dtt_ref, r0, dr, ht_ref, y_ref):
        head_of_ch = lax.shift_right_logical(e_c, int(math.log2(SSD_HEADDIM)))
        expand = jnp.where(head_of_ch + dr * HPG == e_r, 1.0, 0.0).astype(BF16)
        x = x_ref[pl.ds(r0, q), :].astype(F32)
        bt = bt_ref[0, :, pl.ds(r0, q)]
        dt = dt_ref[0, pl.ds(r0, q), :]
        dtt = dtt_ref[0, :, pl.ds(r0, q)]
        la = dt * a_row
        lat = dtt * a_col
        cs = _dot_exact_lhs(tril, la)
        cst = _dot_exact_rhs(lat, triu)
        total = cs[q - 1:q, :]
        if dr == 1:
            rk, rkt = cs - la, cst - lat
            e_off, e_state = jnp.exp(total - rk), jnp.exp(rk)
        else:
            rk, rkt = cs, cst
            e_off, e_state = jnp.exp(rk), jnp.exp(total - rk)
        tot8 = jnp.broadcast_to(jnp.exp(total), (SUBLANES, DTC))
        pieces3 = [p.astype(F32) for p in _split3(tot8)]
        wo = _dot(jnp.concatenate([e_off] + pieces3, axis=0).astype(BF16), expand)
        w_off = wo[0:q, :]
        e_tot = (wo[q:q + 1, :] + wo[q + SUBLANES:q + SUBLANES + 1, :]
                 + wo[q + 2 * SUBLANES:q + 2 * SUBLANES + 1, :])
        w_state = _dot(e_state.astype(BF16), expand)
        dtx = _dot(dt.astype(BF16), expand)
        xd = x * dtx
        h_old = ht_ref[...]
        ht_ref[...] = h_old * e_tot + _dot(bt, (xd * w_state).astype(BF16))
        if y_ref is None:
            return
        cm = c_ref[pl.ds(r0, q), :]
        cb = _dot(cm, bt)
        xdb = xd.astype(BF16)
        pieces = []
        for pair in range(HPG // 2):
            ms = []
            for hh in (2 * pair, 2 * pair + 1):
                col = dr * HPG + hh
                colv = jnp.broadcast_to(rk[:, col:col + 1], (q, q))
                rowv = rkt[col:col + 1, :]
                if dr == 0:
                    seg = jnp.where(lower, colv - rowv, -1e30)
                else:
                    seg = jnp.where(upper, rowv - colv, -1e30)
                ms.append((cb * jnp.exp(seg)).astype(BF16))
            xp = xdb[:, pair * 2 * SSD_HEADDIM:(pair + 1) * 2 * SSD_HEADDIM]
            zero = jnp.zeros_like(xp)
            rhs = jnp.concatenate([jnp.where(left, xp, zero), jnp.where(left, zero, xp)], axis=0)
            pieces.append(_dot(jnp.concatenate(ms, axis=1), rhs))
        y_ref[pl.ds(r0, q), :] = (jnp.concatenate(pieces, axis=1)
                                  + _dot(cm, h_old.astype(BF16)) * w_off)

    htf_ref[...] = jnp.zeros_like(htf_ref)
    htb_ref[...] = jnp.zeros_like(htb_ref)

    def ctx_body(k, carry):
        rf = pl.multiple_of(k * q, q)
        rb = pl.multiple_of((n_ctx - 1 - k) * q, q)
        chunk(xc_ref, btc_ref, None, dtc_ref, dttc_ref, rf, 0, htf_ref, None)
        chunk(xc_ref, btc_ref, None, dtc_ref, dttc_ref, rb, 1, htb_ref, None)
        return carry

    lax.fori_loop(0, n_ctx, ctx_body, 0)

    def lat_body(k, carry):
        rf = pl.multiple_of(k * q, q)
        rb = pl.multiple_of((n_lat - 1 - k) * q, q)
        chunk(xl_ref, btl_ref, cl_ref, dtl_ref, dttl_ref, rf, 0, htf_ref, yf_ref)
        chunk(xl_ref, btl_ref, cl_ref, dtl_ref, dttl_ref, rb, 1, htb_ref, yb_ref)
        return carry

    lax.fori_loop(0, n_lat, lat_body, 0, unroll=2)

    def finish(k, carry):
        r0 = pl.multiple_of(k * q, q)
        x = xl_ref[pl.ds(r0, q), :].astype(F32)
        y = yf_ref[pl.ds(r0, q), :] + yb_ref[pl.ds(r0, q), :] + skip_ref[...] * x
        v = y * _silu(z_ref[pl.ds(r0, q), :].astype(F32))
        v = v * lax.rsqrt(jnp.mean(v * v, axis=-1, keepdims=True) + EPS) * nw_ref[...]
        o_ref[pl.ds(r0, q), :] = v.astype(o_ref.dtype)
        return carry

    lax.fori_loop(0, n_lat, finish, 0)


def _ssd_scan(xc_l, xc_c, bt_l, bt_c, z, dtg, dtgt, a_col, a_row, skip, norm_w, nb, seq, ctx_len):
    n_lat, n_ctx = seq // SSD_CHUNK, ctx_len // SSD_CHUNK
    inner = SSD_GROUPS * GCH
    coff = inner // SSD_STATE + SSD_GROUPS
    cb0 = nb * seq // ctx_len
    in_specs = [
        pl.BlockSpec((seq, GCH), lambda b, g: (b, g)),
        pl.BlockSpec((1, SSD_STATE, seq), lambda b, g: (g, 0, b)),
        pl.BlockSpec((seq, SSD_STATE), lambda b, g: (b, coff + g)),
        pl.BlockSpec((1, seq, DTC), lambda b, g: (g, b, 0)),
        pl.BlockSpec((1, DTC, seq), lambda b, g: (g, 0, b)),
        pl.BlockSpec((ctx_len, GCH), lambda b, g: (b, g)),
        pl.BlockSpec((1, SSD_STATE, ctx_len), lambda b, g: (g, 0, b)),
        pl.BlockSpec((1, ctx_len, DTC), lambda b, g: (g, cb0 + b, 0)),
        pl.BlockSpec((1, DTC, ctx_len), lambda b, g: (g, 0, cb0 + b)),
        pl.BlockSpec((seq, GCH), lambda b, g: (b, g)),
        pl.BlockSpec((1, DTC, 1), lambda b, g: (g, 0, 0)),
        pl.BlockSpec((1, 1, DTC), lambda b, g: (g, 0, 0)),
        pl.BlockSpec((1, GCH), lambda b, g: (0, g)),
        pl.BlockSpec((1, GCH), lambda b, g: (0, g)),
    ]
    return pl.pallas_call(
        functools.partial(_ssd_kernel, n_lat=n_lat, n_ctx=n_ctx),
        out_shape=jax.ShapeDtypeStruct((nb * seq, inner), BF16),
        grid=(nb, SSD_GROUPS),
        in_specs=in_specs,
        out_specs=pl.BlockSpec((seq, GCH), lambda b, g: (b, g)),
        scratch_shapes=[pltpu.VMEM((SSD_STATE, GCH), F32), pltpu.VMEM((SSD_STATE, GCH), F32),
                        pltpu.VMEM((seq, GCH), F32), pltpu.VMEM((seq, GCH), F32)],
        compiler_params=_cparams(2),
        name="ssd_scan",
    )(xc_l, bt_l, xc_l, dtg, dtgt, xc_c, bt_c, dtg, dtgt, z, a_col, a_row, skip, norm_w)


def _expert_kernel(be_ref, nu_ref, first_ref, slot_ref, next_ref, x_ref, wg_hbm, wu_hbm, wd_hbm,
                   o_ref, stage_g, stage_u, stage_d, wgb, wub, wdb, sem, *, layer):
    i = pl.program_id(0)

    def weight_copies(e, s):
        return (pltpu.make_async_copy(wg_hbm.at[layer, e], stage_g.at[s], sem.at[s, 0]),
                pltpu.make_async_copy(wu_hbm.at[layer, e], stage_u.at[s], sem.at[s, 1]),
                pltpu.make_async_copy(wd_hbm.at[layer, e], stage_d.at[s], sem.at[s, 2]))

    @pl.when(i == 0)
    def _():
        for c in weight_copies(be_ref[0], 0):
            c.start()

    @pl.when(jnp.logical_and(i < nu_ref[0], first_ref[i] == 1))
    def _():
        s = slot_ref[i]

        @pl.when(next_ref[i] >= 0)
        def _():
            for c in weight_copies(next_ref[i], 1 - s):
                c.start()

        for c in weight_copies(be_ref[i], s):
            c.wait()
        wgb[...] = stage_g[s].astype(BF16)
        wub[...] = stage_u[s].astype(BF16)
        wdb[...] = stage_d[s].astype(BF16)

    @pl.when(i < nu_ref[0])
    def _():
        lo, hi = _unpack_rows(x_ref[...])
        x = jnp.concatenate([lo.astype(BF16), hi.astype(BF16)], axis=1)
        h = _silu(_dot(x, wgb[...])) * _dot(x, wub[...])
        o_ref[...] = _pack_rows(_dot(h.astype(BF16), wdb[...]))

    @pl.when(i >= nu_ref[0])
    def _():
        o_ref[...] = jnp.zeros_like(o_ref)


def _experts(x_sorted, block_e, n_used, first, slot, next_e, w_gate, w_up, w_down, layer):
    cap, d = x_sorted.shape
    ff = w_gate.shape[3]
    n_blocks = cap // EXPERT_BLOCK

    def row_block(i, be, nu, fi, sl, nx):
        return (jnp.minimum(i, nu[0] - 1), 0)

    hbm = pl.BlockSpec(memory_space=pl.ANY)
    grid_spec = pltpu.PrefetchScalarGridSpec(
        num_scalar_prefetch=5,
        grid=(n_blocks,),
        in_specs=[pl.BlockSpec((EXPERT_BLOCK, d), row_block), hbm, hbm, hbm],
        out_specs=pl.BlockSpec((EXPERT_BLOCK, d), lambda i, be, nu, fi, sl, nx: (i, 0)),
        scratch_shapes=[pltpu.VMEM((2, 2 * d, ff), F32), pltpu.VMEM((2, 2 * d, ff), F32),
                        pltpu.VMEM((2, ff, 2 * d), F32),
                        pltpu.VMEM((2 * d, ff), BF16), pltpu.VMEM((2 * d, ff), BF16),
                        pltpu.VMEM((ff, 2 * d), BF16), pltpu.SemaphoreType.DMA((2, 3))])
    return pl.pallas_call(
        functools.partial(_expert_kernel, layer=layer),
        out_shape=jax.ShapeDtypeStruct((cap, d), jnp.uint32),
        grid_spec=grid_spec,
        compiler_params=_cparams(1),
        name="moe_experts",
    )(block_e, n_used, first, slot, next_e, x_sorted, w_gate, w_up, w_down)


DISPATCH_T = 256
COMBINE_T = 128


HIGH_HALF = 0xFFFF0000


def _pack_rows(a):
    half = a.shape[1] // 2
    lo = lax.bitcast_convert_type(a[:, :half].astype(BF16).astype(F32), jnp.uint32)
    hi = lax.bitcast_convert_type(a[:, half:].astype(BF16).astype(F32), jnp.uint32)
    return jnp.bitwise_or(jnp.bitwise_and(hi, jnp.uint32(HIGH_HALF)),
                          lax.shift_right_logical(lo, jnp.uint32(16)))


def _unpack_rows(w):
    lo = lax.bitcast_convert_type(lax.shift_left(w, jnp.uint32(16)), F32)
    hi = lax.bitcast_convert_type(jnp.bitwise_and(w, jnp.uint32(HIGH_HALF)), F32)
    return lo, hi


def _dispatch_kernel(pe_ref, pd_ref, dest_ref, v_ref, xs_ref, zero_buf, row_buf, sem):
    t_rows = v_ref.shape[0]
    row_buf[...] = _pack_rows(v_ref[...])

    @pl.when(pl.program_id(0) == 0)
    def _():
        zero_buf[...] = jnp.zeros_like(zero_buf)

        def fill(e, carry):
            @pl.when(pd_ref[e] > 0)
            def _():
                start = pl.multiple_of(pe_ref[e] - EXPERT_BLOCK, EXPERT_BLOCK)
                pltpu.make_async_copy(zero_buf, xs_ref.at[pl.ds(start, EXPERT_BLOCK)], sem).start()
            return carry

        def drain(e, carry):
            @pl.when(pd_ref[e] > 0)
            def _():
                pltpu.make_async_copy(zero_buf, xs_ref.at[pl.ds(0, EXPERT_BLOCK)], sem).wait()
            return carry

        lax.fori_loop(0, N_EXPERTS, fill, 0)
        lax.fori_loop(0, N_EXPERTS, drain, 0)

        n_blocks = xs_ref.shape[0] // EXPERT_BLOCK
        first_unused = pe_ref[N_EXPERTS - 1] // EXPERT_BLOCK

        def fill_tail(b, carry):
            start = pl.multiple_of(b * EXPERT_BLOCK, EXPERT_BLOCK)
            pltpu.make_async_copy(zero_buf, xs_ref.at[pl.ds(start, EXPERT_BLOCK)], sem).start()
            return carry

        def drain_tail(b, carry):
            pltpu.make_async_copy(zero_buf, xs_ref.at[pl.ds(0, EXPERT_BLOCK)], sem).wait()
            return carry

        lax.fori_loop(first_unused, n_blocks, fill_tail, 0)
        lax.fori_loop(first_unused, n_blocks, drain_tail, 0)

    def issue(t, carry):
        for k in range(TOP_K):
            pltpu.make_async_copy(row_buf.at[pl.ds(t, 1)],
                                  xs_ref.at[pl.ds(dest_ref[k, t], 1)], sem).start(priority=k % 2)
        return carry

    lax.fori_loop(0, t_rows, issue, 0)
    for k in range(TOP_K):
        pltpu.make_async_copy(row_buf, xs_ref.at[pl.ds(0, t_rows)], sem).wait()


def _dispatch(v, dest, pad_end, padded, cap):
    n, d = v.shape
    grid_spec = pltpu.PrefetchScalarGridSpec(
        num_scalar_prefetch=2,
        grid=(n // DISPATCH_T,),
        in_specs=[pl.BlockSpec((TOP_K, DISPATCH_T), lambda i, pe, pd: (0, i),
                               memory_space=pltpu.SMEM),
                  pl.BlockSpec((DISPATCH_T, d), lambda i, pe, pd: (i, 0))],
        out_specs=pl.BlockSpec(memory_space=pl.ANY),
        scratch_shapes=[pltpu.VMEM((EXPERT_BLOCK, d // 2), jnp.uint32),
                        pltpu.VMEM((DISPATCH_T, d // 2), jnp.uint32),
                        pltpu.SemaphoreType.DMA(())])
    return pl.pallas_call(
        _dispatch_kernel,
        out_shape=jax.ShapeDtypeStruct((cap, d // 2), jnp.uint32),
        grid_spec=grid_spec,
        compiler_params=_cparams(1),
        name="moe_dispatch",
    )(pad_end, padded, dest, v)


def _combine_kernel(dest_ref, y_ref, gate_ref, sh_ref, o_ref, buf, sem):
    t_rows = o_ref.shape[0]

    def issue(t, carry):
        for k in range(TOP_K):
            pltpu.make_async_copy(y_ref.at[pl.ds(dest_ref[k, t], 1)],
                                  buf.at[k, pl.ds(t, 1)], sem).start(priority=k % 2)
        return carry

    lax.fori_loop(0, t_rows, issue, 0)
    for k in range(TOP_K):
        pltpu.make_async_copy(y_ref.at[pl.ds(0, t_rows)], buf.at[k], sem).wait()
    half = o_ref.shape[1] // 2
    acc_lo = sh_ref[:, :half]
    acc_hi = sh_ref[:, half:]
    for k in range(TOP_K):
        lo, hi = _unpack_rows(buf[k])
        g = gate_ref[:, k:k + 1]
        acc_lo = acc_lo + g * lo
        acc_hi = acc_hi + g * hi
    o_ref[:, :half] = acc_lo
    o_ref[:, half:] = acc_hi


def _combine(y_sorted, dest, gate_t, shared):
    n, d = shared.shape
    return pl.pallas_call(
        _combine_kernel,
        out_shape=jax.ShapeDtypeStruct((n, d), F32),
        grid=(n // COMBINE_T,),
        in_specs=[pl.BlockSpec((TOP_K, COMBINE_T), lambda i: (0, i), memory_space=pltpu.SMEM),
                  pl.BlockSpec(memory_space=pl.ANY),
                  pl.BlockSpec((COMBINE_T, TOP_K), lambda i: (i, 0)),
                  pl.BlockSpec((COMBINE_T, d), lambda i: (i, 0))],
        out_specs=pl.BlockSpec((COMBINE_T, d), lambda i: (i, 0)),
        scratch_shapes=[pltpu.VMEM((TOP_K, COMBINE_T, d // 2), jnp.uint32),
                        pltpu.SemaphoreType.DMA(())],
        compiler_params=_cparams(1),
        name="moe_combine",
    )(dest, y_sorted, gate_t, shared)


ROUTE_T = 512
GROUP_SIZE = N_EXPERTS // N_EXPERT_GROUPS


def _route_kernel(v_ref, rwh_ref, rwl_ref, bias_ref, eid_ref, rank_ref, gate_ref, cnt_ref,
                  carry_ref, *, n_tiles):
    i = pl.program_id(0)
    t = v_ref.shape[0]
    ng, gs = N_EXPERT_GROUPS, GROUP_SIZE
    neg = -jnp.inf

    @pl.when(i == 0)
    def _():
        carry_ref[...] = jnp.zeros_like(carry_ref)

    v = v_ref[...].astype(BF16)
    nt_dims = (((1,), (1,)), ((), ()))
    logits = (lax.dot_general(rwh_ref[...], v, nt_dims, preferred_element_type=F32)
              + lax.dot_general(rwl_ref[...], v, nt_dims, preferred_element_type=F32))
    scores = jax.nn.sigmoid(logits)
    biased = scores + bias_ref[:, 0:1]
    x3 = biased.reshape(ng, gs, t)
    s3 = scores.reshape(ng, gs, t)
    mi = lax.broadcasted_iota(jnp.int32, (ng, gs, t), 1).astype(F32)
    fi = lax.broadcasted_iota(jnp.int32, (ng, gs, t), 0).astype(F32) * gs + mi
    gi = lax.broadcasted_iota(jnp.int32, (ng, 1, t), 0).astype(F32)

    m1 = jnp.max(x3, axis=1, keepdims=True)
    i1 = jnp.min(jnp.where(x3 == m1, mi, float(gs)), axis=1, keepdims=True)
    m2 = jnp.max(jnp.where(mi == i1, neg, x3), axis=1, keepdims=True)
    cur = m1 + m2
    gmask = jnp.zeros((ng, 1, t), F32)
    for _ in range(TOPK_GROUPS):
        gm = jnp.max(cur, axis=0, keepdims=True)
        idx = jnp.min(jnp.where(cur == gm, gi, float(ng)), axis=0, keepdims=True)
        hit = gi == idx
        gmask = jnp.where(hit, 1.0, gmask)
        cur = jnp.where(hit, neg, cur)

    cand = jnp.where(gmask > 0.0, x3, neg)
    sel = jnp.zeros((ng, gs, t), F32)
    eids = []
    for _ in range(TOP_K):
        m = jnp.max(jnp.max(cand, axis=1, keepdims=True), axis=0, keepdims=True)
        idx = jnp.where(cand == m, fi, float(N_EXPERTS))
        idx = jnp.min(jnp.min(idx, axis=1, keepdims=True), axis=0, keepdims=True)
        hit = fi == idx
        sel = jnp.where(hit, 1.0, sel)
        cand = jnp.where(hit, neg, cand)
        eids.append(idx)

    selr = sel.reshape(N_EXPERTS, t)
    r_i = lax.broadcasted_iota(jnp.int32, (t, t), 0)
    c_i = lax.broadcasted_iota(jnp.int32, (t, t), 1)
    before = jnp.where(r_i < c_i, 1.0, 0.0).astype(BF16)
    rank = _dot(selr.astype(BF16), before) + carry_ref[:, 0:1]
    carry_ref[...] = carry_ref[...] + jnp.sum(selr, axis=1, keepdims=True)
    rank3 = rank.reshape(ng, gs, t)

    gsel = sel * s3
    denom = jnp.sum(jnp.sum(gsel, axis=1, keepdims=True), axis=0, keepdims=True)
    gate3 = gsel / denom * ROUTED_SCALE

    def pick(a3, hit):
        return jnp.sum(jnp.sum(jnp.where(hit, a3, 0.0), axis=1, keepdims=True), axis=0,
                       keepdims=True).reshape(1, t)

    for k in range(TOP_K):
        hit = fi == eids[k]
        eid_ref[k:k + 1, :] = eids[k].reshape(1, t).astype(jnp.int32)
        rank_ref[k:k + 1, :] = pick(rank3, hit).astype(jnp.int32)
        gate_ref[k:k + 1, :] = pick(gate3, hit)

    @pl.when(i == n_tiles - 1)
    def _():
        cnt_ref[...] = carry_ref[...]


def _route(v, router_w, router_bias):
    n, d = v.shape
    n_tiles = n // ROUTE_T
    rwt = router_w.astype(F32).T
    rwh = rwt.astype(BF16)
    rwl = (rwt - rwh.astype(F32)).astype(BF16)
    bias = jnp.broadcast_to(router_bias.astype(F32)[:, None], (N_EXPERTS, LANES))
    slot = pl.BlockSpec((TOP_K, ROUTE_T), lambda i: (0, i))
    full = pl.BlockSpec((N_EXPERTS, d), lambda i: (0, 0))
    return pl.pallas_call(
        functools.partial(_route_kernel, n_tiles=n_tiles),
        out_shape=(jax.ShapeDtypeStruct((TOP_K, n), jnp.int32),
                   jax.ShapeDtypeStruct((TOP_K, n), jnp.int32),
                   jax.ShapeDtypeStruct((TOP_K, n), F32),
                   jax.ShapeDtypeStruct((N_EXPERTS, LANES), F32)),
        grid=(n_tiles,),
        in_specs=[pl.BlockSpec((ROUTE_T, d), lambda i: (i, 0)), full, full,
                  pl.BlockSpec((N_EXPERTS, LANES), lambda i: (0, 0))],
        out_specs=(slot, slot, slot, pl.BlockSpec((N_EXPERTS, LANES), lambda i: (0, 0))),
        scratch_shapes=[pltpu.VMEM((N_EXPERTS, LANES), F32)],
        compiler_params=_cparams(1),
        name="moe_route",
    )(v, rwh, rwl, bias)


def _slot_rows_kernel(ps_ref, eid_ref, rank_ref, o_ref):
    eid = eid_ref[...]
    acc = rank_ref[...]
    for e in range(N_EXPERTS):
        acc = acc + jnp.where(eid == e, ps_ref[e], 0)
    o_ref[...] = acc


def _slot_rows(eid, rank, pad_start):
    k, n = eid.shape
    whole = lambda i, ps: (0, 0)
    grid_spec = pltpu.PrefetchScalarGridSpec(
        num_scalar_prefetch=1, grid=(1,),
        in_specs=[pl.BlockSpec((k, n), whole), pl.BlockSpec((k, n), whole)],
        out_specs=pl.BlockSpec((k, n), whole))
    return pl.pallas_call(
        _slot_rows_kernel,
        out_shape=jax.ShapeDtypeStruct((k, n), jnp.int32),
        grid_spec=grid_spec,
        compiler_params=_cparams(1),
        name="moe_slot_rows",
    )(pad_start, eid, rank)


def _moe(v, layer, router_w, router_bias, w_gate, w_up, w_down, sw_gate, sw_up, sw_down):
    n, d = v.shape
    eid, rank, gate, cnt = _route(v, router_w, router_bias)
    nk = n * TOP_K
    cap = -(-nk // EXPERT_BLOCK) * EXPERT_BLOCK + N_EXPERTS * EXPERT_BLOCK
    n_blocks = cap // EXPERT_BLOCK
    counts = cnt[:, 0].astype(jnp.int32)
    padded = (counts + EXPERT_BLOCK - 1) // EXPERT_BLOCK * EXPERT_BLOCK
    pad_end = jnp.cumsum(padded)
    pad_start = pad_end - padded
    dest = _slot_rows(eid, rank, pad_start.astype(jnp.int32))
    block_start = jnp.arange(n_blocks, dtype=jnp.int32) * EXPERT_BLOCK
    block_e = jnp.minimum(jnp.sum((pad_end[None, :] <= block_start[:, None]).astype(jnp.int32), axis=1),
                          N_EXPERTS - 1).astype(jnp.int32)
    n_used = (pad_end[-1] // EXPERT_BLOCK).astype(jnp.int32).reshape(1)
    block_id = jnp.arange(n_blocks, dtype=jnp.int32)
    first = ((pad_start[block_e] == block_start) & (block_id < n_used[0])).astype(jnp.int32)
    slot = ((jnp.cumsum(first) - 1) % 2).astype(jnp.int32)
    experts = jnp.arange(N_EXPERTS, dtype=jnp.int32)
    later = (experts[None, :] > experts[:, None]) & (padded[None, :] > 0)
    next_of = jnp.min(jnp.where(later, experts[None, :], N_EXPERTS), axis=1)
    next_of = jnp.where(next_of == N_EXPERTS, -1, next_of).astype(jnp.int32)
    next_e = next_of[block_e]
    x_sorted = _dispatch(v, dest, pad_end.astype(jnp.int32), padded.astype(jnp.int32), cap)
    y_sorted = _experts(x_sorted, block_e, n_used, first, slot, next_e, w_gate, w_up, w_down, layer)
    hs = _matmul(v, [sw_gate, sw_up], [0, 0], sw_gate.shape[1], tn=sw_gate.shape[1], tm=512,
                 epilogue="swiglu", out_dtype=BF16, name="shared_up")
    shared = _matmul(hs, [sw_down], [0], d, tn=1024, tm=512, name="shared_down")
    return _combine(y_sorted, dest, gate.T, shared)


def kernel(x, c, ctx, c_ctx, ada_w, ada_b, norm1_w, norm2_w, s5_lambda_re, s5_lambda_im, s5_log_step, s5_b_re, s5_b_im, s5_c_re, s5_c_im, s5_d, s5_glu_w, s5_glu_b, ssd_in_w, ssd_conv_w, ssd_conv_b, ssd_dt_bias, ssd_a_log, ssd_d, ssd_norm_w, ssd_out_w, moe_router_w, moe_router_bias, moe_w_gate, moe_w_up, moe_w_down, shared_w_gate, shared_w_up, shared_w_down, final_norm_w):
    nb, seq, d = x.shape
    ctx_len = ctx.shape[1]
    n_lat = nb * seq
    n_ctx = nb * ctx_len

    cond = jnp.concatenate([c, c_ctx[None, :], jnp.zeros((SUBLANES - nb - 1, d), F32)], axis=0)
    mods = _ada(cond, ada_w, ada_b)

    def mod_vecs(layer, k):
        m = mods[layer, :, k * d:(k + 1) * d]
        lat = m[:nb].reshape(nb, 1, d)
        cx = jnp.broadcast_to(m[nb].reshape(1, 1, d), (nb, 1, d))
        return lat, cx

    x_lat = x.reshape(n_lat, d)
    x_ctx = ctx.reshape(n_ctx, d)

    sh_l, sh_c = mod_vecs(0, 0)
    sc_l, sc_c = mod_vecs(0, 1)
    u2_c = _s5_prep(ctx, norm1_w[0], sh_c, sc_c)
    u2_l = _s5_prep(x, norm1_w[0], sh_l, sc_l)
    bcat, a_re, a_im, ccat = _s5_pack_params(
        s5_lambda_re[0], s5_lambda_im[0], s5_log_step[0], s5_b_re[0], s5_b_im[0],
        s5_c_re[0], s5_c_im[0], nb)
    nblk = d // S5_CB
    s0 = jnp.zeros((nblk, SUBLANES, 2 * S5_NS), F32)
    y2_c, s_ctx = _s5_scan(u2_c.reshape(ctx_len * 2 * nb, d), bcat, a_re, a_im, ccat, s0)
    y2_l, _ = _s5_scan(u2_l.reshape(seq * 2 * nb, d), bcat, a_re, a_im, ccat, s_ctx)
    g_l = _s5_out(y2_l.reshape(seq, 2 * nb * d), u2_l, s5_d[0], nb, d)
    g_c = _s5_out(y2_c.reshape(ctx_len, 2 * nb * d), u2_c, s5_d[0], nb, d)
    g_all = jnp.concatenate([g_l, g_c], axis=0)
    half = s5_glu_w.shape[2] // 2
    tn = 1024
    glu = _matmul(g_all, [s5_glu_w[0], s5_glu_w[0]], [0, half // tn], half, tn=tn, tm=512,
                  biases=[s5_glu_b[0], s5_glu_b[0]], epilogue="glu", out_dtype=BF16, name="s5_glu")

    n_all = n_lat + n_ctx
    lat_tiles = seq // ROW_TILE

    def all_vecs(layer, k):
        return mods[layer, :nb + 1, k * d:(k + 1) * d].reshape(nb + 1, 1, d)

    def all_index(b, t):
        return jnp.minimum(t // lat_tiles, nb)

    xs0 = jnp.concatenate([x_lat, x_ctx], axis=0)
    xs1, v_all = _resnorm(xs0, glu, all_vecs(0, 2), norm2_w[0], all_vecs(0, 3), all_vecs(0, 4),
                          n_batch=1, seq=n_all, vec_index=all_index, v_dtype=F32)
    moe0 = _moe(v_all, 0, moe_router_w[0], moe_router_bias[0], moe_w_gate, moe_w_up,
                moe_w_down, shared_w_gate[0], shared_w_up[0], shared_w_down[0])

    g5_l, g5_c = mod_vecs(0, 5)
    sh_l, sh_c = mod_vecs(1, 0)
    sc_l, sc_c = mod_vecs(1, 1)
    x2_l, u_l = _resnorm(xs1, moe0, g5_l, norm1_w[1], sh_l, sc_l, n_batch=nb, seq=seq,
                         x_mode="slab", y_mode="slab", xo_mode="row", v_mode="row")
    ctx_tile0 = n_lat // ROW_TILE
    _, u_c = _resnorm(xs1, moe0, g5_c, norm1_w[1], sh_c, sc_c, n_batch=nb, seq=ctx_len,
                      write_x=False, x_tile0=ctx_tile0, y_tile0=ctx_tile0)
    u_all = jnp.concatenate([u_l, u_c], axis=0)
    in_w = ssd_in_w[0]
    inner = ssd_out_w.shape[1]
    conv_dim = ssd_conv_w.shape[2]
    heads = inner // SSD_HEADDIM
    tn = 1024
    z_all = _matmul(u_all, [in_w], [0], inner, tn=tn, tm=512, out_dtype=BF16, name="ssd_in_z")
    xbc = _matmul(u_all, [in_w], [inner // tn], conv_dim, tn=tn, tm=512, name="ssd_in_xbc")
    dt_bias = ssd_dt_bias[0].reshape(-1)
    dt_all = _matmul(u_all, [in_w], [(inner + conv_dim) // LANES], 2 * heads, tn=LANES, tm=512,
                     biases=[jnp.pad(dt_bias, (inner + conv_dim, 0))], epilogue="softplus",
                     name="ssd_in_dt")
    xc_l = _conv_silu(xbc, ssd_conv_w[0], ssd_conv_b[0], seq, 0, nb)
    xc_c = _conv_silu(xbc, ssd_conv_w[0], ssd_conv_b[0], ctx_len, n_lat // ctx_len, nb)
    rows = n_lat + n_ctx

    def b_transposed(xc):
        bm = xc[:, inner:inner + SSD_GROUPS * SSD_STATE]
        return bm.reshape(xc.shape[0], SSD_GROUPS, SSD_STATE).transpose(1, 2, 0)

    dtg = dt_all.reshape(rows, 2, SSD_GROUPS, HPG).transpose(2, 0, 1, 3).reshape(SSD_GROUPS, rows, DTC)
    dtgt = dtg.transpose(0, 2, 1)
    a = -jnp.exp(ssd_a_log[0].astype(F32))
    a_g = a.reshape(2, SSD_GROUPS, HPG).transpose(1, 0, 2).reshape(SSD_GROUPS, DTC)
    skip = jnp.repeat(ssd_d[0].astype(F32), SSD_HEADDIM).reshape(1, inner)
    yn = _ssd_scan(xc_l, xc_c, b_transposed(xc_l), b_transposed(xc_c), z_all, dtg, dtgt,
                   a_g.reshape(SSD_GROUPS, DTC, 1),
                   a_g.reshape(SSD_GROUPS, 1, DTC), skip, ssd_norm_w[0].reshape(1, inner),
                   nb, seq, ctx_len)
    y_lat = _matmul(yn, [ssd_out_w[0]], [0], d, tn=512, tm=512, out_dtype=BF16, name="ssd_out")

    g2_l, _ = mod_vecs(1, 2)
    sh4_l, _ = mod_vecs(1, 3)
    sc4_l, _ = mod_vecs(1, 4)
    x3_l, v_l = _resnorm(x2_l, y_lat, g2_l, norm2_w[1], sh4_l, sc4_l, n_batch=nb, seq=seq,
                         v_dtype=F32)
    moe1 = _moe(v_l, 1, moe_router_w[1], moe_router_bias[1], moe_w_gate, moe_w_up,
                moe_w_down, shared_w_gate[1], shared_w_up[1], shared_w_down[1])
    g5_l, _ = mod_vecs(1, 5)
    _, out = _resnorm(x3_l, moe1, g5_l, final_norm_w, None, None, n_batch=nb, seq=seq, v_dtype=F32,
                      write_x=False, v_mode="slab")
    return out.reshape(nb, seq, d)
```

```python
import functools
import math

import jax
import jax.numpy as jnp
from jax import lax
from jax.experimental import pallas as pl
from jax.experimental.pallas import tpu as pltpu

F32 = jnp.float32
BF16 = jnp.bfloat16

GRID_W = 64
EPS = 1e-6
S5_GROUP = 16
S5_STATE = 64
SSD_HEADDIM = 64
SSD_STATE = 128
SSD_GROUPS = 8
SSD_CONV = 5
SSD_CHUNK = 128
N_EXPERTS = 64
TOP_K = 8
N_EXPERT_GROUPS = 8
TOPK_GROUPS = 4
ROUTED_SCALE = 2.5
EXPERT_BLOCK = 256

VMEM_LIMIT_BYTES = 56 * 1024 * 1024
LANES = 128
SUBLANES = 8


def _cparams(n_axes):
    return pltpu.CompilerParams(
        dimension_semantics=("arbitrary",) * n_axes,
        vmem_limit_bytes=VMEM_LIMIT_BYTES)


def _silu(v):
    return v * jax.nn.sigmoid(v)


def _dot(a, b):
    return jnp.dot(a, b, preferred_element_type=F32)


def _split3(a):
    hi = a.astype(BF16)
    r1 = a - hi.astype(F32)
    mid = r1.astype(BF16)
    lo = (r1 - mid.astype(F32)).astype(BF16)
    return hi, mid, lo


def _dot_exact_rhs(a, sel):
    hi, mid, lo = _split3(a)
    return _dot(hi, sel) + _dot(mid, sel) + _dot(lo, sel)


def _dot_exact_lhs(sel, a):
    hi, mid, lo = _split3(a)
    return _dot(sel, hi) + _dot(sel, mid) + _dot(sel, lo)


def _ada_kernel(c_ref, w_ref, b_ref, o_ref):
    c = _silu(c_ref[...])
    o_ref[0] = _dot(c.astype(BF16), w_ref[0].astype(BF16)) + b_ref[0]


def _ada(cond, ada_w, ada_b):
    depth, d, n = ada_w.shape
    tn = 1024
    rows = cond.shape[0]
    return pl.pallas_call(
        _ada_kernel,
        out_shape=jax.ShapeDtypeStruct((depth, rows, n), F32),
        grid=(depth, n // tn),
        in_specs=[pl.BlockSpec((rows, d), lambda l, j: (0, 0)),
                  pl.BlockSpec((1, d, tn), lambda l, j: (l, 0, j)),
                  pl.BlockSpec((1, 1, tn), lambda l, j: (l, 0, j))],
        out_specs=pl.BlockSpec((1, rows, tn), lambda l, j: (l, 0, j)),
        compiler_params=_cparams(2),
        name="ada",
    )(cond, ada_w, ada_b.reshape(depth, 1, n))


def _mm_kernel(*refs, n_w, has_bias, epilogue):
    x_ref = refs[0]
    w_refs = refs[1:1 + n_w]
    pos = 1 + n_w
    b_refs = refs[pos:pos + n_w] if has_bias else ()
    pos += n_w if has_bias else 0
    o_ref = refs[pos]
    wbf_refs = refs[pos + 1:pos + 1 + n_w]

    @pl.when(pl.program_id(1) == 0)
    def _():
        for w_ref, wbf in zip(w_refs, wbf_refs):
            wbf[...] = w_ref[...].astype(BF16)

    x = x_ref[...].astype(BF16)
    zs = []
    for k in range(n_w):
        z = _dot(x, wbf_refs[k][...])
        if has_bias:
            z = z + b_refs[k][...]
        zs.append(z)
    if epilogue is None:
        out = zs[0]
    elif epilogue == "softplus":
        out = jax.nn.softplus(zs[0])
    elif epilogue == "glu":
        out = zs[0] * jax.nn.sigmoid(zs[1])
    elif epilogue == "swiglu":
        out = _silu(zs[0]) * zs[1]
    o_ref[...] = out.astype(o_ref.dtype)


def _matmul(x, ws, col_offsets, n_out, *, tn, tm, biases=None, epilogue=None,
            out_dtype=F32, name="matmul"):
    m, k = x.shape
    n_w = len(ws)
    has_bias = biases is not None
    in_specs = [pl.BlockSpec((tm, k), lambda j, i: (i, 0))]
    for off in col_offsets:
        in_specs.append(pl.BlockSpec((k, tn), lambda j, i, off=off: (0, j + off)))
    args = [x] + list(ws)
    if has_bias:
        for off in col_offsets:
            in_specs.append(pl.BlockSpec((1, tn), lambda j, i, off=off: (0, j + off)))
        args += [b.reshape(1, -1) for b in biases]
    return pl.pallas_call(
        functools.partial(_mm_kernel, n_w=n_w, has_bias=has_bias, epilogue=epilogue),
        out_shape=jax.ShapeDtypeStruct((m, n_out), out_dtype),
        grid=(n_out // tn, m // tm),
        in_specs=in_specs,
        out_specs=pl.BlockSpec((tm, tn), lambda j, i: (i, j)),
        scratch_shapes=[pltpu.VMEM((k, tn), BF16) for _ in range(n_w)],
        compiler_params=_cparams(2),
        name=name,
    )(*args)


SLABS = 8
ROW_TILE = 256


def _get_piece(ref, mode, j, rows, d):
    if mode == "slab":
        return ref[:, j * d:(j + 1) * d]
    return ref[j * rows:(j + 1) * rows, :]


def _put_piece(ref, mode, j, rows, d, val):
    if mode == "slab":
        ref[:, j * d:(j + 1) * d] = val.astype(ref.dtype)
    else:
        ref[j * rows:(j + 1) * rows, :] = val.astype(ref.dtype)


def _resnorm_kernel(*refs, has_y, write_x, x_mode, y_mode, xo_mode, v_mode, modulate, rows, d):
    it = iter(refs)
    x_ref = next(it)
    y_ref = next(it) if has_y else None
    g_ref = next(it) if has_y else None
    nw_ref = next(it)
    sh_ref = next(it) if modulate else None
    sc_ref = next(it) if modulate else None
    xo_ref = next(it) if write_x else None
    v_ref = next(it)
    nw = nw_ref[...]
    for j in range(SLABS):
        x = _get_piece(x_ref, x_mode, j, rows, d)
        if has_y:
            y = _get_piece(y_ref, y_mode, j, rows, d).astype(F32)
            x = x + g_ref[0] * y
            if write_x:
                _put_piece(xo_ref, xo_mode, j, rows, d, x)
        v = x * lax.rsqrt(jnp.mean(x * x, axis=-1, keepdims=True) + EPS) * nw
        if modulate:
            v = v * (1.0 + sc_ref[0]) + sh_ref[0]
        _put_piece(v_ref, v_mode, j, rows, d, v)


def _resnorm(x, y, gate, norm_w, shift, scale, *, n_batch, seq, x_mode="row",
             y_mode="row", xo_mode="row", v_mode="row", v_dtype=BF16, write_x=True,
             x_tile0=0, y_tile0=0, vec_index=None, name="resnorm"):
    d = x.shape[1]
    n = n_batch * seq
    if vec_index is None:
        vec_index = lambda b, t: b
    has_y = y is not None
    write_x = write_x and has_y
    modulate = shift is not None
    grows = seq // GRID_W
    tiles_per_seq = seq // ROW_TILE
    slab_used = "slab" in (x_mode, y_mode, xo_mode, v_mode)
    rows = grows if slab_used else ROW_TILE // SLABS
    if slab_used:
        assert grows * SLABS == ROW_TILE

    def spec(mode, tile0=0):
        if mode == "slab":
            return pl.BlockSpec((grows, SLABS * d), lambda b, t: (b, t))
        return pl.BlockSpec((ROW_TILE, d), lambda b, t: (b * tiles_per_seq + t + tile0, 0))

    def view(a, mode):
        return a.reshape(a.shape[0] // GRID_W, GRID_W * d) if mode == "slab" else a

    vec = pl.BlockSpec((1, 1, d), lambda b, t: (vec_index(b, t), 0, 0))
    in_specs = [spec(x_mode, x_tile0)]
    args = [view(x, x_mode)]
    if has_y:
        in_specs += [spec(y_mode, y_tile0), vec]
        args += [view(y, y_mode), gate]
    in_specs.append(pl.BlockSpec((1, d), lambda b, t: (0, 0)))
    args.append(norm_w.reshape(1, d))
    if modulate:
        in_specs += [vec, vec]
        args += [shift, scale]
    out_shape, out_specs = [], []
    def out_struct(mode, dtype):
        shape = (n // GRID_W, GRID_W * d) if mode == "slab" else (n, d)
        return jax.ShapeDtypeStruct(shape, dtype)

    if write_x:
        out_shape.append(out_struct(xo_mode, F32))
        out_specs.append(spec(xo_mode))
    out_shape.append(out_struct(v_mode, v_dtype))
    out_specs.append(spec(v_mode))
    outs = pl.pallas_call(
        functools.partial(_resnorm_kernel, has_y=has_y, write_x=write_x, x_mode=x_mode, y_mode=y_mode,
                          xo_mode=xo_mode, v_mode=v_mode, modulate=modulate, rows=rows, d=d),
        out_shape=out_shape,
        grid=(n_batch, tiles_per_seq),
        in_specs=in_specs,
        out_specs=out_specs,
        compiler_params=_cparams(2),
        name=name,
    )(*args)
    outs = [o.reshape(n, d) for o in outs]
    return (outs[0], outs[1]) if write_x else (None, outs[0])


S5_T = 128
S5_CB = 128
S5_NS = (S5_CB // S5_GROUP) * S5_STATE


def _reverse_rows(v_bf16):
    t = v_bf16.shape[0]
    r = lax.broadcasted_iota(jnp.int32, (t, t), 0)
    c = lax.broadcasted_iota(jnp.int32, (t, t), 1)
    flip = jnp.where(r + c == t - 1, 1.0, 0.0).astype(BF16)
    return _dot(flip, v_bf16)


def _s5_prep_kernel(xf_ref, xb_ref, nw_ref, sh_ref, sc_ref, o_ref, *, nb, d):
    nw = nw_ref[...]
    for b in range(nb):
        for rev, x_ref in ((False, xf_ref), (True, xb_ref)):
            x = x_ref[b]
            v = x * lax.rsqrt(jnp.mean(x * x, axis=-1, keepdims=True) + EPS) * nw
            v = (v * (1.0 + sc_ref[b]) + sh_ref[b]).astype(BF16)
            if rev:
                v = _reverse_rows(v).astype(BF16)
            k = b + (nb if rev else 0)
            o_ref[:, k * d:(k + 1) * d] = v


def _s5_prep(x3, norm_w, shift, scale):
    nb, s, d = x3.shape
    nt = s // S5_T
    return pl.pallas_call(
        functools.partial(_s5_prep_kernel, nb=nb, d=d),
        out_shape=jax.ShapeDtypeStruct((s, 2 * nb * d), BF16),
        grid=(nt,),
        in_specs=[pl.BlockSpec((nb, S5_T, d), lambda t: (0, t, 0)),
                  pl.BlockSpec((nb, S5_T, d), lambda t: (0, nt - 1 - t, 0)),
                  pl.BlockSpec((1, d), lambda t: (0, 0)),
                  pl.BlockSpec((nb, 1, d), lambda t: (0, 0, 0)),
                  pl.BlockSpec((nb, 1, d), lambda t: (0, 0, 0))],
        out_specs=pl.BlockSpec((S5_T, 2 * nb * d), lambda t: (t, 0)),
        compiler_params=_cparams(1),
        name="s5_prep",
    )(x3, x3, norm_w.reshape(1, d), shift, scale)


S5_BPS = 2


def _s5_scan_kernel(u_ref, bc_ref, are_ref, aim_ref, cc_ref, s0_ref, y_ref, sf_ref,
                    bu_ref, st_ref, *, n_tiles):
    i = pl.program_id(1)
    ns = S5_NS

    @pl.when(i == 0)
    def _():
        st_ref[...] = s0_ref[...]

    row = lax.broadcasted_iota(jnp.int32, (u_ref.shape[0], S5_CB), 0)
    is_fwd = jnp.bitwise_and(row, SUBLANES - 1) < (SUBLANES // 2)
    for blk in range(S5_BPS):
        u = u_ref[:, blk * S5_CB:(blk + 1) * S5_CB].astype(F32)
        lhs = jnp.concatenate([jnp.where(is_fwd, u, 0.0), jnp.where(is_fwd, 0.0, u)], axis=1)
        bu_ref[blk] = _dot(lhs.astype(BF16), bc_ref[blk])
    coef = [(are_ref[blk], aim_ref[blk]) for blk in range(S5_BPS)]

    def step(t, carry):
        r = pl.multiple_of(t * SUBLANES, SUBLANES)
        out = []
        for blk in range(S5_BPS):
            a_re, a_im = coef[blk]
            s_re, s_im = carry[2 * blk], carry[2 * blk + 1]
            n_re = a_re * s_re - a_im * s_im + bu_ref[blk, pl.ds(r, SUBLANES), 0:ns]
            n_im = a_re * s_im + a_im * s_re + bu_ref[blk, pl.ds(r, SUBLANES), ns:2 * ns]
            bu_ref[blk, pl.ds(r, SUBLANES), 0:ns] = n_re
            bu_ref[blk, pl.ds(r, SUBLANES), ns:2 * ns] = n_im
            out += [n_re, n_im]
        return tuple(out)

    init = []
    for blk in range(S5_BPS):
        init += [st_ref[blk, :, 0:ns], st_ref[blk, :, ns:2 * ns]]
    fin = lax.fori_loop(0, S5_T, step, tuple(init))
    for blk in range(S5_BPS):
        st_ref[blk, :, 0:ns] = fin[2 * blk]
        st_ref[blk, :, ns:2 * ns] = fin[2 * blk + 1]
        y2 = _dot(bu_ref[blk].astype(BF16), cc_ref[blk])
        y_ref[:, blk * S5_CB:(blk + 1) * S5_CB] = jnp.where(is_fwd, y2[:, 0:S5_CB],
                                                            y2[:, S5_CB:2 * S5_CB])

    @pl.when(i == n_tiles - 1)
    def _():
        sf_ref[...] = st_ref[...]


def _s5_scan(u2r, bcat, a_re, a_im, ccat, s0):
    rows, d = u2r.shape
    nblk = d // S5_CB
    tr = S5_T * SUBLANES
    n_tiles = rows // tr
    ns2 = 2 * S5_NS
    w = S5_BPS * S5_CB
    per_block = lambda j, i: (j, 0, 0)
    return pl.pallas_call(
        functools.partial(_s5_scan_kernel, n_tiles=n_tiles),
        out_shape=(jax.ShapeDtypeStruct((rows, d), F32),
                   jax.ShapeDtypeStruct((nblk, SUBLANES, ns2), F32)),
        grid=(nblk // S5_BPS, n_tiles),
        in_specs=[pl.BlockSpec((tr, w), lambda j, i: (i, j)),
                  pl.BlockSpec((S5_BPS, 2 * S5_CB, ns2), per_block),
                  pl.BlockSpec((S5_BPS, SUBLANES, S5_NS), per_block),
                  pl.BlockSpec((S5_BPS, SUBLANES, S5_NS), per_block),
                  pl.BlockSpec((S5_BPS, ns2, 2 * S5_CB), per_block),
                  pl.BlockSpec((S5_BPS, SUBLANES, ns2), per_block)],
        out_specs=(pl.BlockSpec((tr, w), lambda j, i: (i, j)),
                   pl.BlockSpec((S5_BPS, SUBLANES, ns2), per_block)),
        scratch_shapes=[pltpu.VMEM((S5_BPS, tr, ns2), F32),
                        pltpu.VMEM((S5_BPS, SUBLANES, ns2), F32)],
        compiler_params=_cparams(2),
        name="s5_scan",
    )(u2r, bcat, a_re, a_im, ccat, s0)


def _s5_out_kernel(yf_ref, yb_ref, u_ref, skip_ref, o_ref):
    yb = yb_ref[...]
    hi = yb.astype(BF16)
    lo = (yb - hi.astype(F32)).astype(BF16)
    y = (skip_ref[...] * u_ref[...].astype(F32) + yf_ref[...]
         + _reverse_rows(hi) + _reverse_rows(lo))
    o_ref[...] = jax.nn.gelu(y).astype(o_ref.dtype)


def _s5_out(y2, u2, skip, nb, d):
    s = y2.shape[0]
    nt = s // S5_T
    return pl.pallas_call(
        _s5_out_kernel,
        out_shape=jax.ShapeDtypeStruct((nb * s, d), BF16),
        grid=(nb, nt),
        in_specs=[pl.BlockSpec((S5_T, d), lambda b, t: (t, b)),
                  pl.BlockSpec((S5_T, d), lambda b, t: (nt - 1 - t, nb + b)),
                  pl.BlockSpec((S5_T, d), lambda b, t: (t, b)),
                  pl.BlockSpec((1, d), lambda b, t: (0, 0))],
        out_specs=pl.BlockSpec((S5_T, d), lambda b, t: (b * nt + t, 0)),
        compiler_params=_cparams(2),
        name="s5_out",
    )(y2, y2, u2, skip.reshape(1, d))


def _s5_pack_params(lam_re, lam_im, log_step, b_re, b_im, c_re, c_im, nb):
    f32 = F32
    g = lam_re.shape[1]
    gpb = S5_CB // S5_GROUP
    nblk = g // gpb
    eye = jnp.eye(gpb, dtype=f32)
    a_re_rows, a_im_rows, b_parts, c_parts = [], [], [], []
    for dr in range(2):
        lr, li = lam_re[dr].astype(f32), lam_im[dr].astype(f32)
        br, bi = b_re[dr].astype(f32), b_im[dr].astype(f32)
        dt = jnp.exp(log_step[dr].astype(f32))[:, None]
        mag = jnp.exp(lr * dt)
        abar_re = mag * jnp.cos(li * dt)
        abar_im = mag * jnp.sin(li * dt)
        num_re = abar_re - 1.0
        num_im = abar_im
        den = lr * lr + li * li
        f_re = (num_re * lr + num_im * li) / den
        f_im = (num_im * lr - num_re * li) / den
        bbar_re = f_re[..., None] * br - f_im[..., None] * bi
        bbar_im = f_re[..., None] * bi + f_im[..., None] * br
        a_re_rows.append(jnp.broadcast_to(abar_re.reshape(nblk, 1, S5_NS), (nblk, nb, S5_NS)))
        a_im_rows.append(jnp.broadcast_to(abar_im.reshape(nblk, 1, S5_NS), (nblk, nb, S5_NS)))

        def blockdiag_in(bb):
            b4 = bb.reshape(nblk, gpb, S5_STATE, S5_GROUP)
            return jnp.einsum('jgpk,gh->jgkhp', b4, eye).reshape(nblk, S5_CB, S5_NS)

        def blockdiag_out(cc):
            c4 = cc.astype(f32).reshape(nblk, gpb, S5_GROUP, S5_STATE)
            return jnp.einsum('jgkp,gh->jgphk', c4, eye).reshape(nblk, S5_NS, S5_CB)

        b_parts.append(jnp.concatenate([blockdiag_in(bbar_re), blockdiag_in(bbar_im)], axis=2))
        c_parts.append(jnp.concatenate([blockdiag_out(c_re[dr]), -blockdiag_out(c_im[dr])], axis=1))
    a_re = jnp.concatenate(a_re_rows, axis=1)
    a_im = jnp.concatenate(a_im_rows, axis=1)
    bcat = jnp.concatenate(b_parts, axis=1).astype(BF16)
    ccat = jnp.concatenate(c_parts, axis=2).astype(BF16)
    return bcat, a_re, a_im, ccat


def _conv_kernel(x_ref, w_ref, b_ref, o_ref, pad_ref, *, seq):
    halo = SUBLANES
    zeros = jnp.zeros((halo, pad_ref.shape[1]), F32)
    pad_ref[0:halo, :] = zeros
    pad_ref[halo + seq:2 * halo + seq, :] = zeros
    pad_ref[halo:halo + seq, :] = x_ref[...]
    acc = jnp.zeros((seq, pad_ref.shape[1]), F32) + b_ref[...]
    for k in range(SSD_CONV):
        off = halo + k - SSD_CONV // 2
        acc = acc + w_ref[k:k + 1, :] * pad_ref[off:off + seq, :]
    o_ref[...] = _silu(acc).astype(o_ref.dtype)


def _conv_silu(xbc, conv_w, conv_b, seq, row_block_offset, n_seq, out_dtype=BF16):
    _, c = xbc.shape
    tc = 512
    return pl.pallas_call(
        functools.partial(_conv_kernel, seq=seq),
        out_shape=jax.ShapeDtypeStruct((n_seq * seq, c), out_dtype),
        grid=(n_seq, c // tc),
        in_specs=[pl.BlockSpec((seq, tc), lambda b, j: (b + row_block_offset, j)),
                  pl.BlockSpec((SSD_CONV, tc), lambda b, j: (0, j)),
                  pl.BlockSpec((1, tc), lambda b, j: (0, j))],
        out_specs=pl.BlockSpec((seq, tc), lambda b, j: (b, j)),
        scratch_shapes=[pltpu.VMEM((seq + 2 * SUBLANES, tc), F32)],
        compiler_params=_cparams(2),
        name="ssd_conv",
    )(xbc, conv_w, conv_b.reshape(1, c))


HPG = 8
GCH = HPG * SSD_HEADDIM
DTC = 2 * HPG


def _ssd_kernel(xl_ref, btl_ref, cl_ref, dtl_ref, dttl_ref,
                xc_ref, btc_ref, dtc_ref, dttc_ref,
                z_ref, acol_ref, arow_ref, skip_ref, nw_ref, o_ref,
                htf_ref, htb_ref, yf_ref, yb_ref, *, n_lat, n_ctx):
    q = SSD_CHUNK
    r_i = lax.broadcasted_iota(jnp.int32, (q, q), 0)
    c_i = lax.broadcasted_iota(jnp.int32, (q, q), 1)
    tril = jnp.where(c_i <= r_i, 1.0, 0.0).astype(BF16)
    triu = jnp.where(r_i <= c_i, 1.0, 0.0).astype(BF16)
    lower = c_i <= r_i
    upper = c_i >= r_i
    lane = lax.broadcasted_iota(jnp.int32, (q, 2 * SSD_HEADDIM), 1)
    left = lane < SSD_HEADDIM
    e_r = lax.broadcasted_iota(jnp.int32, (DTC, GCH), 0)
    e_c = lax.broadcasted_iota(jnp.int32, (DTC, GCH), 1)
    a_row = arow_ref[0]
    a_col = acol_ref[0]

    def chunk(x_ref, bt_ref, c_ref, dt_ref, dtt_ref, r0, dr, ht_ref, y_ref):
        head_of_ch = lax.shift_right_logical(e_c, int(math.log2(SSD_HEADDIM)))
        expand = jnp.where(head_of_ch + dr * HPG == e_r, 1.0, 0.0).astype(BF16)
        x = x_ref[pl.ds(r0, q), :].astype(F32)
        bt = bt_ref[0, :, pl.ds(r0, q)]
        dt = dt_ref[0, pl.ds(r0, q), :]
        dtt = dtt_ref[0, :, pl.ds(r0, q)]
        la = dt * a_row
        lat = dtt * a_col
        cs = _dot_exact_lhs(tril, la)
        cst = _dot_exact_rhs(lat, triu)
        total = cs[q - 1:q, :]
        if dr == 1:
            rk, rkt = cs - la, cst - lat
            e_off, e_state = jnp.exp(total - rk), jnp.exp(rk)
        else:
            rk, rkt = cs, cst
            e_off, e_state = jnp.exp(rk), jnp.exp(total - rk)
        tot8 = jnp.broadcast_to(jnp.exp(total), (SUBLANES, DTC))
        pieces3 = [p.astype(F32) for p in _split3(tot8)]
        wo = _dot(jnp.concatenate([e_off] + pieces3, axis=0).astype(BF16), expand)
        w_off = wo[0:q, :]
        e_tot = (wo[q:q + 1, :] + wo[q + SUBLANES:q + SUBLANES + 1, :]
                 + wo[q + 2 * SUBLANES:q + 2 * SUBLANES + 1, :])
        w_state = _dot(e_state.astype(BF16), expand)
        dtx = _dot(dt.astype(BF16), expand)
        xd = x * dtx
        h_old = ht_ref[...]
        ht_ref[...] = h_old * e_tot + _dot(bt, (xd * w_state).astype(BF16))
        if y_ref is None:
            return
        cm = c_ref[pl.ds(r0, q), :]
        cb = _dot(cm, bt)
        xdb = xd.astype(BF16)
        pieces = []
        for pair in range(HPG // 2):
            ms = []
            for hh in (2 * pair, 2 * pair + 1):
                col = dr * HPG + hh
                colv = jnp.broadcast_to(rk[:, col:col + 1], (q, q))
                rowv = rkt[col:col + 1, :]
                if dr == 0:
                    seg = jnp.where(lower, colv - rowv, -1e30)
                else:
                    seg = jnp.where(upper, rowv - colv, -1e30)
                ms.append((cb * jnp.exp(seg)).astype(BF16))
            xp = xdb[:, pair * 2 * SSD_HEADDIM:(pair + 1) * 2 * SSD_HEADDIM]
            zero = jnp.zeros_like(xp)
            rhs = jnp.concatenate([jnp.where(left, xp, zero), jnp.where(left, zero, xp)], axis=0)
            pieces.append(_dot(jnp.concatenate(ms, axis=1), rhs))
        y_ref[pl.ds(r0, q), :] = (jnp.concatenate(pieces, axis=1)
                                  + _dot(cm, h_old.astype(BF16)) * w_off)

    htf_ref[...] = jnp.zeros_like(htf_ref)
    htb_ref[...] = jnp.zeros_like(htb_ref)

    def ctx_body(k, carry):
        rf = pl.multiple_of(k * q, q)
        rb = pl.multiple_of((n_ctx - 1 - k) * q, q)
        chunk(xc_ref, btc_ref, None, dtc_ref, dttc_ref, rf, 0, htf_ref, None)
        chunk(xc_ref, btc_ref, None, dtc_ref, dttc_ref, rb, 1, htb_ref, None)
        return carry

    lax.fori_loop(0, n_ctx, ctx_body, 0)

    def lat_body(k, carry):
        rf = pl.multiple_of(k * q, q)
        rb = pl.multiple_of((n_lat - 1 - k) * q, q)
        chunk(xl_ref, btl_ref, cl_ref, dtl_ref, dttl_ref, rf, 0, htf_ref, yf_ref)
        chunk(xl_ref, btl_ref, cl_ref, dtl_ref, dttl_ref, rb, 1, htb_ref, yb_ref)
        return carry

    lax.fori_loop(0, n_lat, lat_body, 0, unroll=2)

    def finish(k, carry):
        r0 = pl.multiple_of(k * q, q)
        x = xl_ref[pl.ds(r0, q), :].astype(F32)
        y = yf_ref[pl.ds(r0, q), :] + yb_ref[pl.ds(r0, q), :] + skip_ref[...] * x
        v = y * _silu(z_ref[pl.ds(r0, q), :].astype(F32))
        v = v * lax.rsqrt(jnp.mean(v * v, axis=-1, keepdims=True) + EPS) * nw_ref[...]
        o_ref[pl.ds(r0, q), :] = v.astype(o_ref.dtype)
        return carry

    lax.fori_loop(0, n_lat, finish, 0)


def _ssd_scan(xc_l, xc_c, bt_l, bt_c, z, dtg, dtgt, a_col, a_row, skip, norm_w, nb, seq, ctx_len):
    n_lat, n_ctx = seq // SSD_CHUNK, ctx_len // SSD_CHUNK
    inner = SSD_GROUPS * GCH
    coff = inner // SSD_STATE + SSD_GROUPS
    cb0 = nb * seq // ctx_len
    in_specs = [
        pl.BlockSpec((seq, GCH), lambda b, g: (b, g)),
        pl.BlockSpec((1, SSD_STATE, seq), lambda b, g: (g, 0, b)),
        pl.BlockSpec((seq, SSD_STATE), lambda b, g: (b, coff + g)),
        pl.BlockSpec((1, seq, DTC), lambda b, g: (g, b, 0)),
        pl.BlockSpec((1, DTC, seq), lambda b, g: (g, 0, b)),
        pl.BlockSpec((ctx_len, GCH), lambda b, g: (b, g)),
        pl.BlockSpec((1, SSD_STATE, ctx_len), lambda b, g: (g, 0, b)),
        pl.BlockSpec((1, ctx_len, DTC), lambda b, g: (g, cb0 + b, 0)),
        pl.BlockSpec((1, DTC, ctx_len), lambda b, g: (g, 0, cb0 + b)),
        pl.BlockSpec((seq, GCH), lambda b, g: (b, g)),
        pl.BlockSpec((1, DTC, 1), lambda b, g: (g, 0, 0)),
        pl.BlockSpec((1, 1, DTC), lambda b, g: (g, 0, 0)),
        pl.BlockSpec((1, GCH), lambda b, g: (0, g)),
        pl.BlockSpec((1, GCH), lambda b, g: (0, g)),
    ]
    return pl.pallas_call(
        functools.partial(_ssd_kernel, n_lat=n_lat, n_ctx=n_ctx),
        out_shape=jax.ShapeDtypeStruct((nb * seq, inner), BF16),
        grid=(nb, SSD_GROUPS),
        in_specs=in_specs,
        out_specs=pl.BlockSpec((seq, GCH), lambda b, g: (b, g)),
        scratch_shapes=[pltpu.VMEM((SSD_STATE, GCH), F32), pltpu.VMEM((SSD_STATE, GCH), F32),
                        pltpu.VMEM((seq, GCH), F32), pltpu.VMEM((seq, GCH), F32)],
        compiler_params=_cparams(2),
        name="ssd_scan",
    )(xc_l, bt_l, xc_l, dtg, dtgt, xc_c, bt_c, dtg, dtgt, z, a_col, a_row, skip, norm_w)


def _expert_kernel(be_ref, nu_ref, first_ref, slot_ref, next_ref, x_ref, wg_hbm, wu_hbm, wd_hbm,
                   o_ref, stage_g, stage_u, stage_d, wgb, wub, wdb, sem, *, layer):
    i = pl.program_id(0)

    def weight_copies(e, s):
        return (pltpu.make_async_copy(wg_hbm.at[layer, e], stage_g.at[s], sem.at[s, 0]),
                pltpu.make_async_copy(wu_hbm.at[layer, e], stage_u.at[s], sem.at[s, 1]),
                pltpu.make_async_copy(wd_hbm.at[layer, e], stage_d.at[s], sem.at[s, 2]))

    @pl.when(i == 0)
    def _():
        for c in weight_copies(be_ref[0], 0):
            c.start()

    @pl.when(jnp.logical_and(i < nu_ref[0], first_ref[i] == 1))
    def _():
        s = slot_ref[i]

        @pl.when(next_ref[i] >= 0)
        def _():
            for c in weight_copies(next_ref[i], 1 - s):
                c.start()

        for c in weight_copies(be_ref[i], s):
            c.wait()
        wgb[...] = stage_g[s].astype(BF16)
        wub[...] = stage_u[s].astype(BF16)
        wdb[...] = stage_d[s].astype(BF16)

    @pl.when(i < nu_ref[0])
    def _():
        lo, hi = _unpack_rows(x_ref[...])
        x = jnp.concatenate([lo.astype(BF16), hi.astype(BF16)], axis=1)
        h = _silu(_dot(x, wgb[...])) * _dot(x, wub[...])
        o_ref[...] = _pack_rows(_dot(h.astype(BF16), wdb[...]))

    @pl.when(i >= nu_ref[0])
    def _():
        o_ref[...] = jnp.zeros_like(o_ref)


def _experts(x_sorted, block_e, n_used, first, slot, next_e, w_gate, w_up, w_down, layer):
    cap, d = x_sorted.shape
    ff = w_gate.shape[3]
    n_blocks = cap // EXPERT_BLOCK

    def row_block(i, be, nu, fi, sl, nx):
        return (jnp.minimum(i, nu[0] - 1), 0)

    hbm = pl.BlockSpec(memory_space=pl.ANY)
    grid_spec = pltpu.PrefetchScalarGridSpec(
        num_scalar_prefetch=5,
        grid=(n_blocks,),
        in_specs=[pl.BlockSpec((EXPERT_BLOCK, d), row_block), hbm, hbm, hbm],
        out_specs=pl.BlockSpec((EXPERT_BLOCK, d), lambda i, be, nu, fi, sl, nx: (i, 0)),
        scratch_shapes=[pltpu.VMEM((2, 2 * d, ff), F32), pltpu.VMEM((2, 2 * d, ff), F32),
                        pltpu.VMEM((2, ff, 2 * d), F32),
                        pltpu.VMEM((2 * d, ff), BF16), pltpu.VMEM((2 * d, ff), BF16),
                        pltpu.VMEM((ff, 2 * d), BF16), pltpu.SemaphoreType.DMA((2, 3))])
    return pl.pallas_call(
        functools.partial(_expert_kernel, layer=layer),
        out_shape=jax.ShapeDtypeStruct((cap, d), jnp.uint32),
        grid_spec=grid_spec,
        compiler_params=_cparams(1),
        name="moe_experts",
    )(block_e, n_used, first, slot, next_e, x_sorted, w_gate, w_up, w_down)


DISPATCH_T = 256
COMBINE_T = 128


HIGH_HALF = 0xFFFF0000


def _pack_rows(a):
    half = a.shape[1] // 2
    lo = lax.bitcast_convert_type(a[:, :half].astype(BF16).astype(F32), jnp.uint32)
    hi = lax.bitcast_convert_type(a[:, half:].astype(BF16).astype(F32), jnp.uint32)
    return jnp.bitwise_or(jnp.bitwise_and(hi, jnp.uint32(HIGH_HALF)),
                          lax.shift_right_logical(lo, jnp.uint32(16)))


def _unpack_rows(w):
    lo = lax.bitcast_convert_type(lax.shift_left(w, jnp.uint32(16)), F32)
    hi = lax.bitcast_convert_type(jnp.bitwise_and(w, jnp.uint32(HIGH_HALF)), F32)
    return lo, hi


def _dispatch_kernel(pe_ref, pd_ref, dest_ref, v_ref, xs_ref, zero_buf, row_buf, sem):
    t_rows = v_ref.shape[0]
    row_buf[...] = _pack_rows(v_ref[...])

    @pl.when(pl.program_id(0) == 0)
    def _():
        zero_buf[...] = jnp.zeros_like(zero_buf)

        def fill(e, carry):
            @pl.when(pd_ref[e] > 0)
            def _():
                start = pl.multiple_of(pe_ref[e] - EXPERT_BLOCK, EXPERT_BLOCK)
                pltpu.make_async_copy(zero_buf, xs_ref.at[pl.ds(start, EXPERT_BLOCK)], sem).start()
            return carry

        def drain(e, carry):
            @pl.when(pd_ref[e] > 0)
            def _():
                pltpu.make_async_copy(zero_buf, xs_ref.at[pl.ds(0, EXPERT_BLOCK)], sem).wait()
            return carry

        lax.fori_loop(0, N_EXPERTS, fill, 0)
        lax.fori_loop(0, N_EXPERTS, drain, 0)

        n_blocks = xs_ref.shape[0] // EXPERT_BLOCK
        first_unused = pe_ref[N_EXPERTS - 1] // EXPERT_BLOCK

        def fill_tail(b, carry):
            start = pl.multiple_of(b * EXPERT_BLOCK, EXPERT_BLOCK)
            pltpu.make_async_copy(zero_buf, xs_ref.at[pl.ds(start, EXPERT_BLOCK)], sem).start()
            return carry

        def drain_tail(b, carry):
            pltpu.make_async_copy(zero_buf, xs_ref.at[pl.ds(0, EXPERT_BLOCK)], sem).wait()
            return carry

        lax.fori_loop(first_unused, n_blocks, fill_tail, 0)
        lax.fori_loop(first_unused, n_blocks, drain_tail, 0)

    def issue(t, carry):
        for k in range(TOP_K):
            pltpu.make_async_copy(row_buf.at[pl.ds(t, 1)],
                                  xs_ref.at[pl.ds(dest_ref[k, t], 1)], sem).start(priority=k % 2)
        return carry

    lax.fori_loop(0, t_rows, issue, 0)
    for k in range(TOP_K):
        pltpu.make_async_copy(row_buf, xs_ref.at[pl.ds(0, t_rows)], sem).wait()


def _dispatch(v, dest, pad_end, padded, cap):
    n, d = v.shape
    grid_spec = pltpu.PrefetchScalarGridSpec(
        num_scalar_prefetch=2,
        grid=(n // DISPATCH_T,),
        in_specs=[pl.BlockSpec((TOP_K, DISPATCH_T), lambda i, pe, pd: (0, i),
                               memory_space=pltpu.SMEM),
                  pl.BlockSpec((DISPATCH_T, d), lambda i, pe, pd: (i, 0))],
        out_specs=pl.BlockSpec(memory_space=pl.ANY),
        scratch_shapes=[pltpu.VMEM((EXPERT_BLOCK, d // 2), jnp.uint32),
                        pltpu.VMEM((DISPATCH_T, d // 2), jnp.uint32),
                        pltpu.SemaphoreType.DMA(())])
    return pl.pallas_call(
        _dispatch_kernel,
        out_shape=jax.ShapeDtypeStruct((cap, d // 2), jnp.uint32),
        grid_spec=grid_spec,
        compiler_params=_cparams(1),
        name="moe_dispatch",
    )(pad_end, padded, dest, v)


def _combine_kernel(dest_ref, next_dest_ref, y_ref, gate_ref, sh_ref, o_ref, buf, sem, *, n_tiles):
    i = pl.program_id(0)
    t_rows = o_ref.shape[0]
    slot = i % 2

    def issue_tile(idx_ref, s):
        def issue(t, carry):
            for k in range(TOP_K):
                pltpu.make_async_copy(y_ref.at[pl.ds(idx_ref[k, t], 1)],
                                      buf.at[s, k, pl.ds(t, 1)], sem.at[s]).start(priority=k % 2)
            return carry

        lax.fori_loop(0, t_rows, issue, 0)

    @pl.when(i == 0)
    def _():
        issue_tile(dest_ref, 0)

    @pl.when(i + 1 < n_tiles)
    def _():
        issue_tile(next_dest_ref, 1 - slot)

    for k in range(TOP_K):
        pltpu.make_async_copy(y_ref.at[pl.ds(0, t_rows)], buf.at[slot, k], sem.at[slot]).wait()
    half = o_ref.shape[1] // 2
    acc_lo = sh_ref[:, :half]
    acc_hi = sh_ref[:, half:]
    for k in range(TOP_K):
        lo, hi = _unpack_rows(buf[slot, k])
        g = gate_ref[:, k:k + 1]
        acc_lo = acc_lo + g * lo
        acc_hi = acc_hi + g * hi
    o_ref[:, :half] = acc_lo
    o_ref[:, half:] = acc_hi


def _combine(y_sorted, dest, gate_t, shared):
    n, d = shared.shape
    n_tiles = n // COMBINE_T
    idx_block = (TOP_K, COMBINE_T)
    return pl.pallas_call(
        functools.partial(_combine_kernel, n_tiles=n_tiles),
        out_shape=jax.ShapeDtypeStruct((n, d), F32),
        grid=(n_tiles,),
        in_specs=[pl.BlockSpec(idx_block, lambda i: (0, i), memory_space=pltpu.SMEM),
                  pl.BlockSpec(idx_block, lambda i: (0, jnp.minimum(i + 1, n_tiles - 1)),
                               memory_space=pltpu.SMEM),
                  pl.BlockSpec(memory_space=pl.ANY),
                  pl.BlockSpec((COMBINE_T, TOP_K), lambda i: (i, 0)),
                  pl.BlockSpec((COMBINE_T, d), lambda i: (i, 0))],
        out_specs=pl.BlockSpec((COMBINE_T, d), lambda i: (i, 0)),
        scratch_shapes=[pltpu.VMEM((2, TOP_K, COMBINE_T, d // 2), jnp.uint32),
                        pltpu.SemaphoreType.DMA((2,))],
        compiler_params=_cparams(1),
        name="moe_combine",
    )(dest, dest, y_sorted, gate_t, shared)


ROUTE_T = 512
GROUP_SIZE = N_EXPERTS // N_EXPERT_GROUPS


def _route_kernel(v_ref, rwh_ref, rwl_ref, bias_ref, eid_ref, rank_ref, gate_ref, cnt_ref,
                  carry_ref, *, n_tiles):
    i = pl.program_id(0)
    t = v_ref.shape[0]
    ng, gs = N_EXPERT_GROUPS, GROUP_SIZE
    neg = -jnp.inf

    @pl.when(i == 0)
    def _():
        carry_ref[...] = jnp.zeros_like(carry_ref)

    v = v_ref[...].astype(BF16)
    nt_dims = (((1,), (1,)), ((), ()))
    logits = (lax.dot_general(rwh_ref[...], v, nt_dims, preferred_element_type=F32)
              + lax.dot_general(rwl_ref[...], v, nt_dims, preferred_element_type=F32))
    scores = jax.nn.sigmoid(logits)
    biased = scores + bias_ref[:, 0:1]
    x3 = biased.reshape(ng, gs, t)
    s3 = scores.reshape(ng, gs, t)
    mi = lax.broadcasted_iota(jnp.int32, (ng, gs, t), 1).astype(F32)
    fi = lax.broadcasted_iota(jnp.int32, (ng, gs, t), 0).astype(F32) * gs + mi
    gi = lax.broadcasted_iota(jnp.int32, (ng, 1, t), 0).astype(F32)

    m1 = jnp.max(x3, axis=1, keepdims=True)
    i1 = jnp.min(jnp.where(x3 == m1, mi, float(gs)), axis=1, keepdims=True)
    m2 = jnp.max(jnp.where(mi == i1, neg, x3), axis=1, keepdims=True)
    cur = m1 + m2
    gmask = jnp.zeros((ng, 1, t), F32)
    for _ in range(TOPK_GROUPS):
        gm = jnp.max(cur, axis=0, keepdims=True)
        idx = jnp.min(jnp.where(cur == gm, gi, float(ng)), axis=0, keepdims=True)
        hit = gi == idx
        gmask = jnp.where(hit, 1.0, gmask)
        cur = jnp.where(hit, neg, cur)

    cand = jnp.where(gmask > 0.0, x3, neg)
    sel = jnp.zeros((ng, gs, t), F32)
    eids = []
    for _ in range(TOP_K):
        m = jnp.max(jnp.max(cand, axis=1, keepdims=True), axis=0, keepdims=True)
        idx = jnp.where(cand == m, fi, float(N_EXPERTS))
        idx = jnp.min(jnp.min(idx, axis=1, keepdims=True), axis=0, keepdims=True)
        hit = fi == idx
        sel = jnp.where(hit, 1.0, sel)
        cand = jnp.where(hit, neg, cand)
        eids.append(idx)

    selr = sel.reshape(N_EXPERTS, t)
    r_i = lax.broadcasted_iota(jnp.int32, (t, t), 0)
    c_i = lax.broadcasted_iota(jnp.int32, (t, t), 1)
    before = jnp.where(r_i < c_i, 1.0, 0.0).astype(BF16)
    rank = _dot(selr.astype(BF16), before) + carry_ref[:, 0:1]
    carry_ref[...] = carry_ref[...] + jnp.sum(selr, axis=1, keepdims=True)
    rank3 = rank.reshape(ng, gs, t)

    gsel = sel * s3
    denom = jnp.sum(jnp.sum(gsel, axis=1, keepdims=True), axis=0, keepdims=True)
    gate3 = gsel / denom * ROUTED_SCALE

    def pick(a3, hit):
        return jnp.sum(jnp.sum(jnp.where(hit, a3, 0.0), axis=1, keepdims=True), axis=0,
                       keepdims=True).reshape(1, t)

    for k in range(TOP_K):
        hit = fi == eids[k]
        eid_ref[k:k + 1, :] = eids[k].reshape(1, t).astype(jnp.int32)
        rank_ref[k:k + 1, :] = pick(rank3, hit).astype(jnp.int32)
        gate_ref[k:k + 1, :] = pick(gate3, hit)

    @pl.when(i == n_tiles - 1)
    def _():
        cnt_ref[...] = carry_ref[...]


def _route(v, router_w, router_bias):
    n, d = v.shape
    n_tiles = n // ROUTE_T
    rwt = router_w.astype(F32).T
    rwh = rwt.astype(BF16)
    rwl = (rwt - rwh.astype(F32)).astype(BF16)
    bias = jnp.broadcast_to(router_bias.astype(F32)[:, None], (N_EXPERTS, LANES))
    slot = pl.BlockSpec((TOP_K, ROUTE_T), lambda i: (0, i))
    full = pl.BlockSpec((N_EXPERTS, d), lambda i: (0, 0))
    return pl.pallas_call(
        functools.partial(_route_kernel, n_tiles=n_tiles),
        out_shape=(jax.ShapeDtypeStruct((TOP_K, n), jnp.int32),
                   jax.ShapeDtypeStruct((TOP_K, n), jnp.int32),
                   jax.ShapeDtypeStruct((TOP_K, n), F32),
                   jax.ShapeDtypeStruct((N_EXPERTS, LANES), F32)),
        grid=(n_tiles,),
        in_specs=[pl.BlockSpec((ROUTE_T, d), lambda i: (i, 0)), full, full,
                  pl.BlockSpec((N_EXPERTS, LANES), lambda i: (0, 0))],
        out_specs=(slot, slot, slot, pl.BlockSpec((N_EXPERTS, LANES), lambda i: (0, 0))),
        scratch_shapes=[pltpu.VMEM((N_EXPERTS, LANES), F32)],
        compiler_params=_cparams(1),
        name="moe_route",
    )(v, rwh, rwl, bias)


def _slot_rows_kernel(ps_ref, eid_ref, rank_ref, o_ref):
    eid = eid_ref[...]
    acc = rank_ref[...]
    for e in range(N_EXPERTS):
        acc = acc + jnp.where(eid == e, ps_ref[e], 0)
    o_ref[...] = acc


def _slot_rows(eid, rank, pad_start):
    k, n = eid.shape
    whole = lambda i, ps: (0, 0)
    grid_spec = pltpu.PrefetchScalarGridSpec(
        num_scalar_prefetch=1, grid=(1,),
        in_specs=[pl.BlockSpec((k, n), whole), pl.BlockSpec((k, n), whole)],
        out_specs=pl.BlockSpec((k, n), whole))
    return pl.pallas_call(
        _slot_rows_kernel,
        out_shape=jax.ShapeDtypeStruct((k, n), jnp.int32),
        grid_spec=grid_spec,
        compiler_params=_cparams(1),
        name="moe_slot_rows",
    )(pad_start, eid, rank)


def _moe(v, layer, router_w, router_bias, w_gate, w_up, w_down, sw_gate, sw_up, sw_down):
    n, d = v.shape
    eid, rank, gate, cnt = _route(v, router_w, router_bias)
    nk = n * TOP_K
    cap = -(-nk // EXPERT_BLOCK) * EXPERT_BLOCK + N_EXPERTS * EXPERT_BLOCK
    n_blocks = cap // EXPERT_BLOCK
    counts = cnt[:, 0].astype(jnp.int32)
    padded = (counts + EXPERT_BLOCK - 1) // EXPERT_BLOCK * EXPERT_BLOCK
    pad_end = jnp.cumsum(padded)
    pad_start = pad_end - padded
    dest = _slot_rows(eid, rank, pad_start.astype(jnp.int32))
    block_start = jnp.arange(n_blocks, dtype=jnp.int32) * EXPERT_BLOCK
    block_e = jnp.minimum(jnp.sum((pad_end[None, :] <= block_start[:, None]).astype(jnp.int32), axis=1),
                          N_EXPERTS - 1).astype(jnp.int32)
    n_used = (pad_end[-1] // EXPERT_BLOCK).astype(jnp.int32).reshape(1)
    block_id = jnp.arange(n_blocks, dtype=jnp.int32)
    first = ((pad_start[block_e] == block_start) & (block_id < n_used[0])).astype(jnp.int32)
    slot = ((jnp.cumsum(first) - 1) % 2).astype(jnp.int32)
    experts = jnp.arange(N_EXPERTS, dtype=jnp.int32)
    later = (experts[None, :] > experts[:, None]) & (padded[None, :] > 0)
    next_of = jnp.min(jnp.where(later, experts[None, :], N_EXPERTS), axis=1)
    next_of = jnp.where(next_of == N_EXPERTS, -1, next_of).astype(jnp.int32)
    next_e = next_of[block_e]
    x_sorted = _dispatch(v, dest, pad_end.astype(jnp.int32), padded.astype(jnp.int32), cap)
    y_sorted = _experts(x_sorted, block_e, n_used, first, slot, next_e, w_gate, w_up, w_down, layer)
    hs = _matmul(v, [sw_gate, sw_up], [0, 0], sw_gate.shape[1], tn=sw_gate.shape[1], tm=512,
                 epilogue="swiglu", out_dtype=BF16, name="shared_up")
    shared = _matmul(hs, [sw_down], [0], d, tn=1024, tm=512, name="shared_down")
    return _combine(y_sorted, dest, gate.T, shared)


def kernel(x, c, ctx, c_ctx, ada_w, ada_b, norm1_w, norm2_w, s5_lambda_re, s5_lambda_im, s5_log_step, s5_b_re, s5_b_im, s5_c_re, s5_c_im, s5_d, s5_glu_w, s5_glu_b, ssd_in_w, ssd_conv_w, ssd_conv_b, ssd_dt_bias, ssd_a_log, ssd_d, ssd_norm_w, ssd_out_w, moe_router_w, moe_router_bias, moe_w_gate, moe_w_up, moe_w_down, shared_w_gate, shared_w_up, shared_w_down, final_norm_w):
    nb, seq, d = x.shape
    ctx_len = ctx.shape[1]
    n_lat = nb * seq
    n_ctx = nb * ctx_len

    cond = jnp.concatenate([c, c_ctx[None, :], jnp.zeros((SUBLANES - nb - 1, d), F32)], axis=0)
    mods = _ada(cond, ada_w, ada_b)

    def mod_vecs(layer, k):
        m = mods[layer, :, k * d:(k + 1) * d]
        lat = m[:nb].reshape(nb, 1, d)
        cx = jnp.broadcast_to(m[nb].reshape(1, 1, d), (nb, 1, d))
        return lat, cx

    x_lat = x.reshape(n_lat, d)
    x_ctx = ctx.reshape(n_ctx, d)

    sh_l, sh_c = mod_vecs(0, 0)
    sc_l, sc_c = mod_vecs(0, 1)
    u2_c = _s5_prep(ctx, norm1_w[0], sh_c, sc_c)
    u2_l = _s5_prep(x, norm1_w[0], sh_l, sc_l)
    bcat, a_re, a_im, ccat = _s5_pack_params(
        s5_lambda_re[0], s5_lambda_im[0], s5_log_step[0], s5_b_re[0], s5_b_im[0],
        s5_c_re[0], s5_c_im[0], nb)
    nblk = d // S5_CB
    s0 = jnp.zeros((nblk, SUBLANES, 2 * S5_NS), F32)
    y2_c, s_ctx = _s5_scan(u2_c.reshape(ctx_len * 2 * nb, d), bcat, a_re, a_im, ccat, s0)
    y2_l, _ = _s5_scan(u2_l.reshape(seq * 2 * nb, d), bcat, a_re, a_im, ccat, s_ctx)
    g_l = _s5_out(y2_l.reshape(seq, 2 * nb * d), u2_l, s5_d[0], nb, d)
    g_c = _s5_out(y2_c.reshape(ctx_len, 2 * nb * d), u2_c, s5_d[0], nb, d)
    g_all = jnp.concatenate([g_l, g_c], axis=0)
    half = s5_glu_w.shape[2] // 2
    tn = 1024
    glu = _matmul(g_all, [s5_glu_w[0], s5_glu_w[0]], [0, half // tn], half, tn=tn, tm=512,
                  biases=[s5_glu_b[0], s5_glu_b[0]], epilogue="glu", out_dtype=BF16, name="s5_glu")

    n_all = n_lat + n_ctx
    lat_tiles = seq // ROW_TILE

    def all_vecs(layer, k):
        return mods[layer, :nb + 1, k * d:(k + 1) * d].reshape(nb + 1, 1, d)

    def all_index(b, t):
        return jnp.minimum(t // lat_tiles, nb)

    xs0 = jnp.concatenate([x_lat, x_ctx], axis=0)
    xs1, v_all = _resnorm(xs0, glu, all_vecs(0, 2), norm2_w[0], all_vecs(0, 3), all_vecs(0, 4),
                          n_batch=1, seq=n_all, vec_index=all_index, v_dtype=F32)
    moe0 = _moe(v_all, 0, moe_router_w[0], moe_router_bias[0], moe_w_gate, moe_w_up,
                moe_w_down, shared_w_gate[0], shared_w_up[0], shared_w_down[0])

    g5_l, g5_c = mod_vecs(0, 5)
    sh_l, sh_c = mod_vecs(1, 0)
    sc_l, sc_c = mod_vecs(1, 1)
    x2_l, u_l = _resnorm(xs1, moe0, g5_l, norm1_w[1], sh_l, sc_l, n_batch=nb, seq=seq,
                         x_mode="slab", y_mode="slab", xo_mode="row", v_mode="row")
    ctx_tile0 = n_lat // ROW_TILE
    _, u_c = _resnorm(xs1, moe0, g5_c, norm1_w[1], sh_c, sc_c, n_batch=nb, seq=ctx_len,
                      write_x=False, x_tile0=ctx_tile0, y_tile0=ctx_tile0)
    u_all = jnp.concatenate([u_l, u_c], axis=0)
    in_w = ssd_in_w[0]
    inner = ssd_out_w.shape[1]
    conv_dim = ssd_conv_w.shape[2]
    heads = inner // SSD_HEADDIM
    tn = 1024
    z_all = _matmul(u_all, [in_w], [0], inner, tn=tn, tm=512, out_dtype=BF16, name="ssd_in_z")
    xbc = _matmul(u_all, [in_w], [inner // tn], conv_dim, tn=tn, tm=512, name="ssd_in_xbc")
    dt_bias = ssd_dt_bias[0].reshape(-1)
    dt_all = _matmul(u_all, [in_w], [(inner + conv_dim) // LANES], 2 * heads, tn=LANES, tm=512,
                     biases=[jnp.pad(dt_bias, (inner + conv_dim, 0))], epilogue="softplus",
                     name="ssd_in_dt")
    xc_l = _conv_silu(xbc, ssd_conv_w[0], ssd_conv_b[0], seq, 0, nb)
    xc_c = _conv_silu(xbc, ssd_conv_w[0], ssd_conv_b[0], ctx_len, n_lat // ctx_len, nb)
    rows = n_lat + n_ctx

    def b_transposed(xc):
        bm = xc[:, inner:inner + SSD_GROUPS * SSD_STATE]
        return bm.reshape(xc.shape[0], SSD_GROUPS, SSD_STATE).transpose(1, 2, 0)

    dtg = dt_all.reshape(rows, 2, SSD_GROUPS, HPG).transpose(2, 0, 1, 3).reshape(SSD_GROUPS, rows, DTC)
    dtgt = dtg.transpose(0, 2, 1)
    a = -jnp.exp(ssd_a_log[0].astype(F32))
    a_g = a.reshape(2, SSD_GROUPS, HPG).transpose(1, 0, 2).reshape(SSD_GROUPS, DTC)
    skip = jnp.repeat(ssd_d[0].astype(F32), SSD_HEADDIM).reshape(1, inner)
    yn = _ssd_scan(xc_l, xc_c, b_transposed(xc_l), b_transposed(xc_c), z_all, dtg, dtgt,
                   a_g.reshape(SSD_GROUPS, DTC, 1),
                   a_g.reshape(SSD_GROUPS, 1, DTC), skip, ssd_norm_w[0].reshape(1, inner),
                   nb, seq, ctx_len)
    y_lat = _matmul(yn, [ssd_out_w[0]], [0], d, tn=512, tm=512, out_dtype=BF16, name="ssd_out")

    g2_l, _ = mod_vecs(1, 2)
    sh4_l, _ = mod_vecs(1, 3)
    sc4_l, _ = mod_vecs(1, 4)
    x3_l, v_l = _resnorm(x2_l, y_lat, g2_l, norm2_w[1], sh4_l, sc4_l, n_batch=nb, seq=seq,
                         v_dtype=F32)
    moe1 = _moe(v_l, 1, moe_router_w[1], moe_router_bias[1], moe_w_gate, moe_w_up,
                moe_w_down, shared_w_gate[1], shared_w_up[1], shared_w_down[1])
    g5_l, _ = mod_vecs(1, 5)
    _, out = _resnorm(x3_l, moe1, g5_l, final_norm_w, None, None, n_batch=nb, seq=seq, v_dtype=F32,
                      write_x=False, v_mode="slab")
    return out.reshape(nb, seq, d)
```

```python
import functools
import math

import jax
import jax.numpy as jnp
from jax import lax
from jax.experimental import pallas as pl
from jax.experimental.pallas import tpu as pltpu

F32 = jnp.float32
BF16 = jnp.bfloat16

GRID_W = 64
EPS = 1e-6
S5_GROUP = 16
S5_STATE = 64
SSD_HEADDIM = 64
SSD_STATE = 128
SSD_GROUPS = 8
SSD_CONV = 5
SSD_CHUNK = 128
N_EXPERTS = 64
TOP_K = 8
N_EXPERT_GROUPS = 8
TOPK_GROUPS = 4
ROUTED_SCALE = 2.5
EXPERT_BLOCK = 256

VMEM_LIMIT_BYTES = 56 * 1024 * 1024
LANES = 128
SUBLANES = 8


def _cparams(n_axes):
    return pltpu.CompilerParams(
        dimension_semantics=("arbitrary",) * n_axes,
        vmem_limit_bytes=VMEM_LIMIT_BYTES)


def _silu(v):
    return v * jax.nn.sigmoid(v)


def _dot(a, b):
    return jnp.dot(a, b, preferred_element_type=F32)


def _split3(a):
    hi = a.astype(BF16)
    r1 = a - hi.astype(F32)
    mid = r1.astype(BF16)
    lo = (r1 - mid.astype(F32)).astype(BF16)
    return hi, mid, lo


def _dot_exact_rhs(a, sel):
    hi, mid, lo = _split3(a)
    return _dot(hi, sel) + _dot(mid, sel) + _dot(lo, sel)


def _dot_exact_lhs(sel, a):
    hi, mid, lo = _split3(a)
    return _dot(sel, hi) + _dot(sel, mid) + _dot(sel, lo)


def _ada_kernel(c_ref, w_ref, b_ref, o_ref):
    c = _silu(c_ref[...])
    o_ref[0] = _dot(c.astype(BF16), w_ref[0].astype(BF16)) + b_ref[0]


def _ada(cond, ada_w, ada_b):
    depth, d, n = ada_w.shape
    tn = 1024
    rows = cond.shape[0]
    return pl.pallas_call(
        _ada_kernel,
        out_shape=jax.ShapeDtypeStruct((depth, rows, n), F32),
        grid=(depth, n // tn),
        in_specs=[pl.BlockSpec((rows, d), lambda l, j: (0, 0)),
                  pl.BlockSpec((1, d, tn), lambda l, j: (l, 0, j)),
                  pl.BlockSpec((1, 1, tn), lambda l, j: (l, 0, j))],
        out_specs=pl.BlockSpec((1, rows, tn), lambda l, j: (l, 0, j)),
        compiler_params=_cparams(2),
        name="ada",
    )(cond, ada_w, ada_b.reshape(depth, 1, n))


def _mm_kernel(*refs, n_w, has_bias, epilogue):
    x_ref = refs[0]
    w_refs = refs[1:1 + n_w]
    pos = 1 + n_w
    b_refs = refs[pos:pos + n_w] if has_bias else ()
    pos += n_w if has_bias else 0
    o_ref = refs[pos]
    wbf_refs = refs[pos + 1:pos + 1 + n_w]

    @pl.when(pl.program_id(1) == 0)
    def _():
        for w_ref, wbf in zip(w_refs, wbf_refs):
            wbf[...] = w_ref[...].astype(BF16)

    x = x_ref[...].astype(BF16)
    zs = []
    for k in range(n_w):
        z = _dot(x, wbf_refs[k][...])
        if has_bias:
            z = z + b_refs[k][...]
        zs.append(z)
    if epilogue is None:
        out = zs[0]
    elif epilogue == "softplus":
        out = jax.nn.softplus(zs[0])
    elif epilogue == "glu":
        out = zs[0] * jax.nn.sigmoid(zs[1])
    elif epilogue == "swiglu":
        out = _silu(zs[0]) * zs[1]
    o_ref[...] = out.astype(o_ref.dtype)


def _matmul(x, ws, col_offsets, n_out, *, tn, tm, biases=None, epilogue=None,
            out_dtype=F32, name="matmul"):
    m, k = x.shape
    n_w = len(ws)
    has_bias = biases is not None
    in_specs = [pl.BlockSpec((tm, k), lambda j, i: (i, 0))]
    for off in col_offsets:
        in_specs.append(pl.BlockSpec((k, tn), lambda j, i, off=off: (0, j + off)))
    args = [x] + list(ws)
    if has_bias:
        for off in col_offsets:
            in_specs.append(pl.BlockSpec((1, tn), lambda j, i, off=off: (0, j + off)))
        args += [b.reshape(1, -1) for b in biases]
    return pl.pallas_call(
        functools.partial(_mm_kernel, n_w=n_w, has_bias=has_bias, epilogue=epilogue),
        out_shape=jax.ShapeDtypeStruct((m, n_out), out_dtype),
        grid=(n_out // tn, m // tm),
        in_specs=in_specs,
        out_specs=pl.BlockSpec((tm, tn), lambda j, i: (i, j)),
        scratch_shapes=[pltpu.VMEM((k, tn), BF16) for _ in range(n_w)],
        compiler_params=_cparams(2),
        name=name,
    )(*args)


SLABS = 8
ROW_TILE = 256


def _get_piece(ref, mode, j, rows, d):
    if mode == "slab":
        return ref[:, j * d:(j + 1) * d]
    return ref[j * rows:(j + 1) * rows, :]


def _put_piece(ref, mode, j, rows, d, val):
    if mode == "slab":
        ref[:, j * d:(j + 1) * d] = val.astype(ref.dtype)
    else:
        ref[j * rows:(j + 1) * rows, :] = val.astype(ref.dtype)


def _resnorm_kernel(*refs, has_y, write_x, x_mode, y_mode, xo_mode, v_mode, modulate, rows, d):
    it = iter(refs)
    x_ref = next(it)
    y_ref = next(it) if has_y else None
    g_ref = next(it) if has_y else None
    nw_ref = next(it)
    sh_ref = next(it) if modulate else None
    sc_ref = next(it) if modulate else None
    xo_ref = next(it) if write_x else None
    v_ref = next(it)
    nw = nw_ref[...]
    for j in range(SLABS):
        x = _get_piece(x_ref, x_mode, j, rows, d)
        if has_y:
            y = _get_piece(y_ref, y_mode, j, rows, d).astype(F32)
            x = x + g_ref[0] * y
            if write_x:
                _put_piece(xo_ref, xo_mode, j, rows, d, x)
        v = x * lax.rsqrt(jnp.mean(x * x, axis=-1, keepdims=True) + EPS) * nw
        if modulate:
            v = v * (1.0 + sc_ref[0]) + sh_ref[0]
        _put_piece(v_ref, v_mode, j, rows, d, v)


def _resnorm(x, y, gate, norm_w, shift, scale, *, n_batch, seq, x_mode="row",
             y_mode="row", xo_mode="row", v_mode="row", v_dtype=BF16, write_x=True,
             x_tile0=0, y_tile0=0, vec_index=None, name="resnorm"):
    d = x.shape[1]
    n = n_batch * seq
    if vec_index is None:
        vec_index = lambda b, t: b
    has_y = y is not None
    write_x = write_x and has_y
    modulate = shift is not None
    grows = seq // GRID_W
    tiles_per_seq = seq // ROW_TILE
    slab_used = "slab" in (x_mode, y_mode, xo_mode, v_mode)
    rows = grows if slab_used else ROW_TILE // SLABS
    if slab_used:
        assert grows * SLABS == ROW_TILE

    def spec(mode, tile0=0):
        if mode == "slab":
            return pl.BlockSpec((grows, SLABS * d), lambda b, t: (b, t))
        return pl.BlockSpec((ROW_TILE, d), lambda b, t: (b * tiles_per_seq + t + tile0, 0))

    def view(a, mode):
        return a.reshape(a.shape[0] // GRID_W, GRID_W * d) if mode == "slab" else a

    vec = pl.BlockSpec((1, 1, d), lambda b, t: (vec_index(b, t), 0, 0))
    in_specs = [spec(x_mode, x_tile0)]
    args = [view(x, x_mode)]
    if has_y:
        in_specs += [spec(y_mode, y_tile0), vec]
        args += [view(y, y_mode), gate]
    in_specs.append(pl.BlockSpec((1, d), lambda b, t: (0, 0)))
    args.append(norm_w.reshape(1, d))
    if modulate:
        in_specs += [vec, vec]
        args += [shift, scale]
    out_shape, out_specs = [], []
    def out_struct(mode, dtype):
        shape = (n // GRID_W, GRID_W * d) if mode == "slab" else (n, d)
        return jax.ShapeDtypeStruct(shape, dtype)

    if write_x:
        out_shape.append(out_struct(xo_mode, F32))
        out_specs.append(spec(xo_mode))
    out_shape.append(out_struct(v_mode, v_dtype))
    out_specs.append(spec(v_mode))
    outs = pl.pallas_call(
        functools.partial(_resnorm_kernel, has_y=has_y, write_x=write_x, x_mode=x_mode, y_mode=y_mode,
                          xo_mode=xo_mode, v_mode=v_mode, modulate=modulate, rows=rows, d=d),
        out_shape=out_shape,
        grid=(n_batch, tiles_per_seq),
        in_specs=in_specs,
        out_specs=out_specs,
        compiler_params=_cparams(2),
        name=name,
    )(*args)
    outs = [o.reshape(n, d) for o in outs]
    return (outs[0], outs[1]) if write_x else (None, outs[0])


S5_T = 128
S5_CB = 128
S5_NS = (S5_CB // S5_GROUP) * S5_STATE


def _reverse_rows(v_bf16):
    t = v_bf16.shape[0]
    r = lax.broadcasted_iota(jnp.int32, (t, t), 0)
    c = lax.broadcasted_iota(jnp.int32, (t, t), 1)
    flip = jnp.where(r + c == t - 1, 1.0, 0.0).astype(BF16)
    return _dot(flip, v_bf16)


def _s5_prep_kernel(xf_ref, xb_ref, nw_ref, sh_ref, sc_ref, o_ref, *, nb, d):
    nw = nw_ref[...]
    for b in range(nb):
        for rev, x_ref in ((False, xf_ref), (True, xb_ref)):
            x = x_ref[b]
            v = x * lax.rsqrt(jnp.mean(x * x, axis=-1, keepdims=True) + EPS) * nw
            v = (v * (1.0 + sc_ref[b]) + sh_ref[b]).astype(BF16)
            if rev:
                v = _reverse_rows(v).astype(BF16)
            k = b + (nb if rev else 0)
            o_ref[:, k * d:(k + 1) * d] = v


def _s5_prep(x3, norm_w, shift, scale):
    nb, s, d = x3.shape
    nt = s // S5_T
    return pl.pallas_call(
        functools.partial(_s5_prep_kernel, nb=nb, d=d),
        out_shape=jax.ShapeDtypeStruct((s, 2 * nb * d), BF16),
        grid=(nt,),
        in_specs=[pl.BlockSpec((nb, S5_T, d), lambda t: (0, t, 0)),
                  pl.BlockSpec((nb, S5_T, d), lambda t: (0, nt - 1 - t, 0)),
                  pl.BlockSpec((1, d), lambda t: (0, 0)),
                  pl.BlockSpec((nb, 1, d), lambda t: (0, 0, 0)),
                  pl.BlockSpec((nb, 1, d), lambda t: (0, 0, 0))],
        out_specs=pl.BlockSpec((S5_T, 2 * nb * d), lambda t: (t, 0)),
        compiler_params=_cparams(1),
        name="s5_prep",
    )(x3, x3, norm_w.reshape(1, d), shift, scale)


S5_BPS = 2


def _s5_scan_kernel(u_ref, bc_ref, are_ref, aim_ref, cc_ref, s0_ref, y_ref, sf_ref,
                    bu_ref, st_ref, *, n_tiles):
    i = pl.program_id(1)
    ns = S5_NS

    @pl.when(i == 0)
    def _():
        st_ref[...] = s0_ref[...]

    row = lax.broadcasted_iota(jnp.int32, (u_ref.shape[0], S5_CB), 0)
    is_fwd = jnp.bitwise_and(row, SUBLANES - 1) < (SUBLANES // 2)
    for blk in range(S5_BPS):
        u = u_ref[:, blk * S5_CB:(blk + 1) * S5_CB].astype(F32)
        lhs = jnp.concatenate([jnp.where(is_fwd, u, 0.0), jnp.where(is_fwd, 0.0, u)], axis=1)
        bu_ref[blk] = _dot(lhs.astype(BF16), bc_ref[blk])
    coef = [(are_ref[blk], aim_ref[blk]) for blk in range(S5_BPS)]

    def step(t, carry):
        r = pl.multiple_of(t * SUBLANES, SUBLANES)
        out = []
        for blk in range(S5_BPS):
            a_re, a_im = coef[blk]
            s_re, s_im = carry[2 * blk], carry[2 * blk + 1]
            n_re = a_re * s_re - a_im * s_im + bu_ref[blk, pl.ds(r, SUBLANES), 0:ns]
            n_im = a_re * s_im + a_im * s_re + bu_ref[blk, pl.ds(r, SUBLANES), ns:2 * ns]
            bu_ref[blk, pl.ds(r, SUBLANES), 0:ns] = n_re
            bu_ref[blk, pl.ds(r, SUBLANES), ns:2 * ns] = n_im
            out += [n_re, n_im]
        return tuple(out)

    init = []
    for blk in range(S5_BPS):
        init += [st_ref[blk, :, 0:ns], st_ref[blk, :, ns:2 * ns]]
    fin = lax.fori_loop(0, S5_T, step, tuple(init))
    for blk in range(S5_BPS):
        st_ref[blk, :, 0:ns] = fin[2 * blk]
        st_ref[blk, :, ns:2 * ns] = fin[2 * blk + 1]
        y2 = _dot(bu_ref[blk].astype(BF16), cc_ref[blk])
        y_ref[:, blk * S5_CB:(blk + 1) * S5_CB] = jnp.where(is_fwd, y2[:, 0:S5_CB],
                                                            y2[:, S5_CB:2 * S5_CB])

    @pl.when(i == n_tiles - 1)
    def _():
        sf_ref[...] = st_ref[...]


def _s5_scan(u2r, bcat, a_re, a_im, ccat, s0):
    rows, d = u2r.shape
    nblk = d // S5_CB
    tr = S5_T * SUBLANES
    n_tiles = rows // tr
    ns2 = 2 * S5_NS
    w = S5_BPS * S5_CB
    per_block = lambda j, i: (j, 0, 0)
    return pl.pallas_call(
        functools.partial(_s5_scan_kernel, n_tiles=n_tiles),
        out_shape=(jax.ShapeDtypeStruct((rows, d), F32),
                   jax.ShapeDtypeStruct((nblk, SUBLANES, ns2), F32)),
        grid=(nblk // S5_BPS, n_tiles),
        in_specs=[pl.BlockSpec((tr, w), lambda j, i: (i, j)),
                  pl.BlockSpec((S5_BPS, 2 * S5_CB, ns2), per_block),
                  pl.BlockSpec((S5_BPS, SUBLANES, S5_NS), per_block),
                  pl.BlockSpec((S5_BPS, SUBLANES, S5_NS), per_block),
                  pl.BlockSpec((S5_BPS, ns2, 2 * S5_CB), per_block),
                  pl.BlockSpec((S5_BPS, SUBLANES, ns2), per_block)],
        out_specs=(pl.BlockSpec((tr, w), lambda j, i: (i, j)),
                   pl.BlockSpec((S5_BPS, SUBLANES, ns2), per_block)),
        scratch_shapes=[pltpu.VMEM((S5_BPS, tr, ns2), F32),
                        pltpu.VMEM((S5_BPS, SUBLANES, ns2), F32)],
        compiler_params=_cparams(2),
        name="s5_scan",
    )(u2r, bcat, a_re, a_im, ccat, s0)


def _s5_out_kernel(yf_ref, yb_ref, u_ref, skip_ref, o_ref):
    yb = yb_ref[...]
    hi = yb.astype(BF16)
    lo = (yb - hi.astype(F32)).astype(BF16)
    y = (skip_ref[...] * u_ref[...].astype(F32) + yf_ref[...]
         + _reverse_rows(hi) + _reverse_rows(lo))
    o_ref[...] = jax.nn.gelu(y).astype(o_ref.dtype)


def _s5_out(y2, u2, skip, nb, d):
    s = y2.shape[0]
    nt = s // S5_T
    return pl.pallas_call(
        _s5_out_kernel,
        out_shape=jax.ShapeDtypeStruct((nb * s, d), BF16),
        grid=(nb, nt),
        in_specs=[pl.BlockSpec((S5_T, d), lambda b, t: (t, b)),
                  pl.BlockSpec((S5_T, d), lambda b, t: (nt - 1 - t, nb + b)),
                  pl.BlockSpec((S5_T, d), lambda b, t: (t, b)),
                  pl.BlockSpec((1, d), lambda b, t: (0, 0))],
        out_specs=pl.BlockSpec((S5_T, d), lambda b, t: (b * nt + t, 0)),
        compiler_params=_cparams(2),
        name="s5_out",
    )(y2, y2, u2, skip.reshape(1, d))


def _s5_pack_params(lam_re, lam_im, log_step, b_re, b_im, c_re, c_im, nb):
    f32 = F32
    g = lam_re.shape[1]
    gpb = S5_CB // S5_GROUP
    nblk = g // gpb
    eye = jnp.eye(gpb, dtype=f32)
    a_re_rows, a_im_rows, b_parts, c_parts = [], [], [], []
    for dr in range(2):
        lr, li = lam_re[dr].astype(f32), lam_im[dr].astype(f32)
        br, bi = b_re[dr].astype(f32), b_im[dr].astype(f32)
        dt = jnp.exp(log_step[dr].astype(f32))[:, None]
        mag = jnp.exp(lr * dt)
        abar_re = mag * jnp.cos(li * dt)
        abar_im = mag * jnp.sin(li * dt)
        num_re = abar_re - 1.0
        num_im = abar_im
        den = lr * lr + li * li
        f_re = (num_re * lr + num_im * li) / den
        f_im = (num_im * lr - num_re * li) / den
        bbar_re = f_re[..., None] * br - f_im[..., None] * bi
        bbar_im = f_re[..., None] * bi + f_im[..., None] * br
        a_re_rows.append(jnp.broadcast_to(abar_re.reshape(nblk, 1, S5_NS), (nblk, nb, S5_NS)))
        a_im_rows.append(jnp.broadcast_to(abar_im.reshape(nblk, 1, S5_NS), (nblk, nb, S5_NS)))

        def blockdiag_in(bb):
            b4 = bb.reshape(nblk, gpb, S5_STATE, S5_GROUP)
            return jnp.einsum('jgpk,gh->jgkhp', b4, eye).reshape(nblk, S5_CB, S5_NS)

        def blockdiag_out(cc):
            c4 = cc.astype(f32).reshape(nblk, gpb, S5_GROUP, S5_STATE)
            return jnp.einsum('jgkp,gh->jgphk', c4, eye).reshape(nblk, S5_NS, S5_CB)

        b_parts.append(jnp.concatenate([blockdiag_in(bbar_re), blockdiag_in(bbar_im)], axis=2))
        c_parts.append(jnp.concatenate([blockdiag_out(c_re[dr]), -blockdiag_out(c_im[dr])], axis=1))
    a_re = jnp.concatenate(a_re_rows, axis=1)
    a_im = jnp.concatenate(a_im_rows, axis=1)
    bcat = jnp.concatenate(b_parts, axis=1).astype(BF16)
    ccat = jnp.concatenate(c_parts, axis=2).astype(BF16)
    return bcat, a_re, a_im, ccat


def _conv_kernel(x_ref, w_ref, b_ref, o_ref, pad_ref, *, seq):
    halo = SUBLANES
    zeros = jnp.zeros((halo, pad_ref.shape[1]), F32)
    pad_ref[0:halo, :] = zeros
    pad_ref[halo + seq:2 * halo + seq, :] = zeros
    pad_ref[halo:halo + seq, :] = x_ref[...]
    acc = jnp.zeros((seq, pad_ref.shape[1]), F32) + b_ref[...]
    for k in range(SSD_CONV):
        off = halo + k - SSD_CONV // 2
        acc = acc + w_ref[k:k + 1, :] * pad_ref[off:off + seq, :]
    o_ref[...] = _silu(acc).astype(o_ref.dtype)


def _conv_silu(xbc, conv_w, conv_b, seq, row_block_offset, n_seq, out_dtype=BF16):
    _, c = xbc.shape
    tc = 512
    return pl.pallas_call(
        functools.partial(_conv_kernel, seq=seq),
        out_shape=jax.ShapeDtypeStruct((n_seq * seq, c), out_dtype),
        grid=(n_seq, c // tc),
        in_specs=[pl.BlockSpec((seq, tc), lambda b, j: (b + row_block_offset, j)),
                  pl.BlockSpec((SSD_CONV, tc), lambda b, j: (0, j)),
                  pl.BlockSpec((1, tc), lambda b, j: (0, j))],
        out_specs=pl.BlockSpec((seq, tc), lambda b, j: (b, j)),
        scratch_shapes=[pltpu.VMEM((seq + 2 * SUBLANES, tc), F32)],
        compiler_params=_cparams(2),
        name="ssd_conv",
    )(xbc, conv_w, conv_b.reshape(1, c))


HPG = 8
GCH = HPG * SSD_HEADDIM
DTC = 2 * HPG


def _ssd_kernel(xl_ref, btl_ref, cl_ref, dtl_ref, dttl_ref,
                xc_ref, btc_ref, dtc_ref, dttc_ref,
                z_ref, acol_ref, arow_ref, skip_ref, nw_ref, o_ref,
                htf_ref, htb_ref, yf_ref, yb_ref, *, n_lat, n_ctx):
    q = SSD_CHUNK
    r_i = lax.broadcasted_iota(jnp.int32, (q, q), 0)
    c_i = lax.broadcasted_iota(jnp.int32, (q, q), 1)
    tril = jnp.where(c_i <= r_i, 1.0, 0.0).astype(BF16)
    triu = jnp.where(r_i <= c_i, 1.0, 0.0).astype(BF16)
    lower = c_i <= r_i
    upper = c_i >= r_i
    lane = lax.broadcasted_iota(jnp.int32, (q, 2 * SSD_HEADDIM), 1)
    left = lane < SSD_HEADDIM
    e_r = lax.broadcasted_iota(jnp.int32, (DTC, GCH), 0)
    e_c = lax.broadcasted_iota(jnp.int32, (DTC, GCH), 1)
    a_row = arow_ref[0]
    a_col = acol_ref[0]

    def chunk(x_ref, bt_ref, c_ref, dt_ref, dtt_ref, r0, dr, ht_ref, y_ref):
        head_of_ch = lax.shift_right_logical(e_c, int(math.log2(SSD_HEADDIM)))
        expand = jnp.where(head_of_ch + dr * HPG == e_r, 1.0, 0.0).astype(BF16)
        x = x_ref[pl.ds(r0, q), :].astype(F32)
        bt = bt_ref[0, :, pl.ds(r0, q)]
        dt = dt_ref[0, pl.ds(r0, q), :]
        dtt = dtt_ref[0, :, pl.ds(r0, q)]
        la = dt * a_row
        lat = dtt * a_col
        cs = _dot_exact_lhs(tril, la)
        cst = _dot_exact_rhs(lat, triu)
        total = cs[q - 1:q, :]
        if dr == 1:
            rk, rkt = cs - la, cst - lat
            e_off, e_state = jnp.exp(total - rk), jnp.exp(rk)
        else:
            rk, rkt = cs, cst
            e_off, e_state = jnp.exp(rk), jnp.exp(total - rk)
        tot8 = jnp.broadcast_to(jnp.exp(total), (SUBLANES, DTC))
        pieces3 = [p.astype(F32) for p in _split3(tot8)]
        wo = _dot(jnp.concatenate([e_off] + pieces3, axis=0).astype(BF16), expand)
        w_off = wo[0:q, :]
        e_tot = (wo[q:q + 1, :] + wo[q + SUBLANES:q + SUBLANES + 1, :]
                 + wo[q + 2 * SUBLANES:q + 2 * SUBLANES + 1, :])
        w_state = _dot(e_state.astype(BF16), expand)
        dtx = _dot(dt.astype(BF16), expand)
        xd = x * dtx
        h_old = ht_ref[...]
        ht_ref[...] = h_old * e_tot + _dot(bt, (xd * w_state).astype(BF16))
        if y_ref is None:
            return
        cm = c_ref[pl.ds(r0, q), :]
        cb = _dot(cm, bt)
        xdb = xd.astype(BF16)
        pieces = []
        for pair in range(HPG // 2):
            ms = []
            for hh in (2 * pair, 2 * pair + 1):
                col = dr * HPG + hh
                colv = jnp.broadcast_to(rk[:, col:col + 1], (q, q))
                rowv = rkt[col:col + 1, :]
                if dr == 0:
                    seg = jnp.where(lower, colv - rowv, -1e30)
                else:
                    seg = jnp.where(upper, rowv - colv, -1e30)
                ms.append((cb * jnp.exp(seg)).astype(BF16))
            xp = xdb[:, pair * 2 * SSD_HEADDIM:(pair + 1) * 2 * SSD_HEADDIM]
            zero = jnp.zeros_like(xp)
            rhs = jnp.concatenate([jnp.where(left, xp, zero), jnp.where(left, zero, xp)], axis=0)
            pieces.append(_dot(jnp.concatenate(ms, axis=1), rhs))
        y_ref[pl.ds(r0, q), :] = (jnp.concatenate(pieces, axis=1)
                                  + _dot(cm, h_old.astype(BF16)) * w_off)

    htf_ref[...] = jnp.zeros_like(htf_ref)
    htb_ref[...] = jnp.zeros_like(htb_ref)

    def ctx_body(k, carry):
        rf = pl.multiple_of(k * q, q)
        rb = pl.multiple_of((n_ctx - 1 - k) * q, q)
        chunk(xc_ref, btc_ref, None, dtc_ref, dttc_ref, rf, 0, htf_ref, None)
        chunk(xc_ref, btc_ref, None, dtc_ref, dttc_ref, rb, 1, htb_ref, None)
        return carry

    lax.fori_loop(0, n_ctx, ctx_body, 0)

    def lat_body(k, carry):
        rf = pl.multiple_of(k * q, q)
        rb = pl.multiple_of((n_lat - 1 - k) * q, q)
        chunk(xl_ref, btl_ref, cl_ref, dtl_ref, dttl_ref, rf, 0, htf_ref, yf_ref)
        chunk(xl_ref, btl_ref, cl_ref, dtl_ref, dttl_ref, rb, 1, htb_ref, yb_ref)
        return carry

    lax.fori_loop(0, n_lat, lat_body, 0, unroll=2)

    def finish(k, carry):
        r0 = pl.multiple_of(k * q, q)
        x = xl_ref[pl.ds(r0, q), :].astype(F32)
        y = yf_ref[pl.ds(r0, q), :] + yb_ref[pl.ds(r0, q), :] + skip_ref[...] * x
        v = y * _silu(z_ref[pl.ds(r0, q), :].astype(F32))
        v = v * lax.rsqrt(jnp.mean(v * v, axis=-1, keepdims=True) + EPS) * nw_ref[...]
        o_ref[pl.ds(r0, q), :] = v.astype(o_ref.dtype)
        return carry

    lax.fori_loop(0, n_lat, finish, 0)


def _ssd_scan(xc_l, xc_c, bt_l, bt_c, z, dtg, dtgt, a_col, a_row, skip, norm_w, nb, seq, ctx_len):
    n_lat, n_ctx = seq // SSD_CHUNK, ctx_len // SSD_CHUNK
    inner = SSD_GROUPS * GCH
    coff = inner // SSD_STATE + SSD_GROUPS
    cb0 = nb * seq // ctx_len
    in_specs = [
        pl.BlockSpec((seq, GCH), lambda b, g: (b, g)),
        pl.BlockSpec((1, SSD_STATE, seq), lambda b, g: (g, 0, b)),
        pl.BlockSpec((seq, SSD_STATE), lambda b, g: (b, coff + g)),
        pl.BlockSpec((1, seq, DTC), lambda b, g: (g, b, 0)),
        pl.BlockSpec((1, DTC, seq), lambda b, g: (g, 0, b)),
        pl.BlockSpec((ctx_len, GCH), lambda b, g: (b, g)),
        pl.BlockSpec((1, SSD_STATE, ctx_len), lambda b, g: (g, 0, b)),
        pl.BlockSpec((1, ctx_len, DTC), lambda b, g: (g, cb0 + b, 0)),
        pl.BlockSpec((1, DTC, ctx_len), lambda b, g: (g, 0, cb0 + b)),
        pl.BlockSpec((seq, GCH), lambda b, g: (b, g)),
        pl.BlockSpec((1, DTC, 1), lambda b, g: (g, 0, 0)),
        pl.BlockSpec((1, 1, DTC), lambda b, g: (g, 0, 0)),
        pl.BlockSpec((1, GCH), lambda b, g: (0, g)),
        pl.BlockSpec((1, GCH), lambda b, g: (0, g)),
    ]
    return pl.pallas_call(
        functools.partial(_ssd_kernel, n_lat=n_lat, n_ctx=n_ctx),
        out_shape=jax.ShapeDtypeStruct((nb * seq, inner), BF16),
        grid=(nb, SSD_GROUPS),
        in_specs=in_specs,
        out_specs=pl.BlockSpec((seq, GCH), lambda b, g: (b, g)),
        scratch_shapes=[pltpu.VMEM((SSD_STATE, GCH), F32), pltpu.VMEM((SSD_STATE, GCH), F32),
                        pltpu.VMEM((seq, GCH), F32), pltpu.VMEM((seq, GCH), F32)],
        compiler_params=_cparams(2),
        name="ssd_scan",
    )(xc_l, bt_l, xc_l, dtg, dtgt, xc_c, bt_c, dtg, dtgt, z, a_col, a_row, skip, norm_w)


def _expert_kernel(be_ref, nu_ref, first_ref, slot_ref, next_ref, x_ref, wg_hbm, wu_hbm, wd_hbm,
                   o_ref, stage_g, stage_u, stage_d, wgb, wub, wdb, sem, *, layer):
    i = pl.program_id(0)

    def weight_copies(e, s):
        return (pltpu.make_async_copy(wg_hbm.at[layer, e], stage_g.at[s], sem.at[s, 0]),
                pltpu.make_async_copy(wu_hbm.at[layer, e], stage_u.at[s], sem.at[s, 1]),
                pltpu.make_async_copy(wd_hbm.at[layer, e], stage_d.at[s], sem.at[s, 2]))

    @pl.when(i == 0)
    def _():
        for c in weight_copies(be_ref[0], 0):
            c.start()

    @pl.when(jnp.logical_and(i < nu_ref[0], first_ref[i] == 1))
    def _():
        s = slot_ref[i]

        @pl.when(next_ref[i] >= 0)
        def _():
            for c in weight_copies(next_ref[i], 1 - s):
                c.start()

        for c in weight_copies(be_ref[i], s):
            c.wait()
        wgb[...] = stage_g[s].astype(BF16)
        wub[...] = stage_u[s].astype(BF16)
        wdb[...] = stage_d[s].astype(BF16)

    @pl.when(i < nu_ref[0])
    def _():
        lo, hi = _unpack_rows(x_ref[...])
        x = jnp.concatenate([lo.astype(BF16), hi.astype(BF16)], axis=1)
        h = _silu(_dot(x, wgb[...])) * _dot(x, wub[...])
        o_ref[...] = _pack_rows(_dot(h.astype(BF16), wdb[...]))

    @pl.when(i >= nu_ref[0])
    def _():
        o_ref[...] = jnp.zeros_like(o_ref)


def _experts(x_sorted, block_e, n_used, first, slot, next_e, w_gate, w_up, w_down, layer):
    cap, d = x_sorted.shape
    ff = w_gate.shape[3]
    n_blocks = cap // EXPERT_BLOCK

    def row_block(i, be, nu, fi, sl, nx):
        return (jnp.minimum(i, nu[0] - 1), 0)

    hbm = pl.BlockSpec(memory_space=pl.ANY)
    grid_spec = pltpu.PrefetchScalarGridSpec(
        num_scalar_prefetch=5,
        grid=(n_blocks,),
        in_specs=[pl.BlockSpec((EXPERT_BLOCK, d), row_block), hbm, hbm, hbm],
        out_specs=pl.BlockSpec((EXPERT_BLOCK, d), lambda i, be, nu, fi, sl, nx: (i, 0)),
        scratch_shapes=[pltpu.VMEM((2, 2 * d, ff), F32), pltpu.VMEM((2, 2 * d, ff), F32),
                        pltpu.VMEM((2, ff, 2 * d), F32),
                        pltpu.VMEM((2 * d, ff), BF16), pltpu.VMEM((2 * d, ff), BF16),
                        pltpu.VMEM((ff, 2 * d), BF16), pltpu.SemaphoreType.DMA((2, 3))])
    return pl.pallas_call(
        functools.partial(_expert_kernel, layer=layer),
        out_shape=jax.ShapeDtypeStruct((cap, d), jnp.uint32),
        grid_spec=grid_spec,
        compiler_params=_cparams(1),
        name="moe_experts",
    )(block_e, n_used, first, slot, next_e, x_sorted, w_gate, w_up, w_down)


DISPATCH_T = 256
COMBINE_T = 128


HIGH_HALF = 0xFFFF0000


def _pack_rows(a):
    half = a.shape[1] // 2
    lo = lax.bitcast_convert_type(a[:, :half].astype(BF16).astype(F32), jnp.uint32)
    hi = lax.bitcast_convert_type(a[:, half:].astype(BF16).astype(F32), jnp.uint32)
    return jnp.bitwise_or(jnp.bitwise_and(hi, jnp.uint32(HIGH_HALF)),
                          lax.shift_right_logical(lo, jnp.uint32(16)))


def _unpack_rows(w):
    lo = lax.bitcast_convert_type(lax.shift_left(w, jnp.uint32(16)), F32)
    hi = lax.bitcast_convert_type(jnp.bitwise_and(w, jnp.uint32(HIGH_HALF)), F32)
    return lo, hi


def _dispatch_kernel(pe_ref, pd_ref, dest_ref, v_ref, xs_ref, zero_buf, row_buf, zero_sem, sem, *,
                     n_tiles):
    i = pl.program_id(0)
    t_rows = v_ref.shape[0]
    slot = i % 2

    @pl.when(pl.program_id(0) == 0)
    def _():
        zero_buf[...] = jnp.zeros_like(zero_buf)

        def fill(e, carry):
            @pl.when(pd_ref[e] > 0)
            def _():
                start = pl.multiple_of(pe_ref[e] - EXPERT_BLOCK, EXPERT_BLOCK)
                pltpu.make_async_copy(zero_buf, xs_ref.at[pl.ds(start, EXPERT_BLOCK)], zero_sem).start()
            return carry

        def drain(e, carry):
            @pl.when(pd_ref[e] > 0)
            def _():
                pltpu.make_async_copy(zero_buf, xs_ref.at[pl.ds(0, EXPERT_BLOCK)], zero_sem).wait()
            return carry

        lax.fori_loop(0, N_EXPERTS, fill, 0)
        lax.fori_loop(0, N_EXPERTS, drain, 0)

        n_blocks = xs_ref.shape[0] // EXPERT_BLOCK
        first_unused = pe_ref[N_EXPERTS - 1] // EXPERT_BLOCK

        def fill_tail(b, carry):
            start = pl.multiple_of(b * EXPERT_BLOCK, EXPERT_BLOCK)
            pltpu.make_async_copy(zero_buf, xs_ref.at[pl.ds(start, EXPERT_BLOCK)], zero_sem).start()
            return carry

        def drain_tail(b, carry):
            pltpu.make_async_copy(zero_buf, xs_ref.at[pl.ds(0, EXPERT_BLOCK)], zero_sem).wait()
            return carry

        lax.fori_loop(first_unused, n_blocks, fill_tail, 0)
        lax.fori_loop(first_unused, n_blocks, drain_tail, 0)

    def wait_slot(s):
        for k in range(TOP_K):
            pltpu.make_async_copy(row_buf.at[s], xs_ref.at[pl.ds(0, t_rows)], sem.at[s]).wait()

    @pl.when(i >= 2)
    def _():
        wait_slot(slot)

    row_buf[slot] = _pack_rows(v_ref[...])

    def issue(t, carry):
        for k in range(TOP_K):
            pltpu.make_async_copy(row_buf.at[slot, pl.ds(t, 1)],
                                  xs_ref.at[pl.ds(dest_ref[k, t], 1)],
                                  sem.at[slot]).start(priority=k % 2)
        return carry

    lax.fori_loop(0, t_rows, issue, 0)

    @pl.when(i == n_tiles - 1)
    def _():
        wait_slot(slot)
        if n_tiles >= 2:
            wait_slot(1 - slot)


def _dispatch(v, dest, pad_end, padded, cap):
    n, d = v.shape
    n_tiles = n // DISPATCH_T
    grid_spec = pltpu.PrefetchScalarGridSpec(
        num_scalar_prefetch=2,
        grid=(n_tiles,),
        in_specs=[pl.BlockSpec((TOP_K, DISPATCH_T), lambda i, pe, pd: (0, i),
                               memory_space=pltpu.SMEM),
                  pl.BlockSpec((DISPATCH_T, d), lambda i, pe, pd: (i, 0))],
        out_specs=pl.BlockSpec(memory_space=pl.ANY),
        scratch_shapes=[pltpu.VMEM((EXPERT_BLOCK, d // 2), jnp.uint32),
                        pltpu.VMEM((2, DISPATCH_T, d // 2), jnp.uint32),
                        pltpu.SemaphoreType.DMA(()), pltpu.SemaphoreType.DMA((2,))])
    return pl.pallas_call(
        functools.partial(_dispatch_kernel, n_tiles=n_tiles),
        out_shape=jax.ShapeDtypeStruct((cap, d // 2), jnp.uint32),
        grid_spec=grid_spec,
        compiler_params=_cparams(1),
        name="moe_dispatch",
    )(pad_end, padded, dest, v)


def _combine_kernel(dest_ref, next_dest_ref, y_ref, gate_ref, sh_ref, o_ref, buf, sem, *, n_tiles):
    i = pl.program_id(0)
    t_rows = o_ref.shape[0]
    slot = i % 2

    def issue_tile(idx_ref, s):
        def issue(t, carry):
            for k in range(TOP_K):
                pltpu.make_async_copy(y_ref.at[pl.ds(idx_ref[k, t], 1)],
                                      buf.at[s, k, pl.ds(t, 1)], sem.at[s]).start(priority=k % 2)
            return carry

        lax.fori_loop(0, t_rows, issue, 0)

    @pl.when(i == 0)
    def _():
        issue_tile(dest_ref, 0)

    @pl.when(i + 1 < n_tiles)
    def _():
        issue_tile(next_dest_ref, 1 - slot)

    for k in range(TOP_K):
        pltpu.make_async_copy(y_ref.at[pl.ds(0, t_rows)], buf.at[slot, k], sem.at[slot]).wait()
    half = o_ref.shape[1] // 2
    acc_lo = sh_ref[:, :half]
    acc_hi = sh_ref[:, half:]
    for k in range(TOP_K):
        lo, hi = _unpack_rows(buf[slot, k])
        g = gate_ref[:, k:k + 1]
        acc_lo = acc_lo + g * lo
        acc_hi = acc_hi + g * hi
    o_ref[:, :half] = acc_lo
    o_ref[:, half:] = acc_hi


def _combine(y_sorted, dest, gate_t, shared):
    n, d = shared.shape
    n_tiles = n // COMBINE_T
    idx_block = (TOP_K, COMBINE_T)
    return pl.pallas_call(
        functools.partial(_combine_kernel, n_tiles=n_tiles),
        out_shape=jax.ShapeDtypeStruct((n, d), F32),
        grid=(n_tiles,),
        in_specs=[pl.BlockSpec(idx_block, lambda i: (0, i), memory_space=pltpu.SMEM),
                  pl.BlockSpec(idx_block, lambda i: (0, jnp.minimum(i + 1, n_tiles - 1)),
                               memory_space=pltpu.SMEM),
                  pl.BlockSpec(memory_space=pl.ANY),
                  pl.BlockSpec((COMBINE_T, TOP_K), lambda i: (i, 0)),
                  pl.BlockSpec((COMBINE_T, d), lambda i: (i, 0))],
        out_specs=pl.BlockSpec((COMBINE_T, d), lambda i: (i, 0)),
        scratch_shapes=[pltpu.VMEM((2, TOP_K, COMBINE_T, d // 2), jnp.uint32),
                        pltpu.SemaphoreType.DMA((2,))],
        compiler_params=_cparams(1),
        name="moe_combine",
    )(dest, dest, y_sorted, gate_t, shared)


ROUTE_T = 512
GROUP_SIZE = N_EXPERTS // N_EXPERT_GROUPS


def _route_kernel(v_ref, rwh_ref, rwl_ref, bias_ref, eid_ref, rank_ref, gate_ref, cnt_ref,
                  carry_ref, *, n_tiles):
    i = pl.program_id(0)
    t = v_ref.shape[0]
    ng, gs = N_EXPERT_GROUPS, GROUP_SIZE
    neg = -jnp.inf

    @pl.when(i == 0)
    def _():
        carry_ref[...] = jnp.zeros_like(carry_ref)

    v = v_ref[...].astype(BF16)
    nt_dims = (((1,), (1,)), ((), ()))
    logits = (lax.dot_general(rwh_ref[...], v, nt_dims, preferred_element_type=F32)
              + lax.dot_general(rwl_ref[...], v, nt_dims, preferred_element_type=F32))
    scores = jax.nn.sigmoid(logits)
    biased = scores + bias_ref[:, 0:1]
    x3 = biased.reshape(ng, gs, t)
    s3 = scores.reshape(ng, gs, t)
    mi = lax.broadcasted_iota(jnp.int32, (ng, gs, t), 1).astype(F32)
    fi = lax.broadcasted_iota(jnp.int32, (ng, gs, t), 0).astype(F32) * gs + mi
    gi = lax.broadcasted_iota(jnp.int32, (ng, 1, t), 0).astype(F32)

    m1 = jnp.max(x3, axis=1, keepdims=True)
    i1 = jnp.min(jnp.where(x3 == m1, mi, float(gs)), axis=1, keepdims=True)
    m2 = jnp.max(jnp.where(mi == i1, neg, x3), axis=1, keepdims=True)
    cur = m1 + m2
    gmask = jnp.zeros((ng, 1, t), F32)
    for _ in range(TOPK_GROUPS):
        gm = jnp.max(cur, axis=0, keepdims=True)
        idx = jnp.min(jnp.where(cur == gm, gi, float(ng)), axis=0, keepdims=True)
        hit = gi == idx
        gmask = jnp.where(hit, 1.0, gmask)
        cur = jnp.where(hit, neg, cur)

    cand = jnp.where(gmask > 0.0, x3, neg)
    sel = jnp.zeros((ng, gs, t), F32)
    eids = []
    for _ in range(TOP_K):
        m = jnp.max(jnp.max(cand, axis=1, keepdims=True), axis=0, keepdims=True)
        idx = jnp.where(cand == m, fi, float(N_EXPERTS))
        idx = jnp.min(jnp.min(idx, axis=1, keepdims=True), axis=0, keepdims=True)
        hit = fi == idx
        sel = jnp.where(hit, 1.0, sel)
        cand = jnp.where(hit, neg, cand)
        eids.append(idx)

    selr = sel.reshape(N_EXPERTS, t)
    r_i = lax.broadcasted_iota(jnp.int32, (t, t), 0)
    c_i = lax.broadcasted_iota(jnp.int32, (t, t), 1)
    before = jnp.where(r_i < c_i, 1.0, 0.0).astype(BF16)
    rank = _dot(selr.astype(BF16), before) + carry_ref[:, 0:1]
    carry_ref[...] = carry_ref[...] + jnp.sum(selr, axis=1, keepdims=True)
    rank3 = rank.reshape(ng, gs, t)

    gsel = sel * s3
    denom = jnp.sum(jnp.sum(gsel, axis=1, keepdims=True), axis=0, keepdims=True)
    gate3 = gsel / denom * ROUTED_SCALE

    def pick(a3, hit):
        return jnp.sum(jnp.sum(jnp.where(hit, a3, 0.0), axis=1, keepdims=True), axis=0,
                       keepdims=True).reshape(1, t)

    for k in range(TOP_K):
        hit = fi == eids[k]
        eid_ref[k:k + 1, :] = eids[k].reshape(1, t).astype(jnp.int32)
        rank_ref[k:k + 1, :] = pick(rank3, hit).astype(jnp.int32)
        gate_ref[k:k + 1, :] = pick(gate3, hit)

    @pl.when(i == n_tiles - 1)
    def _():
        cnt_ref[...] = carry_ref[...]


def _route(v, router_w, router_bias):
    n, d = v.shape
    n_tiles = n // ROUTE_T
    rwt = router_w.astype(F32).T
    rwh = rwt.astype(BF16)
    rwl = (rwt - rwh.astype(F32)).astype(BF16)
    bias = jnp.broadcast_to(router_bias.astype(F32)[:, None], (N_EXPERTS, LANES))
    slot = pl.BlockSpec((TOP_K, ROUTE_T), lambda i: (0, i))
    full = pl.BlockSpec((N_EXPERTS, d), lambda i: (0, 0))
    return pl.pallas_call(
        functools.partial(_route_kernel, n_tiles=n_tiles),
        out_shape=(jax.ShapeDtypeStruct((TOP_K, n), jnp.int32),
                   jax.ShapeDtypeStruct((TOP_K, n), jnp.int32),
                   jax.ShapeDtypeStruct((TOP_K, n), F32),
                   jax.ShapeDtypeStruct((N_EXPERTS, LANES), F32)),
        grid=(n_tiles,),
        in_specs=[pl.BlockSpec((ROUTE_T, d), lambda i: (i, 0)), full, full,
                  pl.BlockSpec((N_EXPERTS, LANES), lambda i: (0, 0))],
        out_specs=(slot, slot, slot, pl.BlockSpec((N_EXPERTS, LANES), lambda i: (0, 0))),
        scratch_shapes=[pltpu.VMEM((N_EXPERTS, LANES), F32)],
        compiler_params=_cparams(1),
        name="moe_route",
    )(v, rwh, rwl, bias)


def _slot_rows_kernel(ps_ref, eid_ref, rank_ref, o_ref):
    eid = eid_ref[...]
    acc = rank_ref[...]
    for e in range(N_EXPERTS):
        acc = acc + jnp.where(eid == e, ps_ref[e], 0)
    o_ref[...] = acc


def _slot_rows(eid, rank, pad_start):
    k, n = eid.shape
    whole = lambda i, ps: (0, 0)
    grid_spec = pltpu.PrefetchScalarGridSpec(
        num_scalar_prefetch=1, grid=(1,),
        in_specs=[pl.BlockSpec((k, n), whole), pl.BlockSpec((k, n), whole)],
        out_specs=pl.BlockSpec((k, n), whole))
    return pl.pallas_call(
        _slot_rows_kernel,
        out_shape=jax.ShapeDtypeStruct((k, n), jnp.int32),
        grid_spec=grid_spec,
        compiler_params=_cparams(1),
        name="moe_slot_rows",
    )(pad_start, eid, rank)


def _moe(v, layer, router_w, router_bias, w_gate, w_up, w_down, sw_gate, sw_up, sw_down):
    n, d = v.shape
    eid, rank, gate, cnt = _route(v, router_w, router_bias)
    nk = n * TOP_K
    cap = -(-nk // EXPERT_BLOCK) * EXPERT_BLOCK + N_EXPERTS * EXPERT_BLOCK
    n_blocks = cap // EXPERT_BLOCK
    counts = cnt[:, 0].astype(jnp.int32)
    padded = (counts + EXPERT_BLOCK - 1) // EXPERT_BLOCK * EXPERT_BLOCK
    pad_end = jnp.cumsum(padded)
    pad_start = pad_end - padded
    dest = _slot_rows(eid, rank, pad_start.astype(jnp.int32))
    block_start = jnp.arange(n_blocks, dtype=jnp.int32) * EXPERT_BLOCK
    block_e = jnp.minimum(jnp.sum((pad_end[None, :] <= block_start[:, None]).astype(jnp.int32), axis=1),
                          N_EXPERTS - 1).astype(jnp.int32)
    n_used = (pad_end[-1] // EXPERT_BLOCK).astype(jnp.int32).reshape(1)
    block_id = jnp.arange(n_blocks, dtype=jnp.int32)
    first = ((pad_start[block_e] == block_start) & (block_id < n_used[0])).astype(jnp.int32)
    slot = ((jnp.cumsum(first) - 1) % 2).astype(jnp.int32)
    experts = jnp.arange(N_EXPERTS, dtype=jnp.int32)
    later = (experts[None, :] > experts[:, None]) & (padded[None, :] > 0)
    next_of = jnp.min(jnp.where(later, experts[None, :], N_EXPERTS), axis=1)
    next_of = jnp.where(next_of == N_EXPERTS, -1, next_of).astype(jnp.int32)
    next_e = next_of[block_e]
    x_sorted = _dispatch(v, dest, pad_end.astype(jnp.int32), padded.astype(jnp.int32), cap)
    y_sorted = _experts(x_sorted, block_e, n_used, first, slot, next_e, w_gate, w_up, w_down, layer)
    hs = _matmul(v, [sw_gate, sw_up], [0, 0], sw_gate.shape[1], tn=sw_gate.shape[1], tm=512,
                 epilogue="swiglu", out_dtype=BF16, name="shared_up")
    shared = _matmul(hs, [sw_down], [0], d, tn=1024, tm=512, name="shared_down")
    return _combine(y_sorted, dest, gate.T, shared)


def kernel(x, c, ctx, c_ctx, ada_w, ada_b, norm1_w, norm2_w, s5_lambda_re, s5_lambda_im, s5_log_step, s5_b_re, s5_b_im, s5_c_re, s5_c_im, s5_d, s5_glu_w, s5_glu_b, ssd_in_w, ssd_conv_w, ssd_conv_b, ssd_dt_bias, ssd_a_log, ssd_d, ssd_norm_w, ssd_out_w, moe_router_w, moe_router_bias, moe_w_gate, moe_w_up, moe_w_down, shared_w_gate, shared_w_up, shared_w_down, final_norm_w):
    nb, seq, d = x.shape
    ctx_len = ctx.shape[1]
    n_lat = nb * seq
    n_ctx = nb * ctx_len

    cond = jnp.concatenate([c, c_ctx[None, :], jnp.zeros((SUBLANES - nb - 1, d), F32)], axis=0)
    mods = _ada(cond, ada_w, ada_b)

    def mod_vecs(layer, k):
        m = mods[layer, :, k * d:(k + 1) * d]
        lat = m[:nb].reshape(nb, 1, d)
        cx = jnp.broadcast_to(m[nb].reshape(1, 1, d), (nb, 1, d))
        return lat, cx

    x_lat = x.reshape(n_lat, d)
    x_ctx = ctx.reshape(n_ctx, d)

    sh_l, sh_c = mod_vecs(0, 0)
    sc_l, sc_c = mod_vecs(0, 1)
    u2_c = _s5_prep(ctx, norm1_w[0], sh_c, sc_c)
    u2_l = _s5_prep(x, norm1_w[0], sh_l, sc_l)
    bcat, a_re, a_im, ccat = _s5_pack_params(
        s5_lambda_re[0], s5_lambda_im[0], s5_log_step[0], s5_b_re[0], s5_b_im[0],
        s5_c_re[0], s5_c_im[0], nb)
    nblk = d // S5_CB
    s0 = jnp.zeros((nblk, SUBLANES, 2 * S5_NS), F32)
    y2_c, s_ctx = _s5_scan(u2_c.reshape(ctx_len * 2 * nb, d), bcat, a_re, a_im, ccat, s0)
    y2_l, _ = _s5_scan(u2_l.reshape(seq * 2 * nb, d), bcat, a_re, a_im, ccat, s_ctx)
    g_l = _s5_out(y2_l.reshape(seq, 2 * nb * d), u2_l, s5_d[0], nb, d)
    g_c = _s5_out(y2_c.reshape(ctx_len, 2 * nb * d), u2_c, s5_d[0], nb, d)
    g_all = jnp.concatenate([g_l, g_c], axis=0)
    half = s5_glu_w.shape[2] // 2
    tn = 1024
    glu = _matmul(g_all, [s5_glu_w[0], s5_glu_w[0]], [0, half // tn], half, tn=tn, tm=512,
                  biases=[s5_glu_b[0], s5_glu_b[0]], epilogue="glu", out_dtype=BF16, name="s5_glu")

    n_all = n_lat + n_ctx
    lat_tiles = seq // ROW_TILE

    def all_vecs(layer, k):
        return mods[layer, :nb + 1, k * d:(k + 1) * d].reshape(nb + 1, 1, d)

    def all_index(b, t):
        return jnp.minimum(t // lat_tiles, nb)

    xs0 = jnp.concatenate([x_lat, x_ctx], axis=0)
    xs1, v_all = _resnorm(xs0, glu, all_vecs(0, 2), norm2_w[0], all_vecs(0, 3), all_vecs(0, 4),
                          n_batch=1, seq=n_all, vec_index=all_index, v_dtype=F32)
    moe0 = _moe(v_all, 0, moe_router_w[0], moe_router_bias[0], moe_w_gate, moe_w_up,
                moe_w_down, shared_w_gate[0], shared_w_up[0], shared_w_down[0])

    g5_l, g5_c = mod_vecs(0, 5)
    sh_l, sh_c = mod_vecs(1, 0)
    sc_l, sc_c = mod_vecs(1, 1)
    x2_l, u_l = _resnorm(xs1, moe0, g5_l, norm1_w[1], sh_l, sc_l, n_batch=nb, seq=seq,
                         x_mode="slab", y_mode="slab", xo_mode="row", v_mode="row")
    ctx_tile0 = n_lat // ROW_TILE
    _, u_c = _resnorm(xs1, moe0, g5_c, norm1_w[1], sh_c, sc_c, n_batch=nb, seq=ctx_len,
                      write_x=False, x_tile0=ctx_tile0, y_tile0=ctx_tile0)
    u_all = jnp.concatenate([u_l, u_c], axis=0)
    in_w = ssd_in_w[0]
    inner = ssd_out_w.shape[1]
    conv_dim = ssd_conv_w.shape[2]
    heads = inner // SSD_HEADDIM
    tn = 1024
    z_all = _matmul(u_all, [in_w], [0], inner, tn=tn, tm=512, out_dtype=BF16, name="ssd_in_z")
    xbc = _matmul(u_all, [in_w], [inner // tn], conv_dim, tn=tn, tm=512, name="ssd_in_xbc")
    dt_bias = ssd_dt_bias[0].reshape(-1)
    dt_all = _matmul(u_all, [in_w], [(inner + conv_dim) // LANES], 2 * heads, tn=LANES, tm=512,
                     biases=[jnp.pad(dt_bias, (inner + conv_dim, 0))], epilogue="softplus",
                     name="ssd_in_dt")
    xc_l = _conv_silu(xbc, ssd_conv_w[0], ssd_conv_b[0], seq, 0, nb)
    xc_c = _conv_silu(xbc, ssd_conv_w[0], ssd_conv_b[0], ctx_len, n_lat // ctx_len, nb)
    rows = n_lat + n_ctx

    def b_transposed(xc):
        bm = xc[:, inner:inner + SSD_GROUPS * SSD_STATE]
        return bm.reshape(xc.shape[0], SSD_GROUPS, SSD_STATE).transpose(1, 2, 0)

    dtg = dt_all.reshape(rows, 2, SSD_GROUPS, HPG).transpose(2, 0, 1, 3).reshape(SSD_GROUPS, rows, DTC)
    dtgt = dtg.transpose(0, 2, 1)
    a = -jnp.exp(ssd_a_log[0].astype(F32))
    a_g = a.reshape(2, SSD_GROUPS, HPG).transpose(1, 0, 2).reshape(SSD_GROUPS, DTC)
    skip = jnp.repeat(ssd_d[0].astype(F32), SSD_HEADDIM).reshape(1, inner)
    yn = _ssd_scan(xc_l, xc_c, b_transposed(xc_l), b_transposed(xc_c), z_all, dtg, dtgt,
                   a_g.reshape(SSD_GROUPS, DTC, 1),
                   a_g.reshape(SSD_GROUPS, 1, DTC), skip, ssd_norm_w[0].reshape(1, inner),
                   nb, seq, ctx_len)
    y_lat = _matmul(yn, [ssd_out_w[0]], [0], d, tn=512, tm=512, out_dtype=BF16, name="ssd_out")

    g2_l, _ = mod_vecs(1, 2)
    sh4_l, _ = mod_vecs(1, 3)
    sc4_l, _ = mod_vecs(1, 4)
    x3_l, v_l = _resnorm(x2_l, y_lat, g2_l, norm2_w[1], sh4_l, sc4_l, n_batch=nb, seq=seq,
                         v_dtype=F32)
    moe1 = _moe(v_l, 1, moe_router_w[1], moe_router_bias[1], moe_w_gate, moe_w_up,
                moe_w_down, shared_w_gate[1], shared_w_up[1], shared_w_down[1])
    g5_l, _ = mod_vecs(1, 5)
    _, out = _resnorm(x3_l, moe1, g5_l, final_norm_w, None, None, n_batch=nb, seq=seq, v_dtype=F32,
                      write_x=False, v_mode="slab")
    return out.reshape(nb, seq, d)
```
